```python
import math
import jax, jax.numpy as jnp
from jax import lax
import numpy as np

D_MODEL = 1024
BATCH = 32
SEQ = 2048
DEPTH = 4

CTX_LEN = 256
GRID_W = 64
N_MIXERS = 2
N_HEADS = 16
N_KV_HEADS = 4
HEAD_DIM = D_MODEL // N_HEADS
GROUP = N_HEADS // N_KV_HEADS
Q_WIDTH = N_HEADS * HEAD_DIM
KV_WIDTH = N_KV_HEADS * HEAD_DIM
WINDOW = 128
Q_BLOCK = 128
ROPE_BASE = 10000.0
AXIS_ROT = HEAD_DIM // 2
F_GROUPS = 4
N_EXPERTS = 64
TOP_K = 8
N_EXPERT_GROUPS = 8
TOPK_GROUPS = 4
EXPERT_FF = D_MODEL // 4
SHARED_FF = D_MODEL // 4
ROUTED_SCALE = 2.5
MOE_BLOCK = 128
N_ATTN_LAYERS = (DEPTH + N_MIXERS - 1) // N_MIXERS
N_FNET_LAYERS = DEPTH // N_MIXERS
DEEPNORM_ALPHA = (2.0 * DEPTH) ** 0.25
DEEPNORM_BETA = (8.0 * DEPTH) ** -0.25
LN_EPS = 1e-5
MOD_EPS = 1e-6

kernel_name = "hybrid_swa_fnet_moe_diffusion_trunk"


def _normalize(x, eps):
    xf = x.astype(jnp.float32)
    mu = jnp.mean(xf, axis=-1, keepdims=True)
    var = jnp.mean(jnp.square(xf - mu), axis=-1, keepdims=True)
    return (xf - mu) * lax.rsqrt(var + eps)


def modulate(x, shift, scale):
    return (_normalize(x, MOD_EPS) * (1.0 + scale.astype(jnp.float32)) + shift.astype(jnp.float32)).astype(x.dtype)


def post_norm(x, y, g, b):
    z = DEEPNORM_ALPHA * x.astype(jnp.float32) + y.astype(jnp.float32)
    return (_normalize(z, LN_EPS) * g.astype(jnp.float32) + b.astype(jnp.float32)).astype(x.dtype)


def axial_rope_tables(L):
    rows = L // GRID_W
    r = jnp.repeat(jnp.arange(rows, dtype=jnp.float32), GRID_W)
    col = jnp.tile(jnp.arange(GRID_W, dtype=jnp.float32), rows)
    inv = ROPE_BASE ** (-jnp.arange(0, AXIS_ROT, 2, dtype=jnp.float32) / AXIS_ROT)
    ar = r[:, None] * inv
    ac = col[:, None] * inv
    ang = jnp.concatenate([ar, ar, ac, ac], axis=-1)
    return jnp.cos(ang), jnp.sin(ang)


def _rot_half(t):
    a, b = jnp.split(t, 2, axis=-1)
    return jnp.concatenate([-b, a], axis=-1)


def apply_rope(x, cos, sin):
    xr, xc = jnp.split(x, 2, axis=-1)
    rot = jnp.concatenate([_rot_half(xr), _rot_half(xc)], axis=-1)
    c = cos[None, :, None, :]
    s = sin[None, :, None, :]
    return (x.astype(jnp.float32) * c + rot.astype(jnp.float32) * s).astype(x.dtype)


def sink_attend(qb, keys, values, masks, sinks):
    B, Q = qb.shape[0], qb.shape[1]
    scale = HEAD_DIM ** -0.5
    parts = []
    for kk, m in zip(keys, masks):
        s = jnp.einsum('bqhgd,bkhd->bhgqk', qb, kk).astype(jnp.float32) * scale
        if m is not None:
            s = jnp.where(m, s, -jnp.inf)
        parts.append(s)
    sink = jnp.broadcast_to(sinks.astype(jnp.float32).reshape(1, N_KV_HEADS, GROUP, 1, 1),
                            (B, N_KV_HEADS, GROUP, Q, 1))
    p = jax.nn.softmax(jnp.concatenate(parts + [sink], axis=-1), axis=-1).astype(qb.dtype)
    out = None
    start = 0
    for vv in values:
        n = vv.shape[1]
        o = jnp.einsum('bhgqk,bkhd->bqhgd', p[..., start:start + n], vv)
        out = o if out is None else out + o
        start += n
    return out


def windowed_attention(q, k, v, k_ctx, v_ctx, sinks):
    B, L = q.shape[0], q.shape[1]
    nb = L // Q_BLOCK
    span = Q_BLOCK + 2 * WINDOW
    pad = ((0, 0), (WINDOW, WINDOW), (0, 0), (0, 0))
    k_pad = jnp.pad(k, pad)
    v_pad = jnp.pad(v, pad)
    offs = jnp.arange(span) - WINDOW
    qi = jnp.arange(Q_BLOCK)

    def one_block(b):
        q0 = b * Q_BLOCK
        qb = lax.dynamic_slice_in_dim(q, q0, Q_BLOCK, axis=1)
        kb = lax.dynamic_slice_in_dim(k_pad, q0, span, axis=1)
        vb = lax.dynamic_slice_in_dim(v_pad, q0, span, axis=1)
        kpos = q0 + offs
        mask = (jnp.abs(qi[:, None] - offs[None, :]) <= WINDOW) & ((kpos >= 0) & (kpos < L))[None, :]
        return sink_attend(qb, (kb, k_ctx), (vb, v_ctx), (mask, None), sinks)

    o = lax.map(one_block, jnp.arange(nb))
    return jnp.moveaxis(o, 0, 1).reshape(B, L, N_KV_HEADS, GROUP, HEAD_DIM)


def attention_mixer(u, u_ctx, w_qkv, w_o, sinks, cos, sin, ctx_out):
    B, L, _ = u.shape
    C = u_ctx.shape[1]
    qkv = u @ w_qkv
    q = apply_rope(qkv[..., :Q_WIDTH].reshape(B, L, N_HEADS, HEAD_DIM), cos, sin)
    k = apply_rope(qkv[..., Q_WIDTH:Q_WIDTH + KV_WIDTH].reshape(B, L, N_KV_HEADS, HEAD_DIM), cos, sin)
    v = qkv[..., Q_WIDTH + KV_WIDTH:].reshape(B, L, N_KV_HEADS, HEAD_DIM)
    q = q.reshape(B, L, N_KV_HEADS, GROUP, HEAD_DIM)
    if ctx_out:
        qkv_c = u_ctx @ w_qkv
        q_c = qkv_c[..., :Q_WIDTH].reshape(B, C, N_KV_HEADS, GROUP, HEAD_DIM)
        kv_c = qkv_c[..., Q_WIDTH:]
    else:
        kv_c = u_ctx @ w_qkv[:, Q_WIDTH:]
    k_c = kv_c[..., :KV_WIDTH].reshape(B, C, N_KV_HEADS, HEAD_DIM)
    v_c = kv_c[..., KV_WIDTH:].reshape(B, C, N_KV_HEADS, HEAD_DIM)
    o = windowed_attention(q, k, v, k_c, v_c, sinks)
    y = o.reshape(B, L, Q_WIDTH) @ w_o
    if not ctx_out:
        return y, None
    o_c = sink_attend(q_c, (k_c,), (v_c,), (None,), sinks)
    return y, o_c.reshape(B, C, Q_WIDTH) @ w_o


def fourier_mixer(u, w_f):
    B, L, D = u.shape
    ug = u.reshape(B, L, F_GROUPS, D // F_GROUPS).astype(jnp.float32)
    f = jnp.fft.fft2(ug, axes=(1, 3), norm='ortho').real
    return f.reshape(B, L, D).astype(u.dtype) @ w_f


def moe_ffn(tok, router_w, router_bias, w_gate, w_up, w_down, s_gate, s_up, s_down):
    T, D = tok.shape
    per_group = N_EXPERTS // N_EXPERT_GROUPS
    scores = jax.nn.sigmoid((tok @ router_w).astype(jnp.float32))
    sel = scores + router_bias.astype(jnp.float32)
    group_score = lax.top_k(sel.reshape(T, N_EXPERT_GROUPS, per_group), 2)[0].sum(-1)
    _, top_groups = lax.top_k(group_score, TOPK_GROUPS)
    group_ok = (top_groups[:, :, None] == jnp.arange(N_EXPERT_GROUPS)[None, None, :]).any(axis=1)
    sel = jnp.where(jnp.repeat(group_ok, per_group, axis=1), sel, -jnp.inf)
    _, top_e = lax.top_k(sel, TOP_K)
    wts = jnp.take_along_axis(scores, top_e, axis=-1)
    wts = wts / jnp.sum(wts, axis=-1, keepdims=True) * ROUTED_SCALE

    TK = T * TOP_K
    flat_e = top_e.reshape(-1)
    flat_tok = jnp.arange(TK, dtype=jnp.int32) // TOP_K
    flat_w = wts.reshape(-1)
    order = jnp.argsort(flat_e)
    e_s, tok_s, w_s = flat_e[order], flat_tok[order], flat_w[order]
    counts = jnp.bincount(flat_e, length=N_EXPERTS)
    starts = jnp.cumsum(counts) - counts
    padded = (counts + MOE_BLOCK - 1) // MOE_BLOCK * MOE_BLOCK
    pend = jnp.cumsum(padded)
    pstarts = pend - padded
    dest = pstarts[e_s] + jnp.arange(TK) - starts[e_s]
    n_blocks = -(-TK // MOE_BLOCK) + N_EXPERTS
    cap = n_blocks * MOE_BLOCK
    tok_buf = jnp.full((cap,), T, jnp.int32).at[dest].set(tok_s.astype(jnp.int32))
    w_buf = jnp.zeros((cap,), tok.dtype).at[dest].set(w_s.astype(tok.dtype))
    block_e = jnp.clip(jnp.searchsorted(pend, jnp.arange(n_blocks) * MOE_BLOCK, side='right'), 0, N_EXPERTS - 1)
    x_pad = jnp.concatenate([tok, jnp.zeros((1, D), tok.dtype)], axis=0)

    def body(y, b):
        rows = lax.dynamic_slice_in_dim(tok_buf, b * MOE_BLOCK, MOE_BLOCK)
        wb = lax.dynamic_slice_in_dim(w_buf, b * MOE_BLOCK, MOE_BLOCK)
        e = block_e[b]
        xb = x_pad[rows]
        hb = jax.nn.silu(xb @ w_gate[e]) * (xb @ w_up[e])
        return y.at[rows].add((hb @ w_down[e]) * wb[:, None]), None

    y, _ = lax.scan(body, jnp.zeros_like(x_pad), jnp.arange(n_blocks))
    shared = (jax.nn.silu(tok @ s_gate) * (tok @ s_up)) @ s_down
    return y[:T] + shared


def setup_inputs(seed: int = 0) -> dict:
    key = jax.random.key(seed)
    ks = jax.random.split(key, 20)
    f32 = jnp.float32
    D = D_MODEL

    def nrm(k, shape, s):
        return jax.random.normal(k, shape, f32) * s

    qkv_col_scale = jnp.concatenate([jnp.ones((Q_WIDTH + KV_WIDTH,), f32),
                                     jnp.full((KV_WIDTH,), DEEPNORM_BETA, f32)])
    return {
        "x": nrm(ks[0], (BATCH, SEQ, D), 1.0),
        "c": nrm(ks[1], (BATCH, D), 1.0),
        "ctx": nrm(ks[2], (BATCH, CTX_LEN, D), 1.0),
        "c_ctx": nrm(ks[3], (D,), 1.0),
        "w_ada": nrm(ks[4], (DEPTH, D, 6 * D), D ** -0.5),
        "b_ada": nrm(ks[5], (DEPTH, 6 * D), 0.02),
        "ln_g": 1.0 + nrm(ks[6], (DEPTH, 2, D), 0.02),
        "ln_b": nrm(ks[7], (DEPTH, 2, D), 0.02),
        "attn_w_qkv": nrm(ks[8], (N_ATTN_LAYERS, D, Q_WIDTH + 2 * KV_WIDTH), D ** -0.5) * qkv_col_scale,
        "attn_w_o": nrm(ks[9], (N_ATTN_LAYERS, Q_WIDTH, D), Q_WIDTH ** -0.5 * DEEPNORM_BETA),
        "attn_sinks": nrm(ks[10], (N_ATTN_LAYERS, N_HEADS), 0.5),
        "fnet_w": nrm(ks[11], (N_FNET_LAYERS, D, D), D ** -0.5 * DEEPNORM_BETA),
        "router_w": nrm(ks[12], (DEPTH, D, N_EXPERTS), D ** -0.5),
        "router_bias": nrm(ks[13], (DEPTH, N_EXPERTS), 0.01),
        "exp_w_gate": nrm(ks[14], (DEPTH, N_EXPERTS, D, EXPERT_FF), D ** -0.5),
        "exp_w_up": nrm(ks[15], (DEPTH, N_EXPERTS, D, EXPERT_FF), D ** -0.5),
        "exp_w_down": nrm(ks[16], (DEPTH, N_EXPERTS, EXPERT_FF, D), EXPERT_FF ** -0.5 * DEEPNORM_BETA),
        "sh_w_gate": nrm(ks[17], (DEPTH, D, SHARED_FF), D ** -0.5),
        "sh_w_up": nrm(ks[18], (DEPTH, D, SHARED_FF), D ** -0.5),
        "sh_w_down": nrm(ks[19], (DEPTH, SHARED_FF, D), SHARED_FF ** -0.5 * DEEPNORM_BETA),
    }


def reference(x, c, ctx, c_ctx, w_ada, b_ada, ln_g, ln_b, attn_w_qkv, attn_w_o, attn_sinks, fnet_w,
              router_w, router_bias, exp_w_gate, exp_w_up, exp_w_down, sh_w_gate, sh_w_up, sh_w_down):
    B, L, D = x.shape
    C = ctx.shape[1]
    cos, sin = axial_rope_tables(L)
    h = ctx
    silu_c = jax.nn.silu(c)
    silu_cc = jax.nn.silu(c_ctx)
    for i in range(DEPTH):
        kind = i % N_MIXERS
        j = i // N_MIXERS
        update_ctx = any((l % N_MIXERS) == 0 for l in range(i + 1, DEPTH))
        mod = silu_c @ w_ada[i] + b_ada[i]
        sh1, sc1, g1, sh2, sc2, g2 = [m[:, None, :] for m in jnp.split(mod, 6, axis=-1)]
        csh1, csc1, cg1, csh2, csc2, cg2 = jnp.split(silu_cc @ w_ada[i] + b_ada[i], 6, axis=-1)

        u = modulate(x, sh1, sc1)
        if kind == 0:
            uc = modulate(h, csh1, csc1)
            y, yc = attention_mixer(u, uc, attn_w_qkv[j], attn_w_o[j], attn_sinks[j], cos, sin, update_ctx)
        else:
            y = fourier_mixer(u, fnet_w[j])
            yc = fourier_mixer(modulate(h, csh1, csc1), fnet_w[j]) if update_ctx else None
        x = post_norm(x, g1 * y, ln_g[i, 0], ln_b[i, 0])
        if update_ctx:
            h = post_norm(h, cg1 * yc, ln_g[i, 0], ln_b[i, 0])

        u2 = modulate(x, sh2, sc2)
        if update_ctx:
            u2c = modulate(h, csh2, csc2)
            toks = jnp.concatenate([u2.reshape(-1, D), u2c.reshape(-1, D)], axis=0)
            f = moe_ffn(toks, router_w[i], router_bias[i], exp_w_gate[i], exp_w_up[i], exp_w_down[i],
                        sh_w_gate[i], sh_w_up[i], sh_w_down[i])
            y2 = f[:B * L].reshape(B, L, D)
            h = post_norm(h, cg2 * f[B * L:].reshape(B, C, D), ln_g[i, 1], ln_b[i, 1])
        else:
            y2 = moe_ffn(u2.reshape(-1, D), router_w[i], router_bias[i], exp_w_gate[i], exp_w_up[i],
                         exp_w_down[i], sh_w_gate[i], sh_w_up[i], sh_w_down[i]).reshape(B, L, D)
        x = post_norm(x, g2 * y2, ln_g[i, 1], ln_b[i, 1])
    return x
```

```python
import functools
import math

import jax
import jax.numpy as jnp
from jax import lax
from jax.experimental import pallas as pl
from jax.experimental.pallas import tpu as pltpu

F32 = jnp.float32
BF16 = jnp.bfloat16

N_KV_HEADS = 4
WINDOW = 128
GRID_W = 64
ROPE_BASE = 10000.0
F_GROUPS = 4
TOP_K = 8
N_EXPERT_GROUPS = 8
TOPK_GROUPS = 4
ROUTED_SCALE = 2.5
N_MIXERS = 2
LN_EPS = 1e-5
MOD_EPS = 1e-6

LANES = 128
VMEM_LIMIT_BYTES = 52 * 1024 * 1024
ROW_TILE = 512
QKV_ROW_TILE = 256
Q_TILE = 128
MOE_ROW_TILE = 512
NEG_BIG = -1e30


def _cparams(n_axes):
    return pltpu.CompilerParams(dimension_semantics=("arbitrary",) * n_axes,
                                vmem_limit_bytes=VMEM_LIMIT_BYTES)


def _dot(a, b):
    return jnp.dot(a, b, preferred_element_type=F32)


def _dot_nt(a, b):
    return lax.dot_general(a, b, (((1,), (1,)), ((), ())), preferred_element_type=F32)


def _split_bf16(a):
    hi = a.astype(BF16)
    lo = (a - hi.astype(F32)).astype(BF16)
    return hi, lo


def _normalize(x, eps):
    mu = jnp.mean(x, axis=-1, keepdims=True)
    xc = x - mu
    var = jnp.mean(xc * xc, axis=-1, keepdims=True)
    return xc * lax.rsqrt(var + eps)


def _modulate(x, shift, scale):
    return _normalize(x, MOD_EPS) * (1.0 + scale) + shift


def _silu(x):
    return x * jax.nn.sigmoid(x)


def _ada_kernel(c_ref, w_ref, b_ref, o_ref):
    s = _silu(c_ref[...])
    sh, sl = _split_bf16(s)
    wh, wl = _split_bf16(w_ref[...])
    o_ref[...] = _dot(sh, wh) + _dot(sl, wh) + _dot(sh, wl) + b_ref[...]


def _ada_all(cvec, w_ada, b_ada):
    depth, d, n = w_ada.shape
    mp = cvec.shape[0]
    tn = 1536
    assert n % tn == 0
    return pl.pallas_call(
        _ada_kernel,
        grid=(depth, n // tn),
        in_specs=[
            pl.BlockSpec((mp, d), lambda i, j: (0, 0)),
            pl.BlockSpec((None, d, tn), lambda i, j: (i, 0, j)),
            pl.BlockSpec((None, 1, tn), lambda i, j: (i, 0, j)),
        ],
        out_specs=pl.BlockSpec((None, mp, tn), lambda i, j: (i, 0, j)),
        out_shape=jax.ShapeDtypeStruct((depth, mp, n), F32),
        compiler_params=_cparams(2),
        name="ada_mod",
    )(cvec, w_ada, b_ada.reshape(depth, 1, n))


def _mod_spec(j, d, ctx_row, batch_axis):
    if ctx_row is None:
        return pl.BlockSpec((None, None, 1, d), lambda *g: (g[batch_axis], j, 0, 0))
    return pl.BlockSpec((None, None, 1, d), lambda *g: (ctx_row, j, 0, 0))


def _row_spec(d):
    return pl.BlockSpec((1, d), lambda *g: (0, 0))


def _qkv_kernel(x_ref, sh_ref, sc_ref, w_ref, cos_ref, sin_ref, q_ref, k_ref, v_ref, *, qw, kvw):
    u = _modulate(x_ref[...], sh_ref[...], sc_ref[...]).astype(BF16)
    r = _dot(u, w_ref[...])
    qk = r[:, :qw + kvw] * cos_ref[...] + r[:, qw + 2 * kvw:] * sin_ref[...]
    q_ref[...] = qk[:, :qw].astype(BF16)
    k_ref[...] = qk[:, qw:].astype(BF16)
    v_ref[...] = r[:, qw + kvw:qw + 2 * kvw].astype(BF16)


def _qkv_proj(x, mod_i, ctx_row, w_all, cos_t, sin_t, qw, kvw):
    b, l, d = x.shape
    tm = min(QKV_ROW_TILE, l)
    assert l % tm == 0
    n_all = w_all.shape[1]
    return pl.pallas_call(
        functools.partial(_qkv_kernel, qw=qw, kvw=kvw),
        grid=(l // tm, b),
        in_specs=[
            pl.BlockSpec((None, tm, d), lambda t, bb: (bb, t, 0)),
            _mod_spec(0, d, ctx_row, 1),
            _mod_spec(1, d, ctx_row, 1),
            pl.BlockSpec((d, n_all), lambda t, bb: (0, 0)),
            pl.BlockSpec((tm, qw + kvw), lambda t, bb: (t, 0)),
            pl.BlockSpec((tm, qw + kvw), lambda t, bb: (t, 0)),
        ],
        out_specs=[
            pl.BlockSpec((None, tm, qw), lambda t, bb: (bb, t, 0)),
            pl.BlockSpec((None, tm, kvw), lambda t, bb: (bb, t, 0)),
            pl.BlockSpec((None, tm, kvw), lambda t, bb: (bb, t, 0)),
        ],
        out_shape=[
            jax.ShapeDtypeStruct((b, l, qw), BF16),
            jax.ShapeDtypeStruct((b, l, kvw), BF16),
            jax.ShapeDtypeStruct((b, l, kvw), BF16),
        ],
        compiler_params=_cparams(2),
        name="qkv_rope",
    )(x, mod_i, mod_i, w_all, cos_t, sin_t)


def _attn_kernel(sink_ref, q_ref, *refs, tq, seq, n_kv, group, hd, has_window):
    if has_window:
        k_ref, v_ref, kc_ref, vc_ref, o_ref = refs
    else:
        kc_ref, vc_ref, o_ref = refs
    q = q_ref[...]
    kc = kc_ref[...]
    vc = vc_ref[...]
    if has_window:
        span = tq + 2 * WINDOW
        q0 = pl.program_id(1) * tq
        start = pl.multiple_of(jnp.clip(q0 - WINDOW, 0, seq - span), LANES)
        kw = k_ref[pl.ds(start, span), :]
        vw = v_ref[pl.ds(start, span), :]
        qpos = q0 + lax.broadcasted_iota(jnp.int32, (tq, span), 0)
        kpos = start + lax.broadcasted_iota(jnp.int32, (tq, span), 1)
        mask1 = jnp.abs(qpos - kpos) <= WINDOW
        mask = jnp.concatenate([mask1] * group, axis=0)
    for h in range(n_kv):
        heads = [h * group + g for g in range(group)]
        qh = jnp.concatenate([q[:, j * hd:(j + 1) * hd] for j in heads], axis=0)
        sink = jnp.concatenate([jnp.full((tq, 1), sink_ref[j], F32) for j in heads], axis=0)
        sc = _dot_nt(qh, kc[:, h * hd:(h + 1) * hd])
        m = jnp.maximum(jnp.max(sc, axis=-1, keepdims=True), sink)
        if has_window:
            sw = jnp.where(mask, _dot_nt(qh, kw[:, h * hd:(h + 1) * hd]), NEG_BIG)
            m = jnp.maximum(m, jnp.max(sw, axis=-1, keepdims=True))
        pc = jnp.exp(sc - m)
        den = jnp.sum(pc, axis=-1, keepdims=True) + jnp.exp(sink - m)
        o = _dot(pc.astype(BF16), vc[:, h * hd:(h + 1) * hd])
        if has_window:
            pw = jnp.exp(sw - m)
            den = den + jnp.sum(pw, axis=-1, keepdims=True)
            o = o + _dot(pw.astype(BF16), vw[:, h * hd:(h + 1) * hd])
        o = o / den
        for g, j in enumerate(heads):
            o_ref[:, j * hd:(j + 1) * hd] = o[g * tq:(g + 1) * tq, :].astype(o_ref.dtype)


def _attention(q, k, v, kc, vc, sinks, has_window):
    b, l, qw = q.shape
    c, kvw = kc.shape[1], kc.shape[2]
    hd = kvw // N_KV_HEADS
    group = qw // kvw
    tq = Q_TILE if has_window else l
    assert l % tq == 0
    if has_window:
        assert l >= tq + 2 * WINDOW
    kern = functools.partial(_attn_kernel, tq=tq, seq=l, n_kv=N_KV_HEADS, group=group, hd=hd,
                             has_window=has_window)
    in_specs = [pl.BlockSpec(memory_space=pltpu.SMEM),
                pl.BlockSpec((None, tq, qw), lambda bb, t: (bb, t, 0))]
    args = [sinks, q]
    if has_window:
        in_specs += [pl.BlockSpec((None, l, kvw), lambda bb, t: (bb, 0, 0)),
                     pl.BlockSpec((None, l, kvw), lambda bb, t: (bb, 0, 0))]
        args += [k, v]
    in_specs += [pl.BlockSpec((None, c, kvw), lambda bb, t: (bb, 0, 0)),
                 pl.BlockSpec((None, c, kvw), lambda bb, t: (bb, 0, 0))]
    args += [kc, vc]
    return pl.pallas_call(
        kern,
        grid=(b, l // tq),
        in_specs=in_specs,
        out_specs=pl.BlockSpec((None, tq, qw), lambda bb, t: (bb, t, 0)),
        out_shape=jax.ShapeDtypeStruct((b, l, qw), BF16),
        compiler_params=_cparams(2),
        name="win_attn" if has_window else "ctx_attn",
    )(*args)


def _proj_post_kernel(a_ref, w_ref, x_ref, gate_ref, g_ref, b_ref, o_ref, *, alpha):
    y = _dot(a_ref[...], w_ref[...])
    z = alpha * x_ref[...] + gate_ref[...] * y
    o_ref[...] = _normalize(z, LN_EPS) * g_ref[...] + b_ref[...]


def _proj_post(a, w, x, mod_i, gate_j, ctx_row, ln_g, ln_b, alpha):
    b, l, d = x.shape
    ka = a.shape[2]
    tm = min(ROW_TILE, l)
    assert l % tm == 0
    return pl.pallas_call(
        functools.partial(_proj_post_kernel, alpha=alpha),
        grid=(b, l // tm),
        in_specs=[
            pl.BlockSpec((None, tm, ka), lambda bb, t: (bb, t, 0)),
            pl.BlockSpec((ka, d), lambda bb, t: (0, 0)),
            pl.BlockSpec((None, tm, d), lambda bb, t: (bb, t, 0)),
            _mod_spec(gate_j, d, ctx_row, 0),
            _row_spec(d),
            _row_spec(d),
        ],
        out_specs=pl.BlockSpec((None, tm, d), lambda bb, t: (bb, t, 0)),
        out_shape=jax.ShapeDtypeStruct((b, l, d), F32),
        compiler_params=_cparams(2),
        name="proj_post",
    )(a, w, x, mod_i, ln_g, ln_b)


def _fnet_a_kernel(x_ref, sh_ref, sc_ref, cs_ref, o_ref, *, d, gc):
    u = _modulate(x_ref[...], sh_ref[...], sc_ref[...]).astype(BF16)
    cs = cs_ref[...]
    for g in range(d // gc):
        r = _dot(u[:, g * gc:(g + 1) * gc], cs)
        o_ref[:, g * gc:(g + 1) * gc] = r[:, :gc].astype(BF16)
        o_ref[:, d + g * gc:d + (g + 1) * gc] = r[:, gc:].astype(BF16)


def _fnet_a(x, mod_i, ctx_row, cs):
    b, l, d = x.shape
    gc = d // F_GROUPS
    tm = min(ROW_TILE, l)
    assert l % tm == 0
    return pl.pallas_call(
        functools.partial(_fnet_a_kernel, d=d, gc=gc),
        grid=(b, l // tm),
        in_specs=[
            pl.BlockSpec((None, tm, d), lambda bb, t: (bb, t, 0)),
            _mod_spec(0, d, ctx_row, 0),
            _mod_spec(1, d, ctx_row, 0),
            pl.BlockSpec((gc, 2 * gc), lambda bb, t: (0, 0)),
        ],
        out_specs=pl.BlockSpec((None, tm, 2 * d), lambda bb, t: (bb, t, 0)),
        out_shape=jax.ShapeDtypeStruct((b, l, 2 * d), BF16),
        compiler_params=_cparams(2),
        name="fnet_chan_dft",
    )(x, mod_i, mod_i, cs)


def _fnet_b_kernel(cl_ref, sl_ref, ab_ref, wf_ref, x_ref, gate_ref, g_ref, b_ref, o_ref, *, d, alpha):
    f = _dot(cl_ref[...], ab_ref[:, :d]) + _dot(sl_ref[...], ab_ref[:, d:])
    y = _dot(f.astype(BF16), wf_ref[...])
    z = alpha * x_ref[...] + gate_ref[...] * y
    o_ref[...] = _normalize(z, LN_EPS) * g_ref[...] + b_ref[...]


def _fnet_b(cl, sl, ab, wf, x, mod_i, gate_j, ctx_row, ln_g, ln_b, alpha):
    b, l, d = x.shape
    tm = min(ROW_TILE, l)
    assert l % tm == 0
    return pl.pallas_call(
        functools.partial(_fnet_b_kernel, d=d, alpha=alpha),
        grid=(b, l // tm),
        in_specs=[
            pl.BlockSpec((tm, l), lambda bb, t: (t, 0)),
            pl.BlockSpec((tm, l), lambda bb, t: (t, 0)),
            pl.BlockSpec((None, l, 2 * d), lambda bb, t: (bb, 0, 0)),
            pl.BlockSpec((d, d), lambda bb, t: (0, 0)),
            pl.BlockSpec((None, tm, d), lambda bb, t: (bb, t, 0)),
            _mod_spec(gate_j, d, ctx_row, 0),
            _row_spec(d),
            _row_spec(d),
        ],
        out_specs=pl.BlockSpec((None, tm, d), lambda bb, t: (bb, t, 0)),
        out_shape=jax.ShapeDtypeStruct((b, l, d), F32),
        compiler_params=_cparams(2),
        name="fnet_seq_dft",
    )(cl, sl, ab, wf, x, mod_i, ln_g, ln_b)


def _dft_tables(n, scale):
    j = jnp.arange(n, dtype=jnp.int32)
    ang = ((j[:, None] * j[None, :]) % n).astype(F32) * (2.0 * math.pi / n)
    return jnp.cos(ang) * scale, jnp.sin(ang) * scale


def _router_kernel(x_ref, sh_ref, sc_ref, rwt_ref, u_ref, lg_ref):
    u = _modulate(x_ref[...], sh_ref[...], sc_ref[...])
    uh, ul = _split_bf16(u)
    u_ref[...] = uh
    wh, wl = _split_bf16(rwt_ref[...])
    lg_ref[...] = _dot_nt(wh, uh) + _dot_nt(wl, uh) + _dot_nt(wh, ul)


def _router(x, mod_i, ctx_row, rwt):
    b, l, d = x.shape
    e = rwt.shape[0]
    tm = min(ROW_TILE, l)
    assert l % tm == 0
    nt = l // tm
    return pl.pallas_call(
        _router_kernel,
        grid=(b, nt),
        in_specs=[
            pl.BlockSpec((None, tm, d), lambda bb, t: (bb, t, 0)),
            _mod_spec(3, d, ctx_row, 0),
            _mod_spec(4, d, ctx_row, 0),
            pl.BlockSpec((e, d), lambda bb, t: (0, 0)),
        ],
        out_specs=[
            pl.BlockSpec((None, tm, d), lambda bb, t: (bb, t, 0)),
            pl.BlockSpec((e, tm), lambda bb, t: (0, bb * nt + t)),
        ],
        out_shape=[
            jax.ShapeDtypeStruct((b, l, d), BF16),
            jax.ShapeDtypeStruct((e, b * l), F32),
        ],
        compiler_params=_cparams(2),
        name="moe_router",
    )(x, mod_i, mod_i, rwt)


def _route(logits_t, bias, tmb):
    e, t = logits_t.shape
    per_group = e // N_EXPERT_GROUPS
    scores = jax.nn.sigmoid(logits_t.T)
    sel = scores + bias.astype(F32)
    group_score = lax.top_k(sel.reshape(t, N_EXPERT_GROUPS, per_group), 2)[0].sum(-1)
    _, top_groups = lax.top_k(group_score, TOPK_GROUPS)
    group_ok = (top_groups[:, :, None] == jnp.arange(N_EXPERT_GROUPS)[None, None, :]).any(axis=1)
    sel = jnp.where(jnp.repeat(group_ok, per_group, axis=1), sel, -jnp.inf)
    _, top_e = lax.top_k(sel, TOP_K)
    wts = jnp.take_along_axis(scores, top_e, axis=-1)
    wts = wts / jnp.sum(wts, axis=-1, keepdims=True) * ROUTED_SCALE

    tk = t * TOP_K
    flat_e = top_e.reshape(-1).astype(jnp.int32)
    order = jnp.argsort(flat_e)
    e_s = flat_e[order]
    counts = jnp.bincount(flat_e, length=e).astype(jnp.int32)
    starts = jnp.cumsum(counts) - counts
    padded = (counts + tmb - 1) // tmb * tmb
    pend = jnp.cumsum(padded)
    pstarts = pend - padded
    dest = (pstarts[e_s] + jnp.arange(tk, dtype=jnp.int32) - starts[e_s]).astype(jnp.int32)
    n_blocks = -(-tk // tmb) + e
    tok_buf = jnp.zeros((n_blocks * tmb,), jnp.int32).at[dest].set((order // TOP_K).astype(jnp.int32))
    pos = jnp.zeros((tk,), jnp.int32).at[order].set(dest).reshape(t, TOP_K)
    n_used = (pend[-1] // tmb).astype(jnp.int32)
    blk = jnp.arange(n_blocks, dtype=jnp.int32)
    block_e = jnp.clip(jnp.searchsorted(pend, blk * tmb, side='right'), 0, e - 1).astype(jnp.int32)
    block_e = jnp.where(blk < n_used, block_e, block_e[jnp.maximum(n_used - 1, 0)])
    return wts, pos, tok_buf, block_e, n_used.reshape(1)


def _gmm_kernel(be_ref, nu_ref, x_ref, wg_ref, wu_ref, wd_ref, o_ref):
    @pl.when(pl.program_id(0) < nu_ref[0])
    def _():
        x = x_ref[...]
        h = _silu(_dot(x, wg_ref[...])) * _dot(x, wu_ref[...])
        o_ref[...] = _dot(h.astype(BF16), wd_ref[...])


def _grouped_experts(x_sorted, block_e, n_used, wg, wu, wd, tmb):
    cap, d = x_sorted.shape
    ff = wg.shape[2]
    n_blocks = cap // tmb

    def row_map(i, be, nu):
        return (jnp.minimum(i, nu[0] - 1), 0)

    def w_map(i, be, nu):
        return (be[i], 0, 0)

    return pl.pallas_call(
        _gmm_kernel,
        grid_spec=pltpu.PrefetchScalarGridSpec(
            num_scalar_prefetch=2,
            grid=(n_blocks,),
            in_specs=[
                pl.BlockSpec((tmb, d), row_map),
                pl.BlockSpec((None, d, ff), w_map),
                pl.BlockSpec((None, d, ff), w_map),
                pl.BlockSpec((None, ff, d), w_map),
            ],
            out_specs=pl.BlockSpec((tmb, d), row_map),
        ),
        out_shape=jax.ShapeDtypeStruct((cap, d), F32),
        compiler_params=_cparams(1),
        name="moe_experts",
    )(block_e, n_used, x_sorted, wg, wu, wd)


def _moe_out_kernel(u_ref, r_ref, sg_ref, su_ref, sd_ref, x_ref, gate_ref, g_ref, b_ref, o_ref, *, alpha):
    u = u_ref[...]
    h = _silu(_dot(u, sg_ref[...])) * _dot(u, su_ref[...])
    y = r_ref[...] + _dot(h.astype(BF16), sd_ref[...])
    z = alpha * x_ref[...] + gate_ref[...] * y
    o_ref[...] = _normalize(z, LN_EPS) * g_ref[...] + b_ref[...]


def _moe_out(u2, routed, sg, su, sd, x, mod_i, ctx_row, ln_g, ln_b, alpha):
    b, l, d = x.shape
    ff = sg.shape[1]
    tm = min(ROW_TILE, l)
    assert l % tm == 0
    return pl.pallas_call(
        functools.partial(_moe_out_kernel, alpha=alpha),
        grid=(b, l // tm),
        in_specs=[
            pl.BlockSpec((None, tm, d), lambda bb, t: (bb, t, 0)),
            pl.BlockSpec((None, tm, d), lambda bb, t: (bb, t, 0)),
            pl.BlockSpec((d, ff), lambda bb, t: (0, 0)),
            pl.BlockSpec((d, ff), lambda bb, t: (0, 0)),
            pl.BlockSpec((ff, d), lambda bb, t: (0, 0)),
            pl.BlockSpec((None, tm, d), lambda bb, t: (bb, t, 0)),
            _mod_spec(5, d, ctx_row, 0),
            _row_spec(d),
            _row_spec(d),
        ],
        out_specs=pl.BlockSpec((None, tm, d), lambda bb, t: (bb, t, 0)),
        out_shape=jax.ShapeDtypeStruct((b, l, d), F32),
        compiler_params=_cparams(2),
        name="moe_shared_post",
    )(u2, routed, sg, su, sd, x, mod_i, ln_g, ln_b)


def _moe_layer(x, mod_i, ctx_row, rwt, rbias, wg, wu, wd, sg, su, sd, ln_g, ln_b, alpha):
    b, l, d = x.shape
    u2, logits_t = _router(x, mod_i, ctx_row, rwt)
    wts, pos, tok_buf, block_e, n_used = _route(logits_t, rbias, MOE_ROW_TILE)
    u2f = u2.reshape(b * l, d)
    x_sorted = jnp.take(u2f, tok_buf, axis=0)
    y_sorted = _grouped_experts(x_sorted, block_e, n_used, wg, wu, wd, MOE_ROW_TILE)
    routed = jnp.zeros((b * l, d), F32)
    for k in range(TOP_K):
        routed = routed + wts[:, k, None] * jnp.take(y_sorted, pos[:, k], axis=0)
    return _moe_out(u2, routed.reshape(b, l, d), sg, su, sd, x, mod_i, ctx_row, ln_g, ln_b, alpha)


def _rope_tables(l, hd, n_q_heads, n_k_heads, rope):
    axis_rot = hd // 2
    qscale = hd ** -0.5
    if rope:
        t = jnp.arange(l, dtype=jnp.int32)
        r = (t // GRID_W).astype(F32)
        col = (t % GRID_W).astype(F32)
        inv = ROPE_BASE ** (-jnp.arange(0, axis_rot, 2, dtype=F32) / axis_rot)
        ar = r[:, None] * inv
        ac = col[:, None] * inv
        ang = jnp.concatenate([ar, ar, ac, ac], axis=-1)
        cos, sin = jnp.cos(ang), jnp.sin(ang)
    else:
        cos, sin = jnp.ones((l, hd), F32), jnp.zeros((l, hd), F32)
    cos_t = jnp.concatenate([jnp.tile(cos, (1, n_q_heads)) * qscale, jnp.tile(cos, (1, n_k_heads))], axis=1)
    sin_t = jnp.concatenate([jnp.tile(sin, (1, n_q_heads)) * qscale, jnp.tile(sin, (1, n_k_heads))], axis=1)
    return cos_t, sin_t


def _rot_columns(w, hd):
    d, n = w.shape
    q = hd // 4
    w4 = w.reshape(d, n // (2 * q), 2, q)
    return jnp.stack([-w4[:, :, 1, :], w4[:, :, 0, :]], axis=2).reshape(d, n)


def kernel(x, c, ctx, c_ctx, w_ada, b_ada, ln_g, ln_b, attn_w_qkv, attn_w_o, attn_sinks, fnet_w,
           router_w, router_bias, exp_w_gate, exp_w_up, exp_w_down, sh_w_gate, sh_w_up, sh_w_down):
    b, l, d = x.shape
    cl = ctx.shape[1]
    depth = w_ada.shape[0]
    n_heads = attn_sinks.shape[1]
    hd = d // n_heads
    qw = n_heads * hd
    kvw = N_KV_HEADS * hd
    alpha = (2.0 * depth) ** 0.25
    gc = d // F_GROUPS

    mp = -(-(b + 1) // 16) * 16
    cvec = jnp.concatenate([c, c_ctx[None, :], jnp.zeros((mp - b - 1, d), F32)], axis=0)
    mod = _ada_all(cvec, w_ada, b_ada).reshape(depth, mp, 6, 1, d)

    h = ctx
    for i in range(depth):
        kind = i % N_MIXERS
        j = i // N_MIXERS
        update_ctx = any((m % N_MIXERS) == 0 for m in range(i + 1, depth))
        mod_i = mod[i]
        g1 = ln_g[i, 0].reshape(1, d)
        b1 = ln_b[i, 0].reshape(1, d)
        g2 = ln_g[i, 1].reshape(1, d)
        b2 = ln_b[i, 1].reshape(1, d)

        if kind == 0:
            w = attn_w_qkv[j]
            w_qk = w[:, :qw + kvw]
            w_all = jnp.concatenate([w, _rot_columns(w_qk, hd)], axis=1).astype(BF16)
            w_o = attn_w_o[j].astype(BF16)
            cos_l, sin_l = _rope_tables(l, hd, n_heads, N_KV_HEADS, True)
            cos_c, sin_c = _rope_tables(cl, hd, n_heads, N_KV_HEADS, False)
            q, k, v = _qkv_proj(x, mod_i, None, w_all, cos_l, sin_l, qw, kvw)
            q_c, k_c, v_c = _qkv_proj(h, mod_i, b, w_all, cos_c, sin_c, qw, kvw)
            o = _attention(q, k, v, k_c, v_c, attn_sinks[j], True)
            x = _proj_post(o, w_o, x, mod_i, 2, None, g1, b1, alpha)
            if update_ctx:
                o_c = _attention(q_c, None, None, k_c, v_c, attn_sinks[j], False)
                h = _proj_post(o_c, w_o, h, mod_i, 2, b, g1, b1, alpha)
        else:
            wf = fnet_w[j].astype(BF16)
            cc, sc = _dft_tables(gc, gc ** -0.5)
            cs = jnp.concatenate([cc, sc], axis=1).astype(BF16)
            streams = [(x, None)] + ([(h, b)] if update_ctx else [])
            outs = []
            for s, ctx_row in streams:
                ls = s.shape[1]
                c_l, s_l = _dft_tables(ls, ls ** -0.5)
                ab = _fnet_a(s, mod_i, ctx_row, cs)
                outs.append(_fnet_b(c_l.astype(BF16), (-s_l).astype(BF16), ab, wf, s, mod_i, 2, ctx_row,
                                    g1, b1, alpha))
            x = outs[0]
            if update_ctx:
                h = outs[1]

        rwt = router_w[i].T
        wg = exp_w_gate[i].astype(BF16)
        wu = exp_w_up[i].astype(BF16)
        wd = exp_w_down[i].astype(BF16)
        sg = sh_w_gate[i].astype(BF16)
        su = sh_w_up[i].astype(BF16)
        sd = sh_w_down[i].astype(BF16)
        x = _moe_layer(x, mod_i, None, rwt, router_bias[i], wg, wu, wd, sg, su, sd, g2, b2, alpha)
        if update_ctx:
            h = _moe_layer(h, mod_i, b, rwt, router_bias[i], wg, wu, wd, sg, su, sd, g2, b2, alpha)
    return x
```

```python
import functools
import math

import jax
import jax.numpy as jnp
from jax import lax
from jax.experimental import pallas as pl
from jax.experimental.pallas import tpu as pltpu

F32 = jnp.float32
BF16 = jnp.bfloat16

N_KV_HEADS = 4
WINDOW = 128
GRID_W = 64
ROPE_BASE = 10000.0
F_GROUPS = 4
TOP_K = 8
N_EXPERT_GROUPS = 8
TOPK_GROUPS = 4
ROUTED_SCALE = 2.5
N_MIXERS = 2
LN_EPS = 1e-5
MOD_EPS = 1e-6

LANES = 128
VMEM_LIMIT_BYTES = 52 * 1024 * 1024
ROW_TILE = 512
QKV_ROW_TILE = 256
Q_TILE = 128
MOE_ROW_TILE = 512
NEG_BIG = -1e30


def _cparams(n_axes):
    return pltpu.CompilerParams(dimension_semantics=("arbitrary",) * n_axes,
                                vmem_limit_bytes=VMEM_LIMIT_BYTES)


def _dot(a, b):
    return jnp.dot(a, b, preferred_element_type=F32)


def _dot_nt(a, b):
    return lax.dot_general(a, b, (((1,), (1,)), ((), ())), preferred_element_type=F32)


def _split_bf16(a):
    hi = a.astype(BF16)
    lo = (a - hi.astype(F32)).astype(BF16)
    return hi, lo


def _normalize(x, eps):
    mu = jnp.mean(x, axis=-1, keepdims=True)
    xc = x - mu
    var = jnp.mean(xc * xc, axis=-1, keepdims=True)
    return xc * lax.rsqrt(var + eps)


def _modulate(x, shift, scale):
    return _normalize(x, MOD_EPS) * (1.0 + scale) + shift


def _silu(x):
    return x * jax.nn.sigmoid(x)


def _ada_kernel(c_ref, w_ref, b_ref, o_ref):
    s = _silu(c_ref[...])
    sh, sl = _split_bf16(s)
    wh, wl = _split_bf16(w_ref[...])
    o_ref[...] = _dot(sh, wh) + _dot(sl, wh) + _dot(sh, wl) + b_ref[...]


def _ada_all(cvec, w_ada, b_ada):
    depth, d, n = w_ada.shape
    mp = cvec.shape[0]
    tn = 1536
    assert n % tn == 0
    return pl.pallas_call(
        _ada_kernel,
        grid=(depth, n // tn),
        in_specs=[
            pl.BlockSpec((mp, d), lambda i, j: (0, 0)),
            pl.BlockSpec((None, d, tn), lambda i, j: (i, 0, j)),
            pl.BlockSpec((None, 1, tn), lambda i, j: (i, 0, j)),
        ],
        out_specs=pl.BlockSpec((None, mp, tn), lambda i, j: (i, 0, j)),
        out_shape=jax.ShapeDtypeStruct((depth, mp, n), F32),
        compiler_params=_cparams(2),
        name="ada_mod",
    )(cvec, w_ada, b_ada.reshape(depth, 1, n))


def _mod_spec(j, d, ctx_row, batch_axis):
    if ctx_row is None:
        return pl.BlockSpec((None, None, 1, d), lambda *g: (g[batch_axis], j, 0, 0))
    return pl.BlockSpec((None, None, 1, d), lambda *g: (ctx_row, j, 0, 0))


def _row_spec(d):
    return pl.BlockSpec((1, d), lambda *g: (0, 0))


def _qkv_kernel(x_ref, sh_ref, sc_ref, w_ref, cos_ref, sin_ref, q_ref, k_ref, v_ref, *, qw, kvw):
    u = _modulate(x_ref[...], sh_ref[...], sc_ref[...]).astype(BF16)
    r = _dot(u, w_ref[...])
    qk = r[:, :qw + kvw] * cos_ref[...] + r[:, qw + 2 * kvw:] * sin_ref[...]
    q_ref[...] = qk[:, :qw].astype(BF16)
    k_ref[...] = qk[:, qw:].astype(BF16)
    v_ref[...] = r[:, qw + kvw:qw + 2 * kvw].astype(BF16)


def _qkv_proj(x, mod_i, ctx_row, w_all, cos_t, sin_t, qw, kvw):
    b, l, d = x.shape
    tm = min(QKV_ROW_TILE, l)
    assert l % tm == 0
    n_all = w_all.shape[1]
    return pl.pallas_call(
        functools.partial(_qkv_kernel, qw=qw, kvw=kvw),
        grid=(l // tm, b),
        in_specs=[
            pl.BlockSpec((None, tm, d), lambda t, bb: (bb, t, 0)),
            _mod_spec(0, d, ctx_row, 1),
            _mod_spec(1, d, ctx_row, 1),
            pl.BlockSpec((d, n_all), lambda t, bb: (0, 0)),
            pl.BlockSpec((tm, qw + kvw), lambda t, bb: (t, 0)),
            pl.BlockSpec((tm, qw + kvw), lambda t, bb: (t, 0)),
        ],
        out_specs=[
            pl.BlockSpec((None, tm, qw), lambda t, bb: (bb, t, 0)),
            pl.BlockSpec((None, tm, kvw), lambda t, bb: (bb, t, 0)),
            pl.BlockSpec((None, tm, kvw), lambda t, bb: (bb, t, 0)),
        ],
        out_shape=[
            jax.ShapeDtypeStruct((b, l, qw), BF16),
            jax.ShapeDtypeStruct((b, l, kvw), BF16),
            jax.ShapeDtypeStruct((b, l, kvw), BF16),
        ],
        compiler_params=_cparams(2),
        name="qkv_rope",
    )(x, mod_i, mod_i, w_all, cos_t, sin_t)


def _attn_kernel(sink_ref, q_ref, *refs, tq, seq, n_kv, group, hd, has_window):
    if has_window:
        k_ref, v_ref, kc_ref, vc_ref, o_ref = refs
    else:
        kc_ref, vc_ref, o_ref = refs
    q = q_ref[...]
    kc = kc_ref[...]
    vc = vc_ref[...]
    if has_window:
        span = tq + 2 * WINDOW
        q0 = pl.program_id(1) * tq
        start = pl.multiple_of(jnp.clip(q0 - WINDOW, 0, seq - span), LANES)
        kw = k_ref[pl.ds(start, span), :]
        vw = v_ref[pl.ds(start, span), :]
        qpos = q0 + lax.broadcasted_iota(jnp.int32, (tq, span), 0)
        kpos = start + lax.broadcasted_iota(jnp.int32, (tq, span), 1)
        mask1 = jnp.abs(qpos - kpos) <= WINDOW
        mask = jnp.concatenate([mask1] * group, axis=0)
    for h in range(n_kv):
        heads = [h * group + g for g in range(group)]
        qh = jnp.concatenate([q[:, j * hd:(j + 1) * hd] for j in heads], axis=0)
        sink = jnp.concatenate([jnp.full((tq, 1), sink_ref[j], F32) for j in heads], axis=0)
        sc = _dot_nt(qh, kc[:, h * hd:(h + 1) * hd])
        m = jnp.maximum(jnp.max(sc, axis=-1, keepdims=True), sink)
        if has_window:
            sw = jnp.where(mask, _dot_nt(qh, kw[:, h * hd:(h + 1) * hd]), NEG_BIG)
            m = jnp.maximum(m, jnp.max(sw, axis=-1, keepdims=True))
        pc = jnp.exp(sc - m)
        den = jnp.sum(pc, axis=-1, keepdims=True) + jnp.exp(sink - m)
        o = _dot(pc.astype(BF16), vc[:, h * hd:(h + 1) * hd])
        if has_window:
            pw = jnp.exp(sw - m)
            den = den + jnp.sum(pw, axis=-1, keepdims=True)
            o = o + _dot(pw.astype(BF16), vw[:, h * hd:(h + 1) * hd])
        o = o / den
        for g, j in enumerate(heads):
            o_ref[:, j * hd:(j + 1) * hd] = o[g * tq:(g + 1) * tq, :].astype(o_ref.dtype)


def _attention(q, k, v, kc, vc, sinks, has_window):
    b, l, qw = q.shape
    c, kvw = kc.shape[1], kc.shape[2]
    hd = kvw // N_KV_HEADS
    group = qw // kvw
    tq = Q_TILE if has_window else l
    assert l % tq == 0
    if has_window:
        assert l >= tq + 2 * WINDOW
    kern = functools.partial(_attn_kernel, tq=tq, seq=l, n_kv=N_KV_HEADS, group=group, hd=hd,
                             has_window=has_window)
    in_specs = [pl.BlockSpec(memory_space=pltpu.SMEM),
                pl.BlockSpec((None, tq, qw), lambda bb, t: (bb, t, 0))]
    args = [sinks, q]
    if has_window:
        in_specs += [pl.BlockSpec((None, l, kvw), lambda bb, t: (bb, 0, 0)),
                     pl.BlockSpec((None, l, kvw), lambda bb, t: (bb, 0, 0))]
        args += [k, v]
    in_specs += [pl.BlockSpec((None, c, kvw), lambda bb, t: (bb, 0, 0)),
                 pl.BlockSpec((None, c, kvw), lambda bb, t: (bb, 0, 0))]
    args += [kc, vc]
    return pl.pallas_call(
        kern,
        grid=(b, l // tq),
        in_specs=in_specs,
        out_specs=pl.BlockSpec((None, tq, qw), lambda bb, t: (bb, t, 0)),
        out_shape=jax.ShapeDtypeStruct((b, l, qw), BF16),
        compiler_params=_cparams(2),
        name="win_attn" if has_window else "ctx_attn",
    )(*args)


def _proj_post_kernel(a_ref, w_ref, x_ref, gate_ref, g_ref, b_ref, o_ref, *, alpha):
    y = _dot(a_ref[...], w_ref[...])
    z = alpha * x_ref[...] + gate_ref[...] * y
    o_ref[...] = _normalize(z, LN_EPS) * g_ref[...] + b_ref[...]


def _proj_post(a, w, x, mod_i, gate_j, ctx_row, ln_g, ln_b, alpha):
    b, l, d = x.shape
    ka = a.shape[2]
    tm = min(ROW_TILE, l)
    assert l % tm == 0
    return pl.pallas_call(
        functools.partial(_proj_post_kernel, alpha=alpha),
        grid=(b, l // tm),
        in_specs=[
            pl.BlockSpec((None, tm, ka), lambda bb, t: (bb, t, 0)),
            pl.BlockSpec((ka, d), lambda bb, t: (0, 0)),
            pl.BlockSpec((None, tm, d), lambda bb, t: (bb, t, 0)),
            _mod_spec(gate_j, d, ctx_row, 0),
            _row_spec(d),
            _row_spec(d),
        ],
        out_specs=pl.BlockSpec((None, tm, d), lambda bb, t: (bb, t, 0)),
        out_shape=jax.ShapeDtypeStruct((b, l, d), F32),
        compiler_params=_cparams(2),
        name="proj_post",
    )(a, w, x, mod_i, ln_g, ln_b)


def _fnet_a_kernel(x_ref, sh_ref, sc_ref, cs_ref, o_ref, *, d, gc):
    u = _modulate(x_ref[...], sh_ref[...], sc_ref[...]).astype(BF16)
    cs = cs_ref[...]
    for g in range(d // gc):
        r = _dot(u[:, g * gc:(g + 1) * gc], cs)
        o_ref[:, g * gc:(g + 1) * gc] = r[:, :gc].astype(BF16)
        o_ref[:, d + g * gc:d + (g + 1) * gc] = r[:, gc:].astype(BF16)


def _fnet_a(x, mod_i, ctx_row, cs):
    b, l, d = x.shape
    gc = d // F_GROUPS
    tm = min(ROW_TILE, l)
    assert l % tm == 0
    return pl.pallas_call(
        functools.partial(_fnet_a_kernel, d=d, gc=gc),
        grid=(b, l // tm),
        in_specs=[
            pl.BlockSpec((None, tm, d), lambda bb, t: (bb, t, 0)),
            _mod_spec(0, d, ctx_row, 0),
            _mod_spec(1, d, ctx_row, 0),
            pl.BlockSpec((gc, 2 * gc), lambda bb, t: (0, 0)),
        ],
        out_specs=pl.BlockSpec((None, tm, 2 * d), lambda bb, t: (bb, t, 0)),
        out_shape=jax.ShapeDtypeStruct((b, l, 2 * d), BF16),
        compiler_params=_cparams(2),
        name="fnet_chan_dft",
    )(x, mod_i, mod_i, cs)


def _fnet_b_kernel(cl_ref, sl_ref, ab_ref, wf_ref, x_ref, gate_ref, g_ref, b_ref, o_ref, *, d, alpha):
    f = _dot(cl_ref[...], ab_ref[:, :d]) + _dot(sl_ref[...], ab_ref[:, d:])
    y = _dot(f.astype(BF16), wf_ref[...])
    z = alpha * x_ref[...] + gate_ref[...] * y
    o_ref[...] = _normalize(z, LN_EPS) * g_ref[...] + b_ref[...]


def _fnet_b(cl, sl, ab, wf, x, mod_i, gate_j, ctx_row, ln_g, ln_b, alpha):
    b, l, d = x.shape
    tm = min(ROW_TILE, l)
    assert l % tm == 0
    return pl.pallas_call(
        functools.partial(_fnet_b_kernel, d=d, alpha=alpha),
        grid=(b, l // tm),
        in_specs=[
            pl.BlockSpec((tm, l), lambda bb, t: (t, 0)),
            pl.BlockSpec((tm, l), lambda bb, t: (t, 0)),
            pl.BlockSpec((None, l, 2 * d), lambda bb, t: (bb, 0, 0)),
            pl.BlockSpec((d, d), lambda bb, t: (0, 0)),
            pl.BlockSpec((None, tm, d), lambda bb, t: (bb, t, 0)),
            _mod_spec(gate_j, d, ctx_row, 0),
            _row_spec(d),
            _row_spec(d),
        ],
        out_specs=pl.BlockSpec((None, tm, d), lambda bb, t: (bb, t, 0)),
        out_shape=jax.ShapeDtypeStruct((b, l, d), F32),
        compiler_params=_cparams(2),
        name="fnet_seq_dft",
    )(cl, sl, ab, wf, x, mod_i, ln_g, ln_b)


def _dft_tables(n, scale):
    j = jnp.arange(n, dtype=jnp.int32)
    ang = ((j[:, None] * j[None, :]) % n).astype(F32) * (2.0 * math.pi / n)
    return jnp.cos(ang) * scale, jnp.sin(ang) * scale


def _router_kernel(x_ref, sh_ref, sc_ref, rwt_ref, bias_ref, tri_ref, u_ref, te_ref, w_ref, rk_ref, cnt_ref,
                   *, n_groups, topk_groups, top_k):
    u = _modulate(x_ref[...], sh_ref[...], sc_ref[...])
    uh, ul = _split_bf16(u)
    u_ref[...] = uh
    wh, wl = _split_bf16(rwt_ref[...])
    logits = _dot_nt(wh, uh) + _dot_nt(wl, uh) + _dot_nt(wh, ul)
    e, tm = logits.shape
    pg = e // n_groups
    neg = -jnp.inf
    scores = jax.nn.sigmoid(logits)
    sel = scores + bias_ref[...]
    sub = lax.broadcasted_iota(jnp.int32, (pg, tm), 0)
    groups = [sel[g * pg:(g + 1) * pg, :] for g in range(n_groups)]
    sgroups = [scores[g * pg:(g + 1) * pg, :] for g in range(n_groups)]

    gs_rows = []
    for s_g in groups:
        m1 = jnp.max(s_g, axis=0, keepdims=True)
        first = jnp.min(jnp.where(s_g == m1, sub, pg), axis=0, keepdims=True)
        m2 = jnp.max(jnp.where(sub == first, neg, s_g), axis=0, keepdims=True)
        gs_rows.append(m1 + m2)
    gs = jnp.concatenate(gs_rows, axis=0)
    gidx = lax.broadcasted_iota(jnp.int32, (n_groups, tm), 0)
    ok = jnp.zeros((n_groups, tm), F32)
    for _ in range(topk_groups):
        m = jnp.max(gs, axis=0, keepdims=True)
        first = jnp.min(jnp.where(gs == m, gidx, n_groups), axis=0, keepdims=True)
        hit = gidx == first
        ok = jnp.where(hit, 1.0, ok)
        gs = jnp.where(hit, neg, gs)

    cur = [jnp.where(ok[g:g + 1, :] > 0.0, groups[g], neg) for g in range(n_groups)]
    eidx = [sub + g * pg for g in range(n_groups)]
    chosen = [jnp.zeros((pg, tm), F32) for _ in range(n_groups)]
    e_rows, s_rows = [], []
    for _ in range(top_k):
        m = jnp.max(functools.reduce(jnp.maximum, cur), axis=0, keepdims=True)
        cand = functools.reduce(jnp.minimum, [jnp.where(cur[g] == m, eidx[g], e) for g in range(n_groups)])
        first = jnp.min(cand, axis=0, keepdims=True)
        picked = jnp.zeros((pg, tm), F32)
        for g in range(n_groups):
            hit = eidx[g] == first
            picked = picked + jnp.where(hit, sgroups[g], 0.0)
            chosen[g] = jnp.where(hit, 1.0, chosen[g])
            cur[g] = jnp.where(hit, neg, cur[g])
        e_rows.append(first)
        s_rows.append(jnp.sum(picked, axis=0, keepdims=True))
    w = jnp.concatenate(s_rows, axis=0)
    te_ref[...] = jnp.concatenate(e_rows, axis=0)
    w_ref[...] = w / jnp.sum(w, axis=0, keepdims=True) * ROUTED_SCALE

    sel_mask = jnp.concatenate(chosen, axis=0).astype(BF16)
    before = _dot(sel_mask, tri_ref[...])
    rk_rows = []
    for k in range(top_k):
        acc = jnp.zeros((pg, tm), F32)
        for g in range(n_groups):
            acc = acc + jnp.where(eidx[g] == e_rows[k], before[g * pg:(g + 1) * pg, :], 0.0)
        rk_rows.append(jnp.sum(acc, axis=0, keepdims=True))
    rk_ref[...] = jnp.concatenate(rk_rows, axis=0).astype(jnp.int32)
    cnt_ref[...] = _dot_nt(jnp.ones((8, tm), BF16), sel_mask)


def _router(x, mod_i, ctx_row, rwt, bias):
    b, l, d = x.shape
    e = rwt.shape[0]
    tm = min(ROW_TILE, l)
    assert l % tm == 0 and e % N_EXPERT_GROUPS == 0
    nt = l // tm
    t = b * l
    row = lax.broadcasted_iota(jnp.int32, (tm, tm), 0)
    col = lax.broadcasted_iota(jnp.int32, (tm, tm), 1)
    tri = (row < col).astype(BF16)
    kern = functools.partial(_router_kernel, n_groups=N_EXPERT_GROUPS, topk_groups=TOPK_GROUPS, top_k=TOP_K)
    tok_spec = pl.BlockSpec((TOP_K, tm), lambda bb, tt: (0, bb * nt + tt))
    return pl.pallas_call(
        kern,
        grid=(b, nt),
        in_specs=[
            pl.BlockSpec((None, tm, d), lambda bb, tt: (bb, tt, 0)),
            _mod_spec(3, d, ctx_row, 0),
            _mod_spec(4, d, ctx_row, 0),
            pl.BlockSpec((e, d), lambda bb, tt: (0, 0)),
            pl.BlockSpec((e, 1), lambda bb, tt: (0, 0)),
            pl.BlockSpec((tm, tm), lambda bb, tt: (0, 0)),
        ],
        out_specs=[
            pl.BlockSpec((None, tm, d), lambda bb, tt: (bb, tt, 0)),
            tok_spec, tok_spec, tok_spec,
            pl.BlockSpec((None, 8, e), lambda bb, tt: (bb * nt + tt, 0, 0)),
        ],
        out_shape=[
            jax.ShapeDtypeStruct((b, l, d), BF16),
            jax.ShapeDtypeStruct((TOP_K, t), jnp.int32),
            jax.ShapeDtypeStruct((TOP_K, t), F32),
            jax.ShapeDtypeStruct((TOP_K, t), jnp.int32),
            jax.ShapeDtypeStruct((b * nt, 8, e), F32),
        ],
        compiler_params=_cparams(2),
        name="moe_router",
    )(x, mod_i, mod_i, rwt, bias.reshape(e, 1).astype(F32), tri)


def _expert_layout(cnt, tmb, n_blocks):
    cnt_i = cnt[:, 0, :].astype(jnp.int32)
    e = cnt_i.shape[1]
    counts = cnt_i.sum(axis=0)
    padded = (counts + tmb - 1) // tmb * tmb
    pend = jnp.cumsum(padded)
    base = (pend - padded)[None, :] + jnp.cumsum(cnt_i, axis=0) - cnt_i
    n_used = (pend[-1] // tmb).astype(jnp.int32)
    blk = jnp.arange(n_blocks, dtype=jnp.int32)
    block_e = jnp.sum((blk[:, None] * tmb >= pend[None, :]).astype(jnp.int32), axis=1)
    last_e = jnp.sum((jnp.maximum(n_used - 1, 0) * tmb >= pend).astype(jnp.int32))
    block_e = jnp.clip(jnp.where(blk < n_used, block_e, last_e), 0, e - 1).astype(jnp.int32)
    return base.reshape(-1).astype(jnp.int32), block_e, n_used.reshape(1)


def _pos_kernel(base_ref, te_ref, rk_ref, pos_ref, *, n_experts):
    i = pl.program_id(0)
    te = te_ref[...]
    pos = rk_ref[...]
    for e in range(n_experts):
        pos = pos + jnp.where(te == e, base_ref[i * n_experts + e], 0)
    pos_ref[...] = pos


def _positions(base, top_e, rank, n_experts, tm):
    k, t = top_e.shape
    spec = pl.BlockSpec((k, tm), lambda i, base_ref: (0, i))
    return pl.pallas_call(
        functools.partial(_pos_kernel, n_experts=n_experts),
        grid_spec=pltpu.PrefetchScalarGridSpec(
            num_scalar_prefetch=1, grid=(t // tm,), in_specs=[spec, spec], out_specs=spec),
        out_shape=jax.ShapeDtypeStruct((k, t), jnp.int32),
        compiler_params=_cparams(1),
        name="moe_positions",
    )(base, top_e, rank)


def _gmm_kernel(be_ref, nu_ref, x_ref, wg_ref, wu_ref, wd_ref, o_ref):
    @pl.when(pl.program_id(0) < nu_ref[0])
    def _():
        x = x_ref[...]
        h = _silu(_dot(x, wg_ref[...])) * _dot(x, wu_ref[...])
        o_ref[...] = _dot(h.astype(BF16), wd_ref[...]).astype(o_ref.dtype)


def _grouped_experts(x_sorted, block_e, n_used, wg, wu, wd, tmb):
    cap, d = x_sorted.shape
    ff = wg.shape[2]
    n_blocks = cap // tmb

    def row_map(i, be, nu):
        return (jnp.minimum(i, nu[0] - 1), 0)

    def w_map(i, be, nu):
        return (be[i], 0, 0)

    return pl.pallas_call(
        _gmm_kernel,
        grid_spec=pltpu.PrefetchScalarGridSpec(
            num_scalar_prefetch=2,
            grid=(n_blocks,),
            in_specs=[
                pl.BlockSpec((tmb, d), row_map),
                pl.BlockSpec((None, d, ff), w_map),
                pl.BlockSpec((None, d, ff), w_map),
                pl.BlockSpec((None, ff, d), w_map),
            ],
            out_specs=pl.BlockSpec((tmb, d), row_map),
        ),
        out_shape=jax.ShapeDtypeStruct((cap, d), BF16),
        compiler_params=_cparams(1),
        name="moe_experts",
    )(block_e, n_used, x_sorted, wg, wu, wd)


def _moe_out_kernel(u_ref, w_ref, *refs, alpha, top_k):
    y_refs = refs[:top_k]
    sg_ref, su_ref, sd_ref, x_ref, gate_ref, g_ref, b_ref, o_ref = refs[top_k:]
    u = u_ref[...]
    h = _silu(_dot(u, sg_ref[...])) * _dot(u, su_ref[...])
    y = _dot(h.astype(BF16), sd_ref[...])
    w = w_ref[...]
    for k in range(top_k):
        y = y + w[:, k:k + 1] * y_refs[k][...].astype(F32)
    z = alpha * x_ref[...] + gate_ref[...] * y
    o_ref[...] = _normalize(z, LN_EPS) * g_ref[...] + b_ref[...]


def _moe_out(u2, wts, ys, sg, su, sd, x, mod_i, ctx_row, ln_g, ln_b, alpha):
    b, l, d = x.shape
    ff = sg.shape[1]
    top_k = len(ys)
    tm = min(ROW_TILE, l)
    assert l % tm == 0
    tok = pl.BlockSpec((None, tm, d), lambda bb, t: (bb, t, 0))
    return pl.pallas_call(
        functools.partial(_moe_out_kernel, alpha=alpha, top_k=top_k),
        grid=(b, l // tm),
        in_specs=[tok, pl.BlockSpec((None, tm, top_k), lambda bb, t: (bb, t, 0))] + [tok] * top_k + [
            pl.BlockSpec((d, ff), lambda bb, t: (0, 0)),
            pl.BlockSpec((d, ff), lambda bb, t: (0, 0)),
            pl.BlockSpec((ff, d), lambda bb, t: (0, 0)),
            tok,
            _mod_spec(5, d, ctx_row, 0),
            _row_spec(d),
            _row_spec(d),
        ],
        out_specs=tok,
        out_shape=jax.ShapeDtypeStruct((b, l, d), F32),
        compiler_params=_cparams(2),
        name="moe_shared_post",
    )(u2, wts, *ys, sg, su, sd, x, mod_i, ln_g, ln_b)


def _moe_layer(x, mod_i, ctx_row, rwt, rbias, wg, wu, wd, sg, su, sd, ln_g, ln_b, alpha):
    b, l, d = x.shape
    t = b * l
    e = rwt.shape[0]
    tmb = MOE_ROW_TILE
    n_blocks = -(-(t * TOP_K) // tmb) + e
    u2, top_e, wts, rank, cnt = _router(x, mod_i, ctx_row, rwt, rbias)
    base, block_e, n_used = _expert_layout(cnt, tmb, n_blocks)
    pos = _positions(base, top_e, rank, e, min(ROW_TILE, l))
    tok_ids = jnp.tile(jnp.arange(t, dtype=jnp.int32), TOP_K)
    tok_buf = jnp.zeros((n_blocks * tmb,), jnp.int32).at[pos.reshape(-1)].set(
        tok_ids, unique_indices=True, mode="promise_in_bounds")
    x_sorted = u2.reshape(t, d).at[tok_buf].get(mode="promise_in_bounds")
    y_sorted = _grouped_experts(x_sorted, block_e, n_used, wg, wu, wd, tmb)
    ys = [y_sorted.at[pos[k]].get(mode="promise_in_bounds", unique_indices=True).reshape(b, l, d)
          for k in range(TOP_K)]
    return _moe_out(u2, wts.T.reshape(b, l, TOP_K), ys, sg, su, sd, x, mod_i, ctx_row, ln_g, ln_b, alpha)


def _rope_tables(l, hd, n_q_heads, n_k_heads, rope):
    axis_rot = hd // 2
    qscale = hd ** -0.5
    if rope:
        t = jnp.arange(l, dtype=jnp.int32)
        r = (t // GRID_W).astype(F32)
        col = (t % GRID_W).astype(F32)
        inv = ROPE_BASE ** (-jnp.arange(0, axis_rot, 2, dtype=F32) / axis_rot)
        ar = r[:, None] * inv
        ac = col[:, None] * inv
        ang = jnp.concatenate([ar, ar, ac, ac], axis=-1)
        cos, sin = jnp.cos(ang), jnp.sin(ang)
    else:
        cos, sin = jnp.ones((l, hd), F32), jnp.zeros((l, hd), F32)
    cos_t = jnp.concatenate([jnp.tile(cos, (1, n_q_heads)) * qscale, jnp.tile(cos, (1, n_k_heads))], axis=1)
    sin_t = jnp.concatenate([jnp.tile(sin, (1, n_q_heads)) * qscale, jnp.tile(sin, (1, n_k_heads))], axis=1)
    return cos_t, sin_t


def _rot_columns(w, hd):
    d, n = w.shape
    q = hd // 4
    w4 = w.reshape(d, n // (2 * q), 2, q)
    return jnp.stack([-w4[:, :, 1, :], w4[:, :, 0, :]], axis=2).reshape(d, n)


def kernel(x, c, ctx, c_ctx, w_ada, b_ada, ln_g, ln_b, attn_w_qkv, attn_w_o, attn_sinks, fnet_w,
           router_w, router_bias, exp_w_gate, exp_w_up, exp_w_down, sh_w_gate, sh_w_up, sh_w_down):
    b, l, d = x.shape
    cl = ctx.shape[1]
    depth = w_ada.shape[0]
    n_heads = attn_sinks.shape[1]
    hd = d // n_heads
    qw = n_heads * hd
    kvw = N_KV_HEADS * hd
    alpha = (2.0 * depth) ** 0.25
    gc = d // F_GROUPS

    mp = -(-(b + 1) // 16) * 16
    cvec = jnp.concatenate([c, c_ctx[None, :], jnp.zeros((mp - b - 1, d), F32)], axis=0)
    mod = _ada_all(cvec, w_ada, b_ada).reshape(depth, mp, 6, 1, d)

    h = ctx
    for i in range(depth):
        kind = i % N_MIXERS
        j = i // N_MIXERS
        update_ctx = any((m % N_MIXERS) == 0 for m in range(i + 1, depth))
        mod_i = mod[i]
        g1 = ln_g[i, 0].reshape(1, d)
        b1 = ln_b[i, 0].reshape(1, d)
        g2 = ln_g[i, 1].reshape(1, d)
        b2 = ln_b[i, 1].reshape(1, d)

        if kind == 0:
            w = attn_w_qkv[j]
            w_qk = w[:, :qw + kvw]
            w_all = jnp.concatenate([w, _rot_columns(w_qk, hd)], axis=1).astype(BF16)
            w_o = attn_w_o[j].astype(BF16)
            cos_l, sin_l = _rope_tables(l, hd, n_heads, N_KV_HEADS, True)
            cos_c, sin_c = _rope_tables(cl, hd, n_heads, N_KV_HEADS, False)
            q, k, v = _qkv_proj(x, mod_i, None, w_all, cos_l, sin_l, qw, kvw)
            q_c, k_c, v_c = _qkv_proj(h, mod_i, b, w_all, cos_c, sin_c, qw, kvw)
            o = _attention(q, k, v, k_c, v_c, attn_sinks[j], True)
            x = _proj_post(o, w_o, x, mod_i, 2, None, g1, b1, alpha)
            if update_ctx:
                o_c = _attention(q_c, None, None, k_c, v_c, attn_sinks[j], False)
                h = _proj_post(o_c, w_o, h, mod_i, 2, b, g1, b1, alpha)
        else:
            wf = fnet_w[j].astype(BF16)
            cc, sc = _dft_tables(gc, gc ** -0.5)
            cs = jnp.concatenate([cc, sc], axis=1).astype(BF16)
            streams = [(x, None)] + ([(h, b)] if update_ctx else [])
            outs = []
            for s, ctx_row in streams:
                ls = s.shape[1]
                c_l, s_l = _dft_tables(ls, ls ** -0.5)
                ab = _fnet_a(s, mod_i, ctx_row, cs)
                outs.append(_fnet_b(c_l.astype(BF16), (-s_l).astype(BF16), ab, wf, s, mod_i, 2, ctx_row,
                                    g1, b1, alpha))
            x = outs[0]
            if update_ctx:
                h = outs[1]

        rwt = router_w[i].T
        wg = exp_w_gate[i].astype(BF16)
        wu = exp_w_up[i].astype(BF16)
        wd = exp_w_down[i].astype(BF16)
        sg = sh_w_gate[i].astype(BF16)
        su = sh_w_up[i].astype(BF16)
        sd = sh_w_down[i].astype(BF16)
        x = _moe_layer(x, mod_i, None, rwt, router_bias[i], wg, wu, wd, sg, su, sd, g2, b2, alpha)
        if update_ctx:
            h = _moe_layer(h, mod_i, b, rwt, router_bias[i], wg, wu, wd, sg, su, sd, g2, b2, alpha)
    return x
```

```python
import functools
import math

import jax
import jax.numpy as jnp
from jax import lax
from jax.experimental import pallas as pl
from jax.experimental.pallas import tpu as pltpu
from jax.experimental.pallas import tpu_sc as plsc

F32 = jnp.float32
BF16 = jnp.bfloat16
I32 = jnp.int32

N_KV_HEADS = 4
WINDOW = 128
GRID_W = 64
ROPE_BASE = 10000.0
F_GROUPS = 4
TOP_K = 8
N_EXPERT_GROUPS = 8
TOPK_GROUPS = 4
ROUTED_SCALE = 2.5
N_MIXERS = 2
LN_EPS = 1e-5
MOD_EPS = 1e-6

LANES = 128
VMEM_LIMIT_BYTES = 52 * 1024 * 1024
ROW_TILE = 512
QKV_ROW_TILE = 256
Q_TILE = 128
MOE_ROW_TILE = 512
DISPATCH_CHUNK = 128
SC_CORES = 2
SC_SUBCORES = 16
NEG_BIG = -1e30


def _cparams(n_axes):
    return pltpu.CompilerParams(dimension_semantics=("arbitrary",) * n_axes,
                                vmem_limit_bytes=VMEM_LIMIT_BYTES)


def _dot(a, b):
    return jnp.dot(a, b, preferred_element_type=F32)


def _dot_nt(a, b):
    return lax.dot_general(a, b, (((1,), (1,)), ((), ())), preferred_element_type=F32)


def _split_bf16(a):
    hi = a.astype(BF16)
    lo = (a - hi.astype(F32)).astype(BF16)
    return hi, lo


def _normalize(x, eps):
    mu = jnp.mean(x, axis=-1, keepdims=True)
    xc = x - mu
    var = jnp.mean(xc * xc, axis=-1, keepdims=True)
    return xc * lax.rsqrt(var + eps)


def _modulate(x, shift, scale):
    return _normalize(x, MOD_EPS) * (1.0 + scale) + shift


def _silu(x):
    return x * jax.nn.sigmoid(x)


def _pack_halves(x):
    n = x.shape[1] // 2
    r = x.astype(BF16).astype(F32)
    hi = pltpu.bitcast(r[:, :n], I32)
    lo = pltpu.bitcast(r[:, n:], I32)
    return hi | lax.shift_right_logical(lo, 16)


def _unpack_halves(w):
    a = pltpu.bitcast(w & jnp.int32(-65536), F32).astype(BF16)
    b = pltpu.bitcast(lax.shift_left(w, 16), F32).astype(BF16)
    return a, b


def _ada_kernel(c_ref, w_ref, b_ref, o_ref):
    s = _silu(c_ref[...])
    sh, sl = _split_bf16(s)
    wh, wl = _split_bf16(w_ref[...])
    o_ref[...] = _dot(sh, wh) + _dot(sl, wh) + _dot(sh, wl) + b_ref[...]


def _ada_all(cvec, w_ada, b_ada):
    depth, d, n = w_ada.shape
    mp = cvec.shape[0]
    tn = 1536
    assert n % tn == 0
    return pl.pallas_call(
        _ada_kernel,
        grid=(depth, n // tn),
        in_specs=[
            pl.BlockSpec((mp, d), lambda i, j: (0, 0)),
            pl.BlockSpec((None, d, tn), lambda i, j: (i, 0, j)),
            pl.BlockSpec((None, 1, tn), lambda i, j: (i, 0, j)),
        ],
        out_specs=pl.BlockSpec((None, mp, tn), lambda i, j: (i, 0, j)),
        out_shape=jax.ShapeDtypeStruct((depth, mp, n), F32),
        compiler_params=_cparams(2),
        name="ada_mod",
    )(cvec, w_ada, b_ada.reshape(depth, 1, n))


def _mod_spec(j, d, ctx_row, batch_axis):
    if ctx_row is None:
        return pl.BlockSpec((None, None, 1, d), lambda *g: (g[batch_axis], j, 0, 0))
    return pl.BlockSpec((None, None, 1, d), lambda *g: (ctx_row, j, 0, 0))


def _row_spec(d):
    return pl.BlockSpec((1, d), lambda *g: (0, 0))


def _qkv_kernel(x_ref, sh_ref, sc_ref, w_ref, cos_ref, sin_ref, q_ref, k_ref, v_ref, *, qw, kvw):
    u = _modulate(x_ref[...], sh_ref[...], sc_ref[...]).astype(BF16)
    r = _dot(u, w_ref[...])
    qk = r[:, :qw + kvw] * cos_ref[...] + r[:, qw + 2 * kvw:] * sin_ref[...]
    q_ref[...] = qk[:, :qw].astype(BF16)
    k_ref[...] = qk[:, qw:].astype(BF16)
    v_ref[...] = r[:, qw + kvw:qw + 2 * kvw].astype(BF16)


def _qkv_proj(x, mod_i, ctx_row, w_all, cos_t, sin_t, qw, kvw):
    b, l, d = x.shape
    tm = min(QKV_ROW_TILE, l)
    assert l % tm == 0
    n_all = w_all.shape[1]
    return pl.pallas_call(
        functools.partial(_qkv_kernel, qw=qw, kvw=kvw),
        grid=(l // tm, b),
        in_specs=[
            pl.BlockSpec((None, tm, d), lambda t, bb: (bb, t, 0)),
            _mod_spec(0, d, ctx_row, 1),
            _mod_spec(1, d, ctx_row, 1),
            pl.BlockSpec((d, n_all), lambda t, bb: (0, 0)),
            pl.BlockSpec((tm, qw + kvw), lambda t, bb: (t, 0)),
            pl.BlockSpec((tm, qw + kvw), lambda t, bb: (t, 0)),
        ],
        out_specs=[
            pl.BlockSpec((None, tm, qw), lambda t, bb: (bb, t, 0)),
            pl.BlockSpec((None, tm, kvw), lambda t, bb: (bb, t, 0)),
            pl.BlockSpec((None, tm, kvw), lambda t, bb: (bb, t, 0)),
        ],
        out_shape=[
            jax.ShapeDtypeStruct((b, l, qw), BF16),
            jax.ShapeDtypeStruct((b, l, kvw), BF16),
            jax.ShapeDtypeStruct((b, l, kvw), BF16),
        ],
        compiler_params=_cparams(2),
        name="qkv_rope",
    )(x, mod_i, mod_i, w_all, cos_t, sin_t)


def _attn_kernel(sink_ref, q_ref, *refs, tq, seq, n_kv, group, hd, has_window):
    if has_window:
        k_ref, v_ref, kc_ref, vc_ref, o_ref = refs
    else:
        kc_ref, vc_ref, o_ref = refs
    q = q_ref[...]
    kc = kc_ref[...]
    vc = vc_ref[...]
    if has_window:
        span = tq + 2 * WINDOW
        q0 = pl.program_id(1) * tq
        start = pl.multiple_of(jnp.clip(q0 - WINDOW, 0, seq - span), LANES)
        kw = k_ref[pl.ds(start, span), :]
        vw = v_ref[pl.ds(start, span), :]
        qpos = q0 + lax.broadcasted_iota(jnp.int32, (tq, span), 0)
        kpos = start + lax.broadcasted_iota(jnp.int32, (tq, span), 1)
        mask1 = jnp.abs(qpos - kpos) <= WINDOW
        mask = jnp.concatenate([mask1] * group, axis=0)
    for h in range(n_kv):
        heads = [h * group + g for g in range(group)]
        qh = jnp.concatenate([q[:, j * hd:(j + 1) * hd] for j in heads], axis=0)
        sink = jnp.concatenate([jnp.full((tq, 1), sink_ref[j], F32) for j in heads], axis=0)
        sc = _dot_nt(qh, kc[:, h * hd:(h + 1) * hd])
        m = jnp.maximum(jnp.max(sc, axis=-1, keepdims=True), sink)
        if has_window:
            sw = jnp.where(mask, _dot_nt(qh, kw[:, h * hd:(h + 1) * hd]), NEG_BIG)
            m = jnp.maximum(m, jnp.max(sw, axis=-1, keepdims=True))
        pc = jnp.exp(sc - m)
        den = jnp.sum(pc, axis=-1, keepdims=True) + jnp.exp(sink - m)
        o = _dot(pc.astype(BF16), vc[:, h * hd:(h + 1) * hd])
        if has_window:
            pw = jnp.exp(sw - m)
            den = den + jnp.sum(pw, axis=-1, keepdims=True)
            o = o + _dot(pw.astype(BF16), vw[:, h * hd:(h + 1) * hd])
        o = o / den
        for g, j in enumerate(heads):
            o_ref[:, j * hd:(j + 1) * hd] = o[g * tq:(g + 1) * tq, :].astype(o_ref.dtype)


def _attention(q, k, v, kc, vc, sinks, has_window):
    b, l, qw = q.shape
    c, kvw = kc.shape[1], kc.shape[2]
    hd = kvw // N_KV_HEADS
    group = qw // kvw
    tq = Q_TILE if has_window else l
    assert l % tq == 0
    if has_window:
        assert l >= tq + 2 * WINDOW
    kern = functools.partial(_attn_kernel, tq=tq, seq=l, n_kv=N_KV_HEADS, group=group, hd=hd,
                             has_window=has_window)
    in_specs = [pl.BlockSpec(memory_space=pltpu.SMEM),
                pl.BlockSpec((None, tq, qw), lambda bb, t: (bb, t, 0))]
    args = [sinks, q]
    if has_window:
        in_specs += [pl.BlockSpec((None, l, kvw), lambda bb, t: (bb, 0, 0)),
                     pl.BlockSpec((None, l, kvw), lambda bb, t: (bb, 0, 0))]
        args += [k, v]
    in_specs += [pl.BlockSpec((None, c, kvw), lambda bb, t: (bb, 0, 0)),
                 pl.BlockSpec((None, c, kvw), lambda bb, t: (bb, 0, 0))]
    args += [kc, vc]
    return pl.pallas_call(
        kern,
        grid=(b, l // tq),
        in_specs=in_specs,
        out_specs=pl.BlockSpec((None, tq, qw), lambda bb, t: (bb, t, 0)),
        out_shape=jax.ShapeDtypeStruct((b, l, qw), BF16),
        compiler_params=_cparams(2),
        name="win_attn" if has_window else "ctx_attn",
    )(*args)


def _proj_post_kernel(a_ref, w_ref, x_ref, gate_ref, g_ref, b_ref, o_ref, *, alpha):
    y = _dot(a_ref[...], w_ref[...])
    z = alpha * x_ref[...] + gate_ref[...] * y
    o_ref[...] = _normalize(z, LN_EPS) * g_ref[...] + b_ref[...]


def _proj_post(a, w, x, mod_i, gate_j, ctx_row, ln_g, ln_b, alpha):
    b, l, d = x.shape
    ka = a.shape[2]
    tm = min(ROW_TILE, l)
    assert l % tm == 0
    return pl.pallas_call(
        functools.partial(_proj_post_kernel, alpha=alpha),
        grid=(b, l // tm),
        in_specs=[
            pl.BlockSpec((None, tm, ka), lambda bb, t: (bb, t, 0)),
            pl.BlockSpec((ka, d), lambda bb, t: (0, 0)),
            pl.BlockSpec((None, tm, d), lambda bb, t: (bb, t, 0)),
            _mod_spec(gate_j, d, ctx_row, 0),
            _row_spec(d),
            _row_spec(d),
        ],
        out_specs=pl.BlockSpec((None, tm, d), lambda bb, t: (bb, t, 0)),
        out_shape=jax.ShapeDtypeStruct((b, l, d), F32),
        compiler_params=_cparams(2),
        name="proj_post",
    )(a, w, x, mod_i, ln_g, ln_b)


def _fnet_a_kernel(x_ref, sh_ref, sc_ref, cs_ref, o_ref, *, d, gc):
    u = _modulate(x_ref[...], sh_ref[...], sc_ref[...]).astype(BF16)
    cs = cs_ref[...]
    for g in range(d // gc):
        r = _dot(u[:, g * gc:(g + 1) * gc], cs)
        o_ref[:, g * gc:(g + 1) * gc] = r[:, :gc].astype(BF16)
        o_ref[:, d + g * gc:d + (g + 1) * gc] = r[:, gc:].astype(BF16)


def _fnet_a(x, mod_i, ctx_row, cs):
    b, l, d = x.shape
    gc = d // F_GROUPS
    tm = min(ROW_TILE, l)
    assert l % tm == 0
    return pl.pallas_call(
        functools.partial(_fnet_a_kernel, d=d, gc=gc),
        grid=(b, l // tm),
        in_specs=[
            pl.BlockSpec((None, tm, d), lambda bb, t: (bb, t, 0)),
            _mod_spec(0, d, ctx_row, 0),
            _mod_spec(1, d, ctx_row, 0),
            pl.BlockSpec((gc, 2 * gc), lambda bb, t: (0, 0)),
        ],
        out_specs=pl.BlockSpec((None, tm, 2 * d), lambda bb, t: (bb, t, 0)),
        out_shape=jax.ShapeDtypeStruct((b, l, 2 * d), BF16),
        compiler_params=_cparams(2),
        name="fnet_chan_dft",
    )(x, mod_i, mod_i, cs)


def _fnet_b_kernel(cl_ref, sl_ref, ab_ref, wf_ref, x_ref, gate_ref, g_ref, b_ref, o_ref, *, d, alpha):
    f = _dot(cl_ref[...], ab_ref[:, :d]) + _dot(sl_ref[...], ab_ref[:, d:])
    y = _dot(f.astype(BF16), wf_ref[...])
    z = alpha * x_ref[...] + gate_ref[...] * y
    o_ref[...] = _normalize(z, LN_EPS) * g_ref[...] + b_ref[...]


def _fnet_b(cl, sl, ab, wf, x, mod_i, gate_j, ctx_row, ln_g, ln_b, alpha):
    b, l, d = x.shape
    tm = min(ROW_TILE, l)
    assert l % tm == 0
    return pl.pallas_call(
        functools.partial(_fnet_b_kernel, d=d, alpha=alpha),
        grid=(b, l // tm),
        in_specs=[
            pl.BlockSpec((tm, l), lambda bb, t: (t, 0)),
            pl.BlockSpec((tm, l), lambda bb, t: (t, 0)),
            pl.BlockSpec((None, l, 2 * d), lambda bb, t: (bb, 0, 0)),
            pl.BlockSpec((d, d), lambda bb, t: (0, 0)),
            pl.BlockSpec((None, tm, d), lambda bb, t: (bb, t, 0)),
            _mod_spec(gate_j, d, ctx_row, 0),
            _row_spec(d),
            _row_spec(d),
        ],
        out_specs=pl.BlockSpec((None, tm, d), lambda bb, t: (bb, t, 0)),
        out_shape=jax.ShapeDtypeStruct((b, l, d), F32),
        compiler_params=_cparams(2),
        name="fnet_seq_dft",
    )(cl, sl, ab, wf, x, mod_i, ln_g, ln_b)


def _dft_tables(n, scale):
    j = jnp.arange(n, dtype=jnp.int32)
    ang = ((j[:, None] * j[None, :]) % n).astype(F32) * (2.0 * math.pi / n)
    return jnp.cos(ang) * scale, jnp.sin(ang) * scale


def _router_kernel(x_ref, sh_ref, sc_ref, rwt_ref, bias_ref, tri_ref, u_ref, te_ref, w_ref, rk_ref, cnt_ref,
                   *, n_groups, topk_groups, top_k):
    u = _modulate(x_ref[...], sh_ref[...], sc_ref[...])
    uh, ul = _split_bf16(u)
    u_ref[...] = _pack_halves(u)
    wh, wl = _split_bf16(rwt_ref[...])
    logits = _dot_nt(wh, uh) + _dot_nt(wl, uh) + _dot_nt(wh, ul)
    e, tm = logits.shape
    pg = e // n_groups
    neg = -jnp.inf
    scores = jax.nn.sigmoid(logits)
    sel = scores + bias_ref[...]
    sub = lax.broadcasted_iota(jnp.int32, (pg, tm), 0)
    groups = [sel[g * pg:(g + 1) * pg, :] for g in range(n_groups)]
    sgroups = [scores[g * pg:(g + 1) * pg, :] for g in range(n_groups)]

    gs_rows = []
    for s_g in groups:
        m1 = jnp.max(s_g, axis=0, keepdims=True)
        first = jnp.min(jnp.where(s_g == m1, sub, pg), axis=0, keepdims=True)
        m2 = jnp.max(jnp.where(sub == first, neg, s_g), axis=0, keepdims=True)
        gs_rows.append(m1 + m2)
    gs = jnp.concatenate(gs_rows, axis=0)
    gidx = lax.broadcasted_iota(jnp.int32, (n_groups, tm), 0)
    ok = jnp.zeros((n_groups, tm), F32)
    for _ in range(topk_groups):
        m = jnp.max(gs, axis=0, keepdims=True)
        first = jnp.min(jnp.where(gs == m, gidx, n_groups), axis=0, keepdims=True)
        hit = gidx == first
        ok = jnp.where(hit, 1.0, ok)
        gs = jnp.where(hit, neg, gs)

    cur = [jnp.where(ok[g:g + 1, :] > 0.0, groups[g], neg) for g in range(n_groups)]
    eidx = [sub + g * pg for g in range(n_groups)]
    chosen = [jnp.zeros((pg, tm), F32) for _ in range(n_groups)]
    e_rows, s_rows = [], []
    for _ in range(top_k):
        m = jnp.max(functools.reduce(jnp.maximum, cur), axis=0, keepdims=True)
        cand = functools.reduce(jnp.minimum, [jnp.where(cur[g] == m, eidx[g], e) for g in range(n_groups)])
        first = jnp.min(cand, axis=0, keepdims=True)
        picked = jnp.zeros((pg, tm), F32)
        for g in range(n_groups):
            hit = eidx[g] == first
            picked = picked + jnp.where(hit, sgroups[g], 0.0)
            chosen[g] = jnp.where(hit, 1.0, chosen[g])
            cur[g] = jnp.where(hit, neg, cur[g])
        e_rows.append(first)
        s_rows.append(jnp.sum(picked, axis=0, keepdims=True))
    w = jnp.concatenate(s_rows, axis=0)
    te_ref[...] = jnp.concatenate(e_rows, axis=0)
    w_ref[...] = w / jnp.sum(w, axis=0, keepdims=True) * ROUTED_SCALE

    sel_mask = jnp.concatenate(chosen, axis=0).astype(BF16)
    before = _dot(sel_mask, tri_ref[...])
    rk_rows = []
    for k in range(top_k):
        acc = jnp.zeros((pg, tm), F32)
        for g in range(n_groups):
            acc = acc + jnp.where(eidx[g] == e_rows[k], before[g * pg:(g + 1) * pg, :], 0.0)
        rk_rows.append(jnp.sum(acc, axis=0, keepdims=True))
    rk_ref[...] = jnp.concatenate(rk_rows, axis=0).astype(jnp.int32)
    cnt_ref[...] = _dot_nt(jnp.ones((8, tm), BF16), sel_mask)


def _router(x, mod_i, ctx_row, rwt, bias):
    b, l, d = x.shape
    e = rwt.shape[0]
    tm = min(ROW_TILE, l)
    assert l % tm == 0 and e % N_EXPERT_GROUPS == 0
    nt = l // tm
    t = b * l
    row = lax.broadcasted_iota(jnp.int32, (tm, tm), 0)
    col = lax.broadcasted_iota(jnp.int32, (tm, tm), 1)
    tri = (row < col).astype(BF16)
    kern = functools.partial(_router_kernel, n_groups=N_EXPERT_GROUPS, topk_groups=TOPK_GROUPS, top_k=TOP_K)
    tok_spec = pl.BlockSpec((TOP_K, tm), lambda bb, tt: (0, bb * nt + tt))
    return pl.pallas_call(
        kern,
        grid=(b, nt),
        in_specs=[
            pl.BlockSpec((None, tm, d), lambda bb, tt: (bb, tt, 0)),
            _mod_spec(3, d, ctx_row, 0),
            _mod_spec(4, d, ctx_row, 0),
            pl.BlockSpec((e, d), lambda bb, tt: (0, 0)),
            pl.BlockSpec((e, 1), lambda bb, tt: (0, 0)),
            pl.BlockSpec((tm, tm), lambda bb, tt: (0, 0)),
        ],
        out_specs=[
            pl.BlockSpec((None, tm, d // 2), lambda bb, tt: (bb, tt, 0)),
            tok_spec, tok_spec, tok_spec,
            pl.BlockSpec((None, 8, e), lambda bb, tt: (bb * nt + tt, 0, 0)),
        ],
        out_shape=[
            jax.ShapeDtypeStruct((b, l, d // 2), I32),
            jax.ShapeDtypeStruct((TOP_K, t), jnp.int32),
            jax.ShapeDtypeStruct((TOP_K, t), F32),
            jax.ShapeDtypeStruct((TOP_K, t), jnp.int32),
            jax.ShapeDtypeStruct((b * nt, 8, e), F32),
        ],
        compiler_params=_cparams(2),
        name="moe_router",
    )(x, mod_i, mod_i, rwt, bias.reshape(e, 1).astype(F32), tri)


def _expert_layout(cnt, tmb, n_blocks):
    cnt_i = cnt[:, 0, :].astype(jnp.int32)
    e = cnt_i.shape[1]
    counts = cnt_i.sum(axis=0)
    padded = (counts + tmb - 1) // tmb * tmb
    pend = jnp.cumsum(padded)
    base = (pend - padded)[None, :] + jnp.cumsum(cnt_i, axis=0) - cnt_i
    n_used = (pend[-1] // tmb).astype(jnp.int32)
    blk = jnp.arange(n_blocks, dtype=jnp.int32)
    block_e = jnp.sum((blk[:, None] * tmb >= pend[None, :]).astype(jnp.int32), axis=1)
    last_e = jnp.sum((jnp.maximum(n_used - 1, 0) * tmb >= pend).astype(jnp.int32))
    block_e = jnp.clip(jnp.where(blk < n_used, block_e, last_e), 0, e - 1).astype(jnp.int32)
    seg_end = (pend - padded + counts)[block_e]
    block_valid = jnp.clip(seg_end - blk * tmb, 0, tmb).astype(jnp.int32)
    return base.reshape(-1).astype(jnp.int32), block_e, block_valid, n_used.reshape(1)


def _pos_kernel(base_ref, te_ref, rk_ref, pos_ref, *, n_experts):
    i = pl.program_id(0)
    te = te_ref[...]
    pos = rk_ref[...]
    for e in range(n_experts):
        pos = pos + jnp.where(te == e, base_ref[i * n_experts + e], 0)
    pos_ref[...] = pos


def _positions(base, top_e, rank, n_experts, tm):
    k, t = top_e.shape
    spec = pl.BlockSpec((k, tm), lambda i, base_ref: (0, i))
    return pl.pallas_call(
        functools.partial(_pos_kernel, n_experts=n_experts),
        grid_spec=pltpu.PrefetchScalarGridSpec(
            num_scalar_prefetch=1, grid=(t // tm,), in_specs=[spec, spec], out_specs=spec),
        out_shape=jax.ShapeDtypeStruct((k, t), jnp.int32),
        compiler_params=_cparams(1),
        name="moe_positions",
    )(base, top_e, rank)


def _sc_dispatch(rows, pos3, cap):
    t, w = rows.shape
    n_chunks, top_k, n = pos3.shape
    n_workers = SC_CORES * SC_SUBCORES
    assert n_chunks * n == t and n <= LANES
    per_worker = -(-n_chunks // n_workers)
    mesh = plsc.VectorSubcoreMesh(core_axis_name="core", subcore_axis_name="subcore",
                                  num_cores=SC_CORES, num_subcores=SC_SUBCORES)

    def body(rows_hbm, pos_hbm, out_hbm, idx_v, rows_v, sem):
        wid = lax.axis_index("subcore") * SC_CORES + lax.axis_index("core")

        @pl.loop(0, per_worker)
        def _(j):
            c = wid * per_worker + j

            @pl.when(c < n_chunks)
            def _():
                pltpu.sync_copy(pos_hbm.at[c], idx_v)
                pltpu.sync_copy(rows_hbm.at[pl.ds(c * n, n)], rows_v)
                copies = [pltpu.async_copy(rows_v, out_hbm.at[idx_v.at[k]], sem) for k in range(top_k)]
                for cp in copies:
                    cp.wait()

    return pl.kernel(
        body,
        out_type=jax.ShapeDtypeStruct((cap, w), I32),
        mesh=mesh,
        scratch_types=[pltpu.VMEM((top_k, n), I32), pltpu.VMEM((n, w), I32), pltpu.SemaphoreType.DMA],
        name="moe_dispatch_sc",
    )(rows, pos3)


def _gmm_kernel(be_ref, bv_ref, nu_ref, x_ref, wg_ref, wu_ref, wd_ref, o_ref):
    i = pl.program_id(0)

    @pl.when(i < nu_ref[0])
    def _():
        words = x_ref[...]
        tmb, half = words.shape
        live = lax.broadcasted_iota(I32, (tmb, half), 0) < bv_ref[i]
        xa, xb = _unpack_halves(jnp.where(live, words, 0))
        g = _dot(xa, wg_ref[:half, :]) + _dot(xb, wg_ref[half:, :])
        u = _dot(xa, wu_ref[:half, :]) + _dot(xb, wu_ref[half:, :])
        o_ref[...] = _dot((_silu(g) * u).astype(BF16), wd_ref[...]).astype(o_ref.dtype)


def _grouped_experts(x_sorted, block_e, block_valid, n_used, wg, wu, wd, tmb):
    cap, half = x_sorted.shape
    d = 2 * half
    ff = wg.shape[2]
    n_blocks = cap // tmb

    def row_map(i, be, bv, nu):
        return (jnp.minimum(i, nu[0] - 1), 0)

    def w_map(i, be, bv, nu):
        return (be[i], 0, 0)

    return pl.pallas_call(
        _gmm_kernel,
        grid_spec=pltpu.PrefetchScalarGridSpec(
            num_scalar_prefetch=3,
            grid=(n_blocks,),
            in_specs=[
                pl.BlockSpec((tmb, half), row_map),
                pl.BlockSpec((None, d, ff), w_map),
                pl.BlockSpec((None, d, ff), w_map),
                pl.BlockSpec((None, ff, d), w_map),
            ],
            out_specs=pl.BlockSpec((tmb, d), row_map),
        ),
        out_shape=jax.ShapeDtypeStruct((cap, d), BF16),
        compiler_params=_cparams(1),
        name="moe_experts",
    )(block_e, block_valid, n_used, x_sorted, wg, wu, wd)


def _moe_out_kernel(u_ref, w_ref, *refs, alpha, top_k):
    y_refs = refs[:top_k]
    sg_ref, su_ref, sd_ref, x_ref, gate_ref, g_ref, b_ref, o_ref = refs[top_k:]
    ua, ub = _unpack_halves(u_ref[...])
    half = ua.shape[1]
    g = _dot(ua, sg_ref[:half, :]) + _dot(ub, sg_ref[half:, :])
    s = _dot(ua, su_ref[:half, :]) + _dot(ub, su_ref[half:, :])
    y = _dot((_silu(g) * s).astype(BF16), sd_ref[...])
    w = w_ref[...]
    for k in range(top_k):
        y = y + w[:, k:k + 1] * y_refs[k][...].astype(F32)
    z = alpha * x_ref[...] + gate_ref[...] * y
    o_ref[...] = _normalize(z, LN_EPS) * g_ref[...] + b_ref[...]


def _moe_out(u2, wts, ys, sg, su, sd, x, mod_i, ctx_row, ln_g, ln_b, alpha):
    b, l, d = x.shape
    ff = sg.shape[1]
    top_k = len(ys)
    tm = min(ROW_TILE, l)
    assert l % tm == 0
    tok = pl.BlockSpec((None, tm, d), lambda bb, t: (bb, t, 0))
    return pl.pallas_call(
        functools.partial(_moe_out_kernel, alpha=alpha, top_k=top_k),
        grid=(b, l // tm),
        in_specs=[pl.BlockSpec((None, tm, d // 2), lambda bb, t: (bb, t, 0)),
                  pl.BlockSpec((None, tm, top_k), lambda bb, t: (bb, t, 0))] + [tok] * top_k + [
            pl.BlockSpec((d, ff), lambda bb, t: (0, 0)),
            pl.BlockSpec((d, ff), lambda bb, t: (0, 0)),
            pl.BlockSpec((ff, d), lambda bb, t: (0, 0)),
            tok,
            _mod_spec(5, d, ctx_row, 0),
            _row_spec(d),
            _row_spec(d),
        ],
        out_specs=tok,
        out_shape=jax.ShapeDtypeStruct((b, l, d), F32),
        compiler_params=_cparams(2),
        name="moe_shared_post",
    )(u2, wts, *ys, sg, su, sd, x, mod_i, ln_g, ln_b)


def _moe_layer(x, mod_i, ctx_row, rwt, rbias, wg, wu, wd, sg, su, sd, ln_g, ln_b, alpha):
    b, l, d = x.shape
    t = b * l
    e = rwt.shape[0]
    tmb = MOE_ROW_TILE
    n_blocks = -(-(t * TOP_K) // tmb) + e
    u2, top_e, wts, rank, cnt = _router(x, mod_i, ctx_row, rwt, rbias)
    base, block_e, block_valid, n_used = _expert_layout(cnt, tmb, n_blocks)
    pos = _positions(base, top_e, rank, e, min(ROW_TILE, l))
    pos3 = pos.reshape(TOP_K, t // DISPATCH_CHUNK, DISPATCH_CHUNK).transpose(1, 0, 2)
    x_sorted = _sc_dispatch(u2.reshape(t, d // 2), pos3, n_blocks * tmb)
    y_sorted = _grouped_experts(x_sorted, block_e, block_valid, n_used, wg, wu, wd, tmb)
    ys = [y_sorted.at[pos[k]].get(mode="promise_in_bounds", unique_indices=True).reshape(b, l, d)
          for k in range(TOP_K)]
    return _moe_out(u2, wts.T.reshape(b, l, TOP_K), ys, sg, su, sd, x, mod_i, ctx_row, ln_g, ln_b, alpha)


def _rope_tables(l, hd, n_q_heads, n_k_heads, rope):
    axis_rot = hd // 2
    qscale = hd ** -0.5
    if rope:
        t = jnp.arange(l, dtype=jnp.int32)
        r = (t // GRID_W).astype(F32)
        col = (t % GRID_W).astype(F32)
        inv = ROPE_BASE ** (-jnp.arange(0, axis_rot, 2, dtype=F32) / axis_rot)
        ar = r[:, None] * inv
        ac = col[:, None] * inv
        ang = jnp.concatenate([ar, ar, ac, ac], axis=-1)
        cos, sin = jnp.cos(ang), jnp.sin(ang)
    else:
        cos, sin = jnp.ones((l, hd), F32), jnp.zeros((l, hd), F32)
    cos_t = jnp.concatenate([jnp.tile(cos, (1, n_q_heads)) * qscale, jnp.tile(cos, (1, n_k_heads))], axis=1)
    sin_t = jnp.concatenate([jnp.tile(sin, (1, n_q_heads)) * qscale, jnp.tile(sin, (1, n_k_heads))], axis=1)
    return cos_t, sin_t


def _rot_columns(w, hd):
    d, n = w.shape
    q = hd // 4
    w4 = w.reshape(d, n // (2 * q), 2, q)
    return jnp.stack([-w4[:, :, 1, :], w4[:, :, 0, :]], axis=2).reshape(d, n)


def kernel(x, c, ctx, c_ctx, w_ada, b_ada, ln_g, ln_b, attn_w_qkv, attn_w_o, attn_sinks, fnet_w,
           router_w, router_bias, exp_w_gate, exp_w_up, exp_w_down, sh_w_gate, sh_w_up, sh_w_down):
    b, l, d = x.shape
    cl = ctx.shape[1]
    depth = w_ada.shape[0]
    n_heads = attn_sinks.shape[1]
    hd = d // n_heads
    qw = n_heads * hd
    kvw = N_KV_HEADS * hd
    alpha = (2.0 * depth) ** 0.25
    gc = d // F_GROUPS

    mp = -(-(b + 1) // 16) * 16
    cvec = jnp.concatenate([c, c_ctx[None, :], jnp.zeros((mp - b - 1, d), F32)], axis=0)
    mod = _ada_all(cvec, w_ada, b_ada).reshape(depth, mp, 6, 1, d)

    h = ctx
    for i in range(depth):
        kind = i % N_MIXERS
        j = i // N_MIXERS
        update_ctx = any((m % N_MIXERS) == 0 for m in range(i + 1, depth))
        mod_i = mod[i]
        g1 = ln_g[i, 0].reshape(1, d)
        b1 = ln_b[i, 0].reshape(1, d)
        g2 = ln_g[i, 1].reshape(1, d)
        b2 = ln_b[i, 1].reshape(1, d)

        if kind == 0:
            w = attn_w_qkv[j]
            w_qk = w[:, :qw + kvw]
            w_all = jnp.concatenate([w, _rot_columns(w_qk, hd)], axis=1).astype(BF16)
            w_o = attn_w_o[j].astype(BF16)
            cos_l, sin_l = _rope_tables(l, hd, n_heads, N_KV_HEADS, True)
            cos_c, sin_c = _rope_tables(cl, hd, n_heads, N_KV_HEADS, False)
            q, k, v = _qkv_proj(x, mod_i, None, w_all, cos_l, sin_l, qw, kvw)
            q_c, k_c, v_c = _qkv_proj(h, mod_i, b, w_all, cos_c, sin_c, qw, kvw)
            o = _attention(q, k, v, k_c, v_c, attn_sinks[j], True)
            x = _proj_post(o, w_o, x, mod_i, 2, None, g1, b1, alpha)
            if update_ctx:
                o_c = _attention(q_c, None, None, k_c, v_c, attn_sinks[j], False)
                h = _proj_post(o_c, w_o, h, mod_i, 2, b, g1, b1, alpha)
        else:
            wf = fnet_w[j].astype(BF16)
            cc, sc = _dft_tables(gc, gc ** -0.5)
            cs = jnp.concatenate([cc, sc], axis=1).astype(BF16)
            streams = [(x, None)] + ([(h, b)] if update_ctx else [])
            outs = []
            for s, ctx_row in streams:
                ls = s.shape[1]
                c_l, s_l = _dft_tables(ls, ls ** -0.5)
                ab = _fnet_a(s, mod_i, ctx_row, cs)
                outs.append(_fnet_b(c_l.astype(BF16), (-s_l).astype(BF16), ab, wf, s, mod_i, 2, ctx_row,
                                    g1, b1, alpha))
            x = outs[0]
            if update_ctx:
                h = outs[1]

        rwt = router_w[i].T
        wg = exp_w_gate[i].astype(BF16)
        wu = exp_w_up[i].astype(BF16)
        wd = exp_w_down[i].astype(BF16)
        sg = sh_w_gate[i].astype(BF16)
        su = sh_w_up[i].astype(BF16)
        sd = sh_w_down[i].astype(BF16)
        x = _moe_layer(x, mod_i, None, rwt, router_bias[i], wg, wu, wd, sg, su, sd, g2, b2, alpha)
        if update_ctx:
            h = _moe_layer(h, mod_i, b, rwt, router_bias[i], wg, wu, wd, sg, su, sd, g2, b2, alpha)
    return x
```

```python
import functools
import math

import jax
import jax.numpy as jnp
from jax import lax
from jax.experimental import pallas as pl
from jax.experimental.pallas import tpu as pltpu
from jax.experimental.pallas import tpu_sc as plsc

F32 = jnp.float32
BF16 = jnp.bfloat16
I32 = jnp.int32

N_KV_HEADS = 4
WINDOW = 128
GRID_W = 64
ROPE_BASE = 10000.0
F_GROUPS = 4
TOP_K = 8
N_EXPERT_GROUPS = 8
TOPK_GROUPS = 4
ROUTED_SCALE = 2.5
N_MIXERS = 2
LN_EPS = 1e-5
MOD_EPS = 1e-6

LANES = 128
VMEM_LIMIT_BYTES = 52 * 1024 * 1024
ROW_TILE = 512
QKV_ROW_TILE = 256
Q_TILE = 128
MOE_ROW_TILE = 512
DISPATCH_CHUNK = 128
COMBINE_CHUNK = 64
N_STREAMS = 2
SC_CORES = 2
SC_SUBCORES = 16
NEG_BIG = -1e30


def _cparams(n_axes):
    return pltpu.CompilerParams(dimension_semantics=("arbitrary",) * n_axes,
                                vmem_limit_bytes=VMEM_LIMIT_BYTES)


def _dot(a, b):
    return jnp.dot(a, b, preferred_element_type=F32)


def _dot_nt(a, b):
    return lax.dot_general(a, b, (((1,), (1,)), ((), ())), preferred_element_type=F32)


def _split_bf16(a):
    hi = a.astype(BF16)
    lo = (a - hi.astype(F32)).astype(BF16)
    return hi, lo


def _normalize(x, eps):
    mu = jnp.mean(x, axis=-1, keepdims=True)
    xc = x - mu
    var = jnp.mean(xc * xc, axis=-1, keepdims=True)
    return xc * lax.rsqrt(var + eps)


def _modulate(x, shift, scale):
    return _normalize(x, MOD_EPS) * (1.0 + scale) + shift


def _silu(x):
    return x * jax.nn.sigmoid(x)


def _pack_halves(x):
    n = x.shape[1] // 2
    r = x.astype(BF16).astype(F32)
    hi = pltpu.bitcast(r[:, :n], I32)
    lo = pltpu.bitcast(r[:, n:], I32)
    return hi | lax.shift_right_logical(lo, 16)


def _unpack_halves(w):
    a = pltpu.bitcast(w & jnp.int32(-65536), F32).astype(BF16)
    b = pltpu.bitcast(lax.shift_left(w, 16), F32).astype(BF16)
    return a, b


def _ada_kernel(c_ref, w_ref, b_ref, o_ref):
    s = _silu(c_ref[...])
    sh, sl = _split_bf16(s)
    wh, wl = _split_bf16(w_ref[...])
    o_ref[...] = _dot(sh, wh) + _dot(sl, wh) + _dot(sh, wl) + b_ref[...]


def _ada_all(cvec, w_ada, b_ada):
    depth, d, n = w_ada.shape
    mp = cvec.shape[0]
    tn = 1536
    assert n % tn == 0
    return pl.pallas_call(
        _ada_kernel,
        grid=(depth, n // tn),
        in_specs=[
            pl.BlockSpec((mp, d), lambda i, j: (0, 0)),
            pl.BlockSpec((None, d, tn), lambda i, j: (i, 0, j)),
            pl.BlockSpec((None, 1, tn), lambda i, j: (i, 0, j)),
        ],
        out_specs=pl.BlockSpec((None, mp, tn), lambda i, j: (i, 0, j)),
        out_shape=jax.ShapeDtypeStruct((depth, mp, n), F32),
        compiler_params=_cparams(2),
        name="ada_mod",
    )(cvec, w_ada, b_ada.reshape(depth, 1, n))


def _mod_spec(j, d, ctx_row, batch_axis):
    if ctx_row is None:
        return pl.BlockSpec((None, None, 1, d), lambda *g: (g[batch_axis], j, 0, 0))
    return pl.BlockSpec((None, None, 1, d), lambda *g: (ctx_row, j, 0, 0))


def _row_spec(d):
    return pl.BlockSpec((1, d), lambda *g: (0, 0))


def _qkv_kernel(x_ref, sh_ref, sc_ref, w_ref, cos_ref, sin_ref, q_ref, k_ref, v_ref, *, qw, kvw):
    u = _modulate(x_ref[...], sh_ref[...], sc_ref[...]).astype(BF16)
    r = _dot(u, w_ref[...])
    qk = r[:, :qw + kvw] * cos_ref[...] + r[:, qw + 2 * kvw:] * sin_ref[...]
    q_ref[...] = qk[:, :qw].astype(BF16)
    k_ref[...] = qk[:, qw:].astype(BF16)
    v_ref[...] = r[:, qw + kvw:qw + 2 * kvw].astype(BF16)


def _qkv_proj(x, mod_i, ctx_row, w_all, cos_t, sin_t, qw, kvw):
    b, l, d = x.shape
    tm = min(QKV_ROW_TILE, l)
    assert l % tm == 0
    n_all = w_all.shape[1]
    return pl.pallas_call(
        functools.partial(_qkv_kernel, qw=qw, kvw=kvw),
        grid=(l // tm, b),
        in_specs=[
            pl.BlockSpec((None, tm, d), lambda t, bb: (bb, t, 0)),
            _mod_spec(0, d, ctx_row, 1),
            _mod_spec(1, d, ctx_row, 1),
            pl.BlockSpec((d, n_all), lambda t, bb: (0, 0)),
            pl.BlockSpec((tm, qw + kvw), lambda t, bb: (t, 0)),
            pl.BlockSpec((tm, qw + kvw), lambda t, bb: (t, 0)),
        ],
        out_specs=[
            pl.BlockSpec((None, tm, qw), lambda t, bb: (bb, t, 0)),
            pl.BlockSpec((None, tm, kvw), lambda t, bb: (bb, t, 0)),
            pl.BlockSpec((None, tm, kvw), lambda t, bb: (bb, t, 0)),
        ],
        out_shape=[
            jax.ShapeDtypeStruct((b, l, qw), BF16),
            jax.ShapeDtypeStruct((b, l, kvw), BF16),
            jax.ShapeDtypeStruct((b, l, kvw), BF16),
        ],
        compiler_params=_cparams(2),
        name="qkv_rope",
    )(x, mod_i, mod_i, w_all, cos_t, sin_t)


def _attn_kernel(sink_ref, q_ref, *refs, tq, seq, n_kv, group, hd, has_window):
    if has_window:
        k_ref, v_ref, kc_ref, vc_ref, o_ref = refs
    else:
        kc_ref, vc_ref, o_ref = refs
    q = q_ref[...]
    kc = kc_ref[...]
    vc = vc_ref[...]
    if has_window:
        span = tq + 2 * WINDOW
        q0 = pl.program_id(1) * tq
        start = pl.multiple_of(jnp.clip(q0 - WINDOW, 0, seq - span), LANES)
        kw = k_ref[pl.ds(start, span), :]
        vw = v_ref[pl.ds(start, span), :]
        qpos = q0 + lax.broadcasted_iota(jnp.int32, (tq, span), 0)
        kpos = start + lax.broadcasted_iota(jnp.int32, (tq, span), 1)
        mask1 = jnp.abs(qpos - kpos) <= WINDOW
        mask = jnp.concatenate([mask1] * group, axis=0)
    for h in range(n_kv):
        heads = [h * group + g for g in range(group)]
        qh = jnp.concatenate([q[:, j * hd:(j + 1) * hd] for j in heads], axis=0)
        sink = jnp.concatenate([jnp.full((tq, 1), sink_ref[j], F32) for j in heads], axis=0)
        sc = _dot_nt(qh, kc[:, h * hd:(h + 1) * hd])
        m = jnp.maximum(jnp.max(sc, axis=-1, keepdims=True), sink)
        if has_window:
            sw = jnp.where(mask, _dot_nt(qh, kw[:, h * hd:(h + 1) * hd]), NEG_BIG)
            m = jnp.maximum(m, jnp.max(sw, axis=-1, keepdims=True))
        pc = jnp.exp(sc - m)
        den = jnp.sum(pc, axis=-1, keepdims=True) + jnp.exp(sink - m)
        o = _dot(pc.astype(BF16), vc[:, h * hd:(h + 1) * hd])
        if has_window:
            pw = jnp.exp(sw - m)
            den = den + jnp.sum(pw, axis=-1, keepdims=True)
            o = o + _dot(pw.astype(BF16), vw[:, h * hd:(h + 1) * hd])
        o = o / den
        for g, j in enumerate(heads):
            o_ref[:, j * hd:(j + 1) * hd] = o[g * tq:(g + 1) * tq, :].astype(o_ref.dtype)


def _attention(q, k, v, kc, vc, sinks, has_window):
    b, l, qw = q.shape
    c, kvw = kc.shape[1], kc.shape[2]
    hd = kvw // N_KV_HEADS
    group = qw // kvw
    tq = Q_TILE if has_window else l
    assert l % tq == 0
    if has_window:
        assert l >= tq + 2 * WINDOW
    kern = functools.partial(_attn_kernel, tq=tq, seq=l, n_kv=N_KV_HEADS, group=group, hd=hd,
                             has_window=has_window)
    in_specs = [pl.BlockSpec(memory_space=pltpu.SMEM),
                pl.BlockSpec((None, tq, qw), lambda bb, t: (bb, t, 0))]
    args = [sinks, q]
    if has_window:
        in_specs += [pl.BlockSpec((None, l, kvw), lambda bb, t: (bb, 0, 0)),
                     pl.BlockSpec((None, l, kvw), lambda bb, t: (bb, 0, 0))]
        args += [k, v]
    in_specs += [pl.BlockSpec((None, c, kvw), lambda bb, t: (bb, 0, 0)),
                 pl.BlockSpec((None, c, kvw), lambda bb, t: (bb, 0, 0))]
    args += [kc, vc]
    return pl.pallas_call(
        kern,
        grid=(b, l // tq),
        in_specs=in_specs,
        out_specs=pl.BlockSpec((None, tq, qw), lambda bb, t: (bb, t, 0)),
        out_shape=jax.ShapeDtypeStruct((b, l, qw), BF16),
        compiler_params=_cparams(2),
        name="win_attn" if has_window else "ctx_attn",
    )(*args)


def _proj_post_kernel(a_ref, w_ref, x_ref, gate_ref, g_ref, b_ref, o_ref, *, alpha):
    y = _dot(a_ref[...], w_ref[...])
    z = alpha * x_ref[...] + gate_ref[...] * y
    o_ref[...] = _normalize(z, LN_EPS) * g_ref[...] + b_ref[...]


def _proj_post(a, w, x, mod_i, gate_j, ctx_row, ln_g, ln_b, alpha):
    b, l, d = x.shape
    ka = a.shape[2]
    tm = min(ROW_TILE, l)
    assert l % tm == 0
    return pl.pallas_call(
        functools.partial(_proj_post_kernel, alpha=alpha),
        grid=(b, l // tm),
        in_specs=[
            pl.BlockSpec((None, tm, ka), lambda bb, t: (bb, t, 0)),
            pl.BlockSpec((ka, d), lambda bb, t: (0, 0)),
            pl.BlockSpec((None, tm, d), lambda bb, t: (bb, t, 0)),
            _mod_spec(gate_j, d, ctx_row, 0),
            _row_spec(d),
            _row_spec(d),
        ],
        out_specs=pl.BlockSpec((None, tm, d), lambda bb, t: (bb, t, 0)),
        out_shape=jax.ShapeDtypeStruct((b, l, d), F32),
        compiler_params=_cparams(2),
        name="proj_post",
    )(a, w, x, mod_i, ln_g, ln_b)


def _fnet_a_kernel(x_ref, sh_ref, sc_ref, cs_ref, o_ref, *, d, gc):
    u = _modulate(x_ref[...], sh_ref[...], sc_ref[...]).astype(BF16)
    cs = cs_ref[...]
    for g in range(d // gc):
        r = _dot(u[:, g * gc:(g + 1) * gc], cs)
        o_ref[:, g * gc:(g + 1) * gc] = r[:, :gc].astype(BF16)
        o_ref[:, d + g * gc:d + (g + 1) * gc] = r[:, gc:].astype(BF16)


def _fnet_a(x, mod_i, ctx_row, cs):
    b, l, d = x.shape
    gc = d // F_GROUPS
    tm = min(ROW_TILE, l)
    assert l % tm == 0
    return pl.pallas_call(
        functools.partial(_fnet_a_kernel, d=d, gc=gc),
        grid=(b, l // tm),
        in_specs=[
            pl.BlockSpec((None, tm, d), lambda bb, t: (bb, t, 0)),
            _mod_spec(0, d, ctx_row, 0),
            _mod_spec(1, d, ctx_row, 0),
            pl.BlockSpec((gc, 2 * gc), lambda bb, t: (0, 0)),
        ],
        out_specs=pl.BlockSpec((None, tm, 2 * d), lambda bb, t: (bb, t, 0)),
        out_shape=jax.ShapeDtypeStruct((b, l, 2 * d), BF16),
        compiler_params=_cparams(2),
        name="fnet_chan_dft",
    )(x, mod_i, mod_i, cs)


def _fnet_b_kernel(cl_ref, sl_ref, ab_ref, wf_ref, x_ref, gate_ref, g_ref, b_ref, o_ref, *, d, alpha):
    f = _dot(cl_ref[...], ab_ref[:, :d]) + _dot(sl_ref[...], ab_ref[:, d:])
    y = _dot(f.astype(BF16), wf_ref[...])
    z = alpha * x_ref[...] + gate_ref[...] * y
    o_ref[...] = _normalize(z, LN_EPS) * g_ref[...] + b_ref[...]


def _fnet_b(cl, sl, ab, wf, x, mod_i, gate_j, ctx_row, ln_g, ln_b, alpha):
    b, l, d = x.shape
    tm = min(ROW_TILE, l)
    assert l % tm == 0
    return pl.pallas_call(
        functools.partial(_fnet_b_kernel, d=d, alpha=alpha),
        grid=(b, l // tm),
        in_specs=[
            pl.BlockSpec((tm, l), lambda bb, t: (t, 0)),
            pl.BlockSpec((tm, l), lambda bb, t: (t, 0)),
            pl.BlockSpec((None, l, 2 * d), lambda bb, t: (bb, 0, 0)),
            pl.BlockSpec((d, d), lambda bb, t: (0, 0)),
            pl.BlockSpec((None, tm, d), lambda bb, t: (bb, t, 0)),
            _mod_spec(gate_j, d, ctx_row, 0),
            _row_spec(d),
            _row_spec(d),
        ],
        out_specs=pl.BlockSpec((None, tm, d), lambda bb, t: (bb, t, 0)),
        out_shape=jax.ShapeDtypeStruct((b, l, d), F32),
        compiler_params=_cparams(2),
        name="fnet_seq_dft",
    )(cl, sl, ab, wf, x, mod_i, ln_g, ln_b)


def _dft_tables(n, scale):
    j = jnp.arange(n, dtype=jnp.int32)
    ang = ((j[:, None] * j[None, :]) % n).astype(F32) * (2.0 * math.pi / n)
    return jnp.cos(ang) * scale, jnp.sin(ang) * scale


def _router_kernel(x_ref, sh_ref, sc_ref, rwt_ref, bias_ref, tri_ref, u_ref, te_ref, w_ref, rk_ref, cnt_ref,
                   *, n_groups, topk_groups, top_k):
    u = _modulate(x_ref[...], sh_ref[...], sc_ref[...])
    uh, ul = _split_bf16(u)
    u_ref[...] = _pack_halves(u)
    wh, wl = _split_bf16(rwt_ref[...])
    logits = _dot_nt(wh, uh) + _dot_nt(wl, uh) + _dot_nt(wh, ul)
    e, tm = logits.shape
    pg = e // n_groups
    neg = -jnp.inf
    scores = jax.nn.sigmoid(logits)
    sel = scores + bias_ref[...]
    sub = lax.broadcasted_iota(jnp.int32, (pg, tm), 0)
    groups = [sel[g * pg:(g + 1) * pg, :] for g in range(n_groups)]
    sgroups = [scores[g * pg:(g + 1) * pg, :] for g in range(n_groups)]

    gs_rows = []
    for s_g in groups:
        m1 = jnp.max(s_g, axis=0, keepdims=True)
        first = jnp.min(jnp.where(s_g == m1, sub, pg), axis=0, keepdims=True)
        m2 = jnp.max(jnp.where(sub == first, neg, s_g), axis=0, keepdims=True)
        gs_rows.append(m1 + m2)
    gs = jnp.concatenate(gs_rows, axis=0)
    gidx = lax.broadcasted_iota(jnp.int32, (n_groups, tm), 0)
    ok = jnp.zeros((n_groups, tm), F32)
    for _ in range(topk_groups):
        m = jnp.max(gs, axis=0, keepdims=True)
        first = jnp.min(jnp.where(gs == m, gidx, n_groups), axis=0, keepdims=True)
        hit = gidx == first
        ok = jnp.where(hit, 1.0, ok)
        gs = jnp.where(hit, neg, gs)

    cur = [jnp.where(ok[g:g + 1, :] > 0.0, groups[g], neg) for g in range(n_groups)]
    eidx = [sub + g * pg for g in range(n_groups)]
    chosen = [jnp.zeros((pg, tm), F32) for _ in range(n_groups)]
    e_rows, s_rows = [], []
    for _ in range(top_k):
        m = jnp.max(functools.reduce(jnp.maximum, cur), axis=0, keepdims=True)
        cand = functools.reduce(jnp.minimum, [jnp.where(cur[g] == m, eidx[g], e) for g in range(n_groups)])
        first = jnp.min(cand, axis=0, keepdims=True)
        picked = jnp.zeros((pg, tm), F32)
        for g in range(n_groups):
            hit = eidx[g] == first
            picked = picked + jnp.where(hit, sgroups[g], 0.0)
            chosen[g] = jnp.where(hit, 1.0, chosen[g])
            cur[g] = jnp.where(hit, neg, cur[g])
        e_rows.append(first)
        s_rows.append(jnp.sum(picked, axis=0, keepdims=True))
    w = jnp.concatenate(s_rows, axis=0)
    te_ref[...] = jnp.concatenate(e_rows, axis=0)
    w_ref[...] = w / jnp.sum(w, axis=0, keepdims=True) * ROUTED_SCALE

    sel_mask = jnp.concatenate(chosen, axis=0).astype(BF16)
    before = _dot(sel_mask, tri_ref[...])
    rk_rows = []
    for k in range(top_k):
        acc = jnp.zeros((pg, tm), F32)
        for g in range(n_groups):
            acc = acc + jnp.where(eidx[g] == e_rows[k], before[g * pg:(g + 1) * pg, :], 0.0)
        rk_rows.append(jnp.sum(acc, axis=0, keepdims=True))
    rk_ref[...] = jnp.concatenate(rk_rows, axis=0).astype(jnp.int32)
    cnt_ref[...] = _dot_nt(jnp.ones((8, tm), BF16), sel_mask)


def _router(x, mod_i, ctx_row, rwt, bias):
    b, l, d = x.shape
    e = rwt.shape[0]
    tm = min(ROW_TILE, l)
    assert l % tm == 0 and e % N_EXPERT_GROUPS == 0
    nt = l // tm
    t = b * l
    row = lax.broadcasted_iota(jnp.int32, (tm, tm), 0)
    col = lax.broadcasted_iota(jnp.int32, (tm, tm), 1)
    tri = (row < col).astype(BF16)
    kern = functools.partial(_router_kernel, n_groups=N_EXPERT_GROUPS, topk_groups=TOPK_GROUPS, top_k=TOP_K)
    tok_spec = pl.BlockSpec((TOP_K, tm), lambda bb, tt: (0, bb * nt + tt))
    return pl.pallas_call(
        kern,
        grid=(b, nt),
        in_specs=[
            pl.BlockSpec((None, tm, d), lambda bb, tt: (bb, tt, 0)),
            _mod_spec(3, d, ctx_row, 0),
            _mod_spec(4, d, ctx_row, 0),
            pl.BlockSpec((e, d), lambda bb, tt: (0, 0)),
            pl.BlockSpec((e, 1), lambda bb, tt: (0, 0)),
            pl.BlockSpec((tm, tm), lambda bb, tt: (0, 0)),
        ],
        out_specs=[
            pl.BlockSpec((None, tm, d // 2), lambda bb, tt: (bb, tt, 0)),
            tok_spec, tok_spec, tok_spec,
            pl.BlockSpec((None, 8, e), lambda bb, tt: (bb * nt + tt, 0, 0)),
        ],
        out_shape=[
            jax.ShapeDtypeStruct((b, l, d // 2), I32),
            jax.ShapeDtypeStruct((TOP_K, t), jnp.int32),
            jax.ShapeDtypeStruct((TOP_K, t), F32),
            jax.ShapeDtypeStruct((TOP_K, t), jnp.int32),
            jax.ShapeDtypeStruct((b * nt, 8, e), F32),
        ],
        compiler_params=_cparams(2),
        name="moe_router",
    )(x, mod_i, mod_i, rwt, bias.reshape(e, 1).astype(F32), tri)


def _expert_layout(cnt, tmb, n_blocks):
    cnt_i = cnt[:, 0, :].astype(jnp.int32)
    e = cnt_i.shape[1]
    counts = cnt_i.sum(axis=0)
    padded = (counts + tmb - 1) // tmb * tmb
    pend = jnp.cumsum(padded)
    base = (pend - padded)[None, :] + jnp.cumsum(cnt_i, axis=0) - cnt_i
    n_used = (pend[-1] // tmb).astype(jnp.int32)
    blk = jnp.arange(n_blocks, dtype=jnp.int32)
    block_e = jnp.sum((blk[:, None] * tmb >= pend[None, :]).astype(jnp.int32), axis=1)
    last_e = jnp.sum((jnp.maximum(n_used - 1, 0) * tmb >= pend).astype(jnp.int32))
    block_e = jnp.clip(jnp.where(blk < n_used, block_e, last_e), 0, e - 1).astype(jnp.int32)
    seg_end = (pend - padded + counts)[block_e]
    block_valid = jnp.clip(seg_end - blk * tmb, 0, tmb).astype(jnp.int32)
    return base.reshape(-1).astype(jnp.int32), block_e, block_valid, n_used.reshape(1)


def _pos_kernel(base_ref, te_ref, rk_ref, pos_ref, *, n_experts):
    i = pl.program_id(0)
    te = te_ref[...]
    pos = rk_ref[...]
    for e in range(n_experts):
        pos = pos + jnp.where(te == e, base_ref[i * n_experts + e], 0)
    pos_ref[...] = pos


def _positions(base, top_e, rank, n_experts, tm):
    k, t = top_e.shape
    spec = pl.BlockSpec((k, tm), lambda i, base_ref: (0, i))
    return pl.pallas_call(
        functools.partial(_pos_kernel, n_experts=n_experts),
        grid_spec=pltpu.PrefetchScalarGridSpec(
            num_scalar_prefetch=1, grid=(t // tm,), in_specs=[spec, spec], out_specs=spec),
        out_shape=jax.ShapeDtypeStruct((k, t), jnp.int32),
        compiler_params=_cparams(1),
        name="moe_positions",
    )(base, top_e, rank)


def _sc_dispatch(rows, pos3, cap):
    t, w = rows.shape
    n_chunks, top_k, n = pos3.shape
    n_workers = SC_CORES * SC_SUBCORES
    assert n_chunks * n == t and n <= LANES
    per_worker = -(-n_chunks // n_workers)
    mesh = plsc.VectorSubcoreMesh(core_axis_name="core", subcore_axis_name="subcore",
                                  num_cores=SC_CORES, num_subcores=SC_SUBCORES)

    def body(rows_hbm, pos_hbm, out_hbm, idx_v, rows_v, sem):
        wid = lax.axis_index("subcore") * SC_CORES + lax.axis_index("core")

        @pl.loop(0, per_worker)
        def _(j):
            c = wid * per_worker + j

            @pl.when(c < n_chunks)
            def _():
                pltpu.sync_copy(pos_hbm.at[c], idx_v)
                pltpu.sync_copy(rows_hbm.at[pl.ds(c * n, n)], rows_v)
                copies = [pltpu.async_copy(rows_v, out_hbm.at[idx_v.at[k]], sem) for k in range(top_k)]
                for cp in copies:
                    cp.wait()

    return pl.kernel(
        body,
        out_type=jax.ShapeDtypeStruct((cap, w), I32),
        mesh=mesh,
        scratch_types=[pltpu.VMEM((top_k, n), I32), pltpu.VMEM((n, w), I32), pltpu.SemaphoreType.DMA],
        name="moe_dispatch_sc",
    )(rows, pos3)


def _sc_combine_gather(rows, pos3):
    cap, w = rows.shape
    n_chunks, top_k, n = pos3.shape
    t = n_chunks * n
    n_workers = SC_CORES * SC_SUBCORES
    assert n <= LANES
    per_worker = -(-n_chunks // n_workers)
    mesh = plsc.VectorSubcoreMesh(core_axis_name="core", subcore_axis_name="subcore",
                                  num_cores=SC_CORES, num_subcores=SC_SUBCORES)

    def body(rows_hbm, pos_hbm, out_hbm, idx_v, buf0, buf1, gsem0, gsem1, wsem0, wsem1):
        wid = lax.axis_index("subcore") * SC_CORES + lax.axis_index("core")
        bufs, gsems, wsems = (buf0, buf1), (gsem0, gsem1), (wsem0, wsem1)

        @pl.loop(0, per_worker)
        def _(j):
            c = wid * per_worker + j

            @pl.when(c < n_chunks)
            def _():
                pltpu.sync_copy(pos_hbm.at[c], idx_v)
                gathers = [None] * top_k
                writes = [None] * top_k
                gathers[0] = pltpu.async_copy(rows_hbm.at[idx_v.at[0]], bufs[0], gsems[0])
                for k in range(top_k):
                    if k + 1 < top_k:
                        if k >= 1:
                            writes[k - 1].wait()
                        gathers[k + 1] = pltpu.async_copy(rows_hbm.at[idx_v.at[k + 1]], bufs[(k + 1) % 2],
                                                          gsems[(k + 1) % 2])
                    gathers[k].wait()
                    writes[k] = pltpu.async_copy(bufs[k % 2], out_hbm.at[k, pl.ds(c * n, n)], wsems[k % 2])
                writes[top_k - 2].wait()
                writes[top_k - 1].wait()

    return pl.kernel(
        body,
        out_type=jax.ShapeDtypeStruct((top_k, t, w), I32),
        mesh=mesh,
        scratch_types=[pltpu.VMEM((top_k, n), I32), pltpu.VMEM((n, w), I32), pltpu.VMEM((n, w), I32),
                       pltpu.SemaphoreType.DMA, pltpu.SemaphoreType.DMA,
                       pltpu.SemaphoreType.DMA, pltpu.SemaphoreType.DMA],
        name="moe_combine_sc",
    )(rows, pos3)


def _gmm_kernel(be_ref, bv_ref, nu_ref, x_ref, wg_ref, wu_ref, wd_ref, o_ref):
    i = pl.program_id(0)

    @pl.when(i < nu_ref[0])
    def _():
        words = x_ref[...]
        tmb, half = words.shape
        live = lax.broadcasted_iota(I32, (tmb, half), 0) < bv_ref[i]
        xa, xb = _unpack_halves(jnp.where(live, words, 0))
        g = _dot(xa, wg_ref[:half, :]) + _dot(xb, wg_ref[half:, :])
        u = _dot(xa, wu_ref[:half, :]) + _dot(xb, wu_ref[half:, :])
        o_ref[...] = _pack_halves(_dot((_silu(g) * u).astype(BF16), wd_ref[...]))


def _grouped_experts(x_sorted, block_e, block_valid, n_used, wg, wu, wd, tmb):
    cap, half = x_sorted.shape
    d = 2 * half
    ff = wg.shape[2]
    n_blocks = cap // tmb

    def row_map(i, be, bv, nu):
        return (jnp.minimum(i, nu[0] - 1), 0)

    def w_map(i, be, bv, nu):
        return (be[i], 0, 0)

    return pl.pallas_call(
        _gmm_kernel,
        grid_spec=pltpu.PrefetchScalarGridSpec(
            num_scalar_prefetch=3,
            grid=(n_blocks,),
            in_specs=[
                pl.BlockSpec((tmb, half), row_map),
                pl.BlockSpec((None, d, ff), w_map),
                pl.BlockSpec((None, d, ff), w_map),
                pl.BlockSpec((None, ff, d), w_map),
            ],
            out_specs=pl.BlockSpec((tmb, half), row_map),
        ),
        out_shape=jax.ShapeDtypeStruct((cap, half), I32),
        compiler_params=_cparams(1),
        name="moe_experts",
    )(block_e, block_valid, n_used, x_sorted, wg, wu, wd)


def _moe_out_kernel(u_ref, w_ref, *refs, alpha, top_k):
    y_refs = refs[:top_k]
    sg_ref, su_ref, sd_ref, x_ref, gate_ref, g_ref, b_ref, o_ref = refs[top_k:]
    ua, ub = _unpack_halves(u_ref[...])
    half = ua.shape[1]
    g = _dot(ua, sg_ref[:half, :]) + _dot(ub, sg_ref[half:, :])
    s = _dot(ua, su_ref[:half, :]) + _dot(ub, su_ref[half:, :])
    y = _dot((_silu(g) * s).astype(BF16), sd_ref[...])
    w = w_ref[...]
    ya = y[:, :half]
    yb = y[:, half:]
    for k in range(top_k):
        words = y_refs[k][...]
        wk = w[:, k:k + 1]
        ya = ya + wk * pltpu.bitcast(words & jnp.int32(-65536), F32)
        yb = yb + wk * pltpu.bitcast(lax.shift_left(words, 16), F32)
    x = x_ref[...]
    gate = gate_ref[...]
    za = alpha * x[:, :half] + gate[:, :half] * ya
    zb = alpha * x[:, half:] + gate[:, half:] * yb
    z = jnp.concatenate([za, zb], axis=1)
    o_ref[...] = _normalize(z, LN_EPS) * g_ref[...] + b_ref[...]


def _moe_out(u2, wts, ys, sg, su, sd, x, mod_i, ctx_row, ln_g, ln_b, alpha):
    b, l, d = x.shape
    ff = sg.shape[1]
    top_k = ys.shape[0]
    tm = min(ROW_TILE, l)
    assert l % tm == 0
    tok = pl.BlockSpec((None, tm, d), lambda bb, t: (bb, t, 0))
    words = pl.BlockSpec((None, tm, d // 2), lambda bb, t: (bb, t, 0))
    picks = [pl.BlockSpec((None, None, tm, d // 2), functools.partial(lambda bb, t, k: (k, bb, t, 0), k=k))
             for k in range(top_k)]
    return pl.pallas_call(
        functools.partial(_moe_out_kernel, alpha=alpha, top_k=top_k),
        grid=(b, l // tm),
        in_specs=[words, pl.BlockSpec((None, tm, top_k), lambda bb, t: (bb, t, 0))] + picks + [
            pl.BlockSpec((d, ff), lambda bb, t: (0, 0)),
            pl.BlockSpec((d, ff), lambda bb, t: (0, 0)),
            pl.BlockSpec((ff, d), lambda bb, t: (0, 0)),
            tok,
            _mod_spec(5, d, ctx_row, 0),
            _row_spec(d),
            _row_spec(d),
        ],
        out_specs=tok,
        out_shape=jax.ShapeDtypeStruct((b, l, d), F32),
        compiler_params=_cparams(2),
        name="moe_shared_post",
    )(u2, wts, *([ys] * top_k), sg, su, sd, x, mod_i, ln_g, ln_b)


def _moe_layer(x, mod_i, ctx_row, rwt, rbias, wg, wu, wd, sg, su, sd, ln_g, ln_b, alpha):
    b, l, d = x.shape
    t = b * l
    e = rwt.shape[0]
    tmb = MOE_ROW_TILE
    n_blocks = -(-(t * TOP_K) // tmb) + e
    u2, top_e, wts, rank, cnt = _router(x, mod_i, ctx_row, rwt, rbias)
    base, block_e, block_valid, n_used = _expert_layout(cnt, tmb, n_blocks)
    pos = _positions(base, top_e, rank, e, min(ROW_TILE, l))
    pos_d = pos.reshape(TOP_K, t // DISPATCH_CHUNK, DISPATCH_CHUNK).transpose(1, 0, 2)
    pos_c = pos.reshape(TOP_K, t // COMBINE_CHUNK, COMBINE_CHUNK).transpose(1, 0, 2)
    x_sorted = _sc_dispatch(u2.reshape(t, d // 2), pos_d, n_blocks * tmb)
    y_sorted = _grouped_experts(x_sorted, block_e, block_valid, n_used, wg, wu, wd, tmb)
    ys = _sc_combine_gather(y_sorted, pos_c).reshape(TOP_K, b, l, d // 2)
    return _moe_out(u2, wts.T.reshape(b, l, TOP_K), ys, sg, su, sd, x, mod_i, ctx_row, ln_g, ln_b, alpha)


def _rope_tables(l, hd, n_q_heads, n_k_heads, rope):
    axis_rot = hd // 2
    qscale = hd ** -0.5
    if rope:
        t = jnp.arange(l, dtype=jnp.int32)
        r = (t // GRID_W).astype(F32)
        col = (t % GRID_W).astype(F32)
        inv = ROPE_BASE ** (-jnp.arange(0, axis_rot, 2, dtype=F32) / axis_rot)
        ar = r[:, None] * inv
        ac = col[:, None] * inv
        ang = jnp.concatenate([ar, ar, ac, ac], axis=-1)
        cos, sin = jnp.cos(ang), jnp.sin(ang)
    else:
        cos, sin = jnp.ones((l, hd), F32), jnp.zeros((l, hd), F32)
    cos_t = jnp.concatenate([jnp.tile(cos, (1, n_q_heads)) * qscale, jnp.tile(cos, (1, n_k_heads))], axis=1)
    sin_t = jnp.concatenate([jnp.tile(sin, (1, n_q_heads)) * qscale, jnp.tile(sin, (1, n_k_heads))], axis=1)
    return cos_t, sin_t


def _rot_columns(w, hd):
    d, n = w.shape
    q = hd // 4
    w4 = w.reshape(d, n // (2 * q), 2, q)
    return jnp.stack([-w4[:, :, 1, :], w4[:, :, 0, :]], axis=2).reshape(d, n)


def kernel(x, c, ctx, c_ctx, w_ada, b_ada, ln_g, ln_b, attn_w_qkv, attn_w_o, attn_sinks, fnet_w,
           router_w, router_bias, exp_w_gate, exp_w_up, exp_w_down, sh_w_gate, sh_w_up, sh_w_down):
    b, l, d = x.shape
    cl = ctx.shape[1]
    depth = w_ada.shape[0]
    n_heads = attn_sinks.shape[1]
    hd = d // n_heads
    qw = n_heads * hd
    kvw = N_KV_HEADS * hd
    alpha = (2.0 * depth) ** 0.25
    gc = d // F_GROUPS

    mp = -(-(b + 1) // 16) * 16
    cvec = jnp.concatenate([c, c_ctx[None, :], jnp.zeros((mp - b - 1, d), F32)], axis=0)
    mod = _ada_all(cvec, w_ada, b_ada).reshape(depth, mp, 6, 1, d)

    def run_stream(x, h, mod):
        b = x.shape[0]
        for i in range(depth):
            kind = i % N_MIXERS
            j = i // N_MIXERS
            update_ctx = any((m % N_MIXERS) == 0 for m in range(i + 1, depth))
            mod_i = mod[i]
            g1 = ln_g[i, 0].reshape(1, d)
            b1 = ln_b[i, 0].reshape(1, d)
            g2 = ln_g[i, 1].reshape(1, d)
            b2 = ln_b[i, 1].reshape(1, d)

            if kind == 0:
                w_all, w_o = attn_w[j]
                q, k, v = _qkv_proj(x, mod_i, None, w_all, cos_l, sin_l, qw, kvw)
                q_c, k_c, v_c = _qkv_proj(h, mod_i, b, w_all, cos_c, sin_c, qw, kvw)
                o = _attention(q, k, v, k_c, v_c, attn_sinks[j], True)
                x = _proj_post(o, w_o, x, mod_i, 2, None, g1, b1, alpha)
                if update_ctx:
                    o_c = _attention(q_c, None, None, k_c, v_c, attn_sinks[j], False)
                    h = _proj_post(o_c, w_o, h, mod_i, 2, b, g1, b1, alpha)
            else:
                streams = [(x, None)] + ([(h, b)] if update_ctx else [])
                outs = []
                for s, ctx_row in streams:
                    c_l, s_l = dft_seq[s.shape[1]]
                    ab = _fnet_a(s, mod_i, ctx_row, cs)
                    outs.append(_fnet_b(c_l, s_l, ab, fnet_wb[j], s, mod_i, 2, ctx_row, g1, b1, alpha))
                x = outs[0]
                if update_ctx:
                    h = outs[1]

            x = _moe_layer(x, mod_i, None, *moe_w[i], g2, b2, alpha)
            if update_ctx:
                h = _moe_layer(h, mod_i, b, *moe_w[i], g2, b2, alpha)
        return x

    attn_w = []
    for j in range(attn_w_qkv.shape[0]):
        w = attn_w_qkv[j]
        w_all = jnp.concatenate([w, _rot_columns(w[:, :qw + kvw], hd)], axis=1).astype(BF16)
        attn_w.append((w_all, attn_w_o[j].astype(BF16)))
    cos_l, sin_l = _rope_tables(l, hd, n_heads, N_KV_HEADS, True)
    cos_c, sin_c = _rope_tables(cl, hd, n_heads, N_KV_HEADS, False)
    fnet_wb = [fnet_w[j].astype(BF16) for j in range(fnet_w.shape[0])]
    cc, sc = _dft_tables(gc, gc ** -0.5)
    cs = jnp.concatenate([cc, sc], axis=1).astype(BF16)
    dft_seq = {}
    for ls in (l, cl):
        c_l, s_l = _dft_tables(ls, ls ** -0.5)
        dft_seq[ls] = (c_l.astype(BF16), (-s_l).astype(BF16))
    moe_w = [(router_w[i].T, router_bias[i], exp_w_gate[i].astype(BF16), exp_w_up[i].astype(BF16),
              exp_w_down[i].astype(BF16), sh_w_gate[i].astype(BF16), sh_w_up[i].astype(BF16),
              sh_w_down[i].astype(BF16)) for i in range(depth)]

    n_streams = N_STREAMS if b % N_STREAMS == 0 else 1
    bs = b // n_streams
    outs = []
    for s in range(n_streams):
        mod_s = jnp.concatenate([mod[:, s * bs:(s + 1) * bs], mod[:, b:b + 1]], axis=1)
        outs.append(run_stream(x[s * bs:(s + 1) * bs], ctx[s * bs:(s + 1) * bs], mod_s))
    return outs[0] if n_streams == 1 else jnp.concatenate(outs, axis=0)
```

```python
import functools
import math

import jax
import jax.numpy as jnp
from jax import lax
from jax.experimental import pallas as pl
from jax.experimental.pallas import tpu as pltpu
from jax.experimental.pallas import tpu_sc as plsc

F32 = jnp.float32
BF16 = jnp.bfloat16
I32 = jnp.int32

N_KV_HEADS = 4
WINDOW = 128
GRID_W = 64
ROPE_BASE = 10000.0
F_GROUPS = 4
TOP_K = 8
N_EXPERT_GROUPS = 8
TOPK_GROUPS = 4
ROUTED_SCALE = 2.5
N_MIXERS = 2
LN_EPS = 1e-5
MOD_EPS = 1e-6

LANES = 128
VMEM_LIMIT_BYTES = 52 * 1024 * 1024
ROW_TILE = 512
QKV_ROW_TILE = 256
Q_TILE = 128
ATTN_ROW_CHUNK = 32
ATTN_UNROLL = 16
MOE_ROW_TILE = 512
DISPATCH_CHUNK = 128
COMBINE_CHUNK = 64
N_STREAMS = 2
SC_CORES = 2
SC_SUBCORES = 16
NEG_BIG = -1e30


def _cparams(n_axes):
    return pltpu.CompilerParams(dimension_semantics=("arbitrary",) * n_axes,
                                vmem_limit_bytes=VMEM_LIMIT_BYTES)


def _dot(a, b):
    return jnp.dot(a, b, preferred_element_type=F32)


def _dot_nt(a, b):
    return lax.dot_general(a, b, (((1,), (1,)), ((), ())), preferred_element_type=F32)


def _split_bf16(a):
    hi = a.astype(BF16)
    lo = (a - hi.astype(F32)).astype(BF16)
    return hi, lo


def _normalize(x, eps):
    mu = jnp.mean(x, axis=-1, keepdims=True)
    xc = x - mu
    var = jnp.mean(xc * xc, axis=-1, keepdims=True)
    return xc * lax.rsqrt(var + eps)


def _modulate(x, shift, scale):
    return _normalize(x, MOD_EPS) * (1.0 + scale) + shift


def _silu(x):
    return x * jax.nn.sigmoid(x)


def _pack_halves(x):
    n = x.shape[1] // 2
    r = x.astype(BF16).astype(F32)
    hi = pltpu.bitcast(r[:, :n], I32)
    lo = pltpu.bitcast(r[:, n:], I32)
    return hi | lax.shift_right_logical(lo, 16)


def _unpack_halves(w):
    a = pltpu.bitcast(w & jnp.int32(-65536), F32).astype(BF16)
    b = pltpu.bitcast(lax.shift_left(w, 16), F32).astype(BF16)
    return a, b


def _ada_kernel(c_ref, w_ref, b_ref, o_ref):
    s = _silu(c_ref[...])
    sh, sl = _split_bf16(s)
    wh, wl = _split_bf16(w_ref[...])
    o_ref[...] = _dot(sh, wh) + _dot(sl, wh) + _dot(sh, wl) + b_ref[...]


def _ada_all(cvec, w_ada, b_ada):
    depth, d, n = w_ada.shape
    mp = cvec.shape[0]
    tn = 1536
    assert n % tn == 0
    return pl.pallas_call(
        _ada_kernel,
        grid=(depth, n // tn),
        in_specs=[
            pl.BlockSpec((mp, d), lambda i, j: (0, 0)),
            pl.BlockSpec((None, d, tn), lambda i, j: (i, 0, j)),
            pl.BlockSpec((None, 1, tn), lambda i, j: (i, 0, j)),
        ],
        out_specs=pl.BlockSpec((None, mp, tn), lambda i, j: (i, 0, j)),
        out_shape=jax.ShapeDtypeStruct((depth, mp, n), F32),
        compiler_params=_cparams(2),
        name="ada_mod",
    )(cvec, w_ada, b_ada.reshape(depth, 1, n))


def _mod_spec(j, d, ctx_row, batch_axis):
    if ctx_row is None:
        return pl.BlockSpec((None, None, 1, d), lambda *g: (g[batch_axis], j, 0, 0))
    return pl.BlockSpec((None, None, 1, d), lambda *g: (ctx_row, j, 0, 0))


def _row_spec(d):
    return pl.BlockSpec((1, d), lambda *g: (0, 0))


def _qkv_kernel(x_ref, sh_ref, sc_ref, w_ref, cos_ref, sin_ref, q_ref, k_ref, v_ref, *, qw, kvw):
    u = _modulate(x_ref[...], sh_ref[...], sc_ref[...]).astype(BF16)
    r = _dot(u, w_ref[...])
    qk = r[:, :qw + kvw] * cos_ref[...] + r[:, qw + 2 * kvw:] * sin_ref[...]
    q_ref[...] = qk[:, :qw].astype(BF16)
    k_ref[...] = qk[:, qw:].astype(BF16)
    v_ref[...] = r[:, qw + kvw:qw + 2 * kvw].astype(BF16)


def _qkv_proj(x, mod_i, ctx_row, w_all, cos_t, sin_t, qw, kvw):
    b, l, d = x.shape
    tm = min(QKV_ROW_TILE, l)
    assert l % tm == 0
    n_all = w_all.shape[1]
    return pl.pallas_call(
        functools.partial(_qkv_kernel, qw=qw, kvw=kvw),
        grid=(l // tm, b),
        in_specs=[
            pl.BlockSpec((None, tm, d), lambda t, bb: (bb, t, 0)),
            _mod_spec(0, d, ctx_row, 1),
            _mod_spec(1, d, ctx_row, 1),
            pl.BlockSpec((d, n_all), lambda t, bb: (0, 0)),
            pl.BlockSpec((tm, qw + kvw), lambda t, bb: (t, 0)),
            pl.BlockSpec((tm, qw + kvw), lambda t, bb: (t, 0)),
        ],
        out_specs=[
            pl.BlockSpec((None, tm, qw), lambda t, bb: (bb, t, 0)),
            pl.BlockSpec((None, tm, kvw), lambda t, bb: (bb, t, 0)),
            pl.BlockSpec((None, tm, kvw), lambda t, bb: (bb, t, 0)),
        ],
        out_shape=[
            jax.ShapeDtypeStruct((b, l, qw), BF16),
            jax.ShapeDtypeStruct((b, l, kvw), BF16),
            jax.ShapeDtypeStruct((b, l, kvw), BF16),
        ],
        compiler_params=_cparams(2),
        name="qkv_rope",
    )(x, mod_i, mod_i, w_all, cos_t, sin_t)


def _attn_kernel(sink_ref, q_ref, *refs, tq, seq, n_kv, group, hd, has_window):
    if has_window:
        k_ref, v_ref, kc_ref, vc_ref, o_ref, s_scr, p_scr, m_scr, bias_scr = refs
    else:
        kc_ref, vc_ref, o_ref, s_scr, p_scr, m_scr = refs
    q = q_ref[...]
    kc = kc_ref[...]
    vc = vc_ref[...]
    span = tq + 2 * WINDOW if has_window else 0
    rows = group * tq
    if has_window:
        q0 = pl.program_id(1) * tq
        start = pl.multiple_of(jnp.clip(q0 - WINDOW, 0, seq - span), LANES)
        kw = k_ref[pl.ds(start, span), :]
        vw = v_ref[pl.ds(start, span), :]
        qpos = q0 + lax.broadcasted_iota(jnp.int32, (tq, span), 0)
        kpos = start + lax.broadcasted_iota(jnp.int32, (tq, span), 1)
        bias_scr[...] = jnp.where(jnp.abs(qpos - kpos) <= WINDOW, 0.0, NEG_BIG)

    for h in range(n_kv):
        heads = [h * group + g for g in range(group)]
        qh = jnp.concatenate([q[:, j * hd:(j + 1) * hd] for j in heads], axis=0)
        if has_window:
            s_scr[h, :, :span] = _dot_nt(qh, kw[:, h * hd:(h + 1) * hd])
        s_scr[h, :, span:] = _dot_nt(qh, kc[:, h * hd:(h + 1) * hd])

    chunks_per_head = tq // ATTN_ROW_CHUNK
    n_chunks = rows // ATTN_ROW_CHUNK
    n_tiles = (span + kc.shape[0]) // LANES
    win_tiles = span // LANES

    def logit_tiles(h, r):
        row = pl.multiple_of(r * ATTN_ROW_CHUNK, ATTN_ROW_CHUNK)
        s = s_scr[h, pl.ds(row, ATTN_ROW_CHUNK), :]
        tiles = [s[:, i * LANES:(i + 1) * LANES] for i in range(n_tiles)]
        if has_window:
            brow = pl.multiple_of((r % chunks_per_head) * ATTN_ROW_CHUNK, ATTN_ROW_CHUNK)
            bias = bias_scr[pl.ds(brow, ATTN_ROW_CHUNK), :]
            tiles = [t + bias[:, i * LANES:(i + 1) * LANES] if i < win_tiles else t for i, t in enumerate(tiles)]
        return row, tiles

    for h in range(n_kv):
        def row_max(r, carry, h=h):
            row, tiles = logit_tiles(h, r)
            sink = sink_ref[h * group + r // chunks_per_head]
            m = jnp.max(functools.reduce(jnp.maximum, tiles), axis=-1, keepdims=True)
            m_scr[h, pl.ds(row, ATTN_ROW_CHUNK), :] = jnp.broadcast_to(jnp.maximum(m, sink),
                                                                     (ATTN_ROW_CHUNK, LANES))
            return carry

        lax.fori_loop(0, n_chunks, row_max, 0, unroll=ATTN_UNROLL)

    for h in range(n_kv):
        def probs(r, carry, h=h):
            row, tiles = logit_tiles(h, r)
            sink = sink_ref[h * group + r // chunks_per_head]
            m = m_scr[h, pl.ds(row, ATTN_ROW_CHUNK), :]
            es = [jnp.exp(t - m) for t in tiles]
            for i, e in enumerate(es):
                p_scr[h, pl.ds(row, ATTN_ROW_CHUNK), i * LANES:(i + 1) * LANES] = e.astype(BF16)
            den = jnp.sum(functools.reduce(jnp.add, es), axis=-1, keepdims=True) + jnp.exp(sink - m)
            m_scr[h, pl.ds(row, ATTN_ROW_CHUNK), :] = 1.0 / den
            return carry

        lax.fori_loop(0, n_chunks, probs, 0, unroll=ATTN_UNROLL)

    for h in range(n_kv):
        o = _dot(p_scr[h, :, span:], vc[:, h * hd:(h + 1) * hd])
        if has_window:
            o = o + _dot(p_scr[h, :, :span], vw[:, h * hd:(h + 1) * hd])
        o = o * m_scr[h, :, :hd]
        for g in range(group):
            j = h * group + g
            o_ref[:, j * hd:(j + 1) * hd] = o[g * tq:(g + 1) * tq, :].astype(o_ref.dtype)


def _attention(q, k, v, kc, vc, sinks, has_window):
    b, l, qw = q.shape
    c, kvw = kc.shape[1], kc.shape[2]
    hd = kvw // N_KV_HEADS
    group = qw // kvw
    tq = Q_TILE if has_window else l
    assert l % tq == 0
    if has_window:
        assert l >= tq + 2 * WINDOW
    kern = functools.partial(_attn_kernel, tq=tq, seq=l, n_kv=N_KV_HEADS, group=group, hd=hd,
                             has_window=has_window)
    in_specs = [pl.BlockSpec(memory_space=pltpu.SMEM),
                pl.BlockSpec((None, tq, qw), lambda bb, t: (bb, t, 0))]
    args = [sinks, q]
    if has_window:
        in_specs += [pl.BlockSpec((None, l, kvw), lambda bb, t: (bb, 0, 0)),
                     pl.BlockSpec((None, l, kvw), lambda bb, t: (bb, 0, 0))]
        args += [k, v]
    in_specs += [pl.BlockSpec((None, c, kvw), lambda bb, t: (bb, 0, 0)),
                 pl.BlockSpec((None, c, kvw), lambda bb, t: (bb, 0, 0))]
    args += [kc, vc]
    n_keys = c + (tq + 2 * WINDOW if has_window else 0)
    scratch = [pltpu.VMEM((N_KV_HEADS, group * tq, n_keys), F32),
               pltpu.VMEM((N_KV_HEADS, group * tq, n_keys), BF16),
               pltpu.VMEM((N_KV_HEADS, group * tq, LANES), F32)]
    if has_window:
        scratch.append(pltpu.VMEM((tq, tq + 2 * WINDOW), F32))
    return pl.pallas_call(
        kern,
        grid=(b, l // tq),
        in_specs=in_specs,
        out_specs=pl.BlockSpec((None, tq, qw), lambda bb, t: (bb, t, 0)),
        out_shape=jax.ShapeDtypeStruct((b, l, qw), BF16),
        scratch_shapes=scratch,
        compiler_params=_cparams(2),
        name="win_attn" if has_window else "ctx_attn",
    )(*args)


def _proj_post_kernel(a_ref, w_ref, x_ref, gate_ref, g_ref, b_ref, o_ref, *, alpha):
    y = _dot(a_ref[...], w_ref[...])
    z = alpha * x_ref[...] + gate_ref[...] * y
    o_ref[...] = _normalize(z, LN_EPS) * g_ref[...] + b_ref[...]


def _proj_post(a, w, x, mod_i, gate_j, ctx_row, ln_g, ln_b, alpha):
    b, l, d = x.shape
    ka = a.shape[2]
    tm = min(ROW_TILE, l)
    assert l % tm == 0
    return pl.pallas_call(
        functools.partial(_proj_post_kernel, alpha=alpha),
        grid=(b, l // tm),
        in_specs=[
            pl.BlockSpec((None, tm, ka), lambda bb, t: (bb, t, 0)),
            pl.BlockSpec((ka, d), lambda bb, t: (0, 0)),
            pl.BlockSpec((None, tm, d), lambda bb, t: (bb, t, 0)),
            _mod_spec(gate_j, d, ctx_row, 0),
            _row_spec(d),
            _row_spec(d),
        ],
        out_specs=pl.BlockSpec((None, tm, d), lambda bb, t: (bb, t, 0)),
        out_shape=jax.ShapeDtypeStruct((b, l, d), F32),
        compiler_params=_cparams(2),
        name="proj_post",
    )(a, w, x, mod_i, ln_g, ln_b)


def _fnet_a_kernel(x_ref, sh_ref, sc_ref, cs_ref, o_ref, *, d, gc):
    u = _modulate(x_ref[...], sh_ref[...], sc_ref[...]).astype(BF16)
    cs = cs_ref[...]
    for g in range(d // gc):
        r = _dot(u[:, g * gc:(g + 1) * gc], cs)
        o_ref[:, g * gc:(g + 1) * gc] = r[:, :gc].astype(BF16)
        o_ref[:, d + g * gc:d + (g + 1) * gc] = r[:, gc:].astype(BF16)


def _fnet_a(x, mod_i, ctx_row, cs):
    b, l, d = x.shape
    gc = d // F_GROUPS
    tm = min(ROW_TILE, l)
    assert l % tm == 0
    return pl.pallas_call(
        functools.partial(_fnet_a_kernel, d=d, gc=gc),
        grid=(b, l // tm),
        in_specs=[
            pl.BlockSpec((None, tm, d), lambda bb, t: (bb, t, 0)),
            _mod_spec(0, d, ctx_row, 0),
            _mod_spec(1, d, ctx_row, 0),
            pl.BlockSpec((gc, 2 * gc), lambda bb, t: (0, 0)),
        ],
        out_specs=pl.BlockSpec((None, tm, 2 * d), lambda bb, t: (bb, t, 0)),
        out_shape=jax.ShapeDtypeStruct((b, l, 2 * d), BF16),
        compiler_params=_cparams(2),
        name="fnet_chan_dft",
    )(x, mod_i, mod_i, cs)


def _fnet_b_kernel(cl_ref, sl_ref, ab_ref, wf_ref, x_ref, gate_ref, g_ref, b_ref, o_ref, *, d, alpha):
    f = _dot(cl_ref[...], ab_ref[:, :d]) + _dot(sl_ref[...], ab_ref[:, d:])
    y = _dot(f.astype(BF16), wf_ref[...])
    z = alpha * x_ref[...] + gate_ref[...] * y
    o_ref[...] = _normalize(z, LN_EPS) * g_ref[...] + b_ref[...]


def _fnet_b(cl, sl, ab, wf, x, mod_i, gate_j, ctx_row, ln_g, ln_b, alpha):
    b, l, d = x.shape
    tm = min(ROW_TILE, l)
    assert l % tm == 0
    return pl.pallas_call(
        functools.partial(_fnet_b_kernel, d=d, alpha=alpha),
        grid=(b, l // tm),
        in_specs=[
            pl.BlockSpec((tm, l), lambda bb, t: (t, 0)),
            pl.BlockSpec((tm, l), lambda bb, t: (t, 0)),
            pl.BlockSpec((None, l, 2 * d), lambda bb, t: (bb, 0, 0)),
            pl.BlockSpec((d, d), lambda bb, t: (0, 0)),
            pl.BlockSpec((None, tm, d), lambda bb, t: (bb, t, 0)),
            _mod_spec(gate_j, d, ctx_row, 0),
            _row_spec(d),
            _row_spec(d),
        ],
        out_specs=pl.BlockSpec((None, tm, d), lambda bb, t: (bb, t, 0)),
        out_shape=jax.ShapeDtypeStruct((b, l, d), F32),
        compiler_params=_cparams(2),
        name="fnet_seq_dft",
    )(cl, sl, ab, wf, x, mod_i, ln_g, ln_b)


def _dft_tables(n, scale):
    j = jnp.arange(n, dtype=jnp.int32)
    ang = ((j[:, None] * j[None, :]) % n).astype(F32) * (2.0 * math.pi / n)
    return jnp.cos(ang) * scale, jnp.sin(ang) * scale


def _router_kernel(x_ref, sh_ref, sc_ref, rwt_ref, bias_ref, tri_ref, u_ref, te_ref, w_ref, rk_ref, cnt_ref,
                   *, n_groups, topk_groups, top_k):
    u = _modulate(x_ref[...], sh_ref[...], sc_ref[...])
    uh, ul = _split_bf16(u)
    u_ref[...] = _pack_halves(u)
    wh, wl = _split_bf16(rwt_ref[...])
    logits = _dot_nt(wh, uh) + _dot_nt(wl, uh) + _dot_nt(wh, ul)
    e, tm = logits.shape
    pg = e // n_groups
    neg = -jnp.inf
    scores = jax.nn.sigmoid(logits)
    sel = scores + bias_ref[...]
    sub = lax.broadcasted_iota(jnp.int32, (pg, tm), 0)
    groups = [sel[g * pg:(g + 1) * pg, :] for g in range(n_groups)]
    sgroups = [scores[g * pg:(g + 1) * pg, :] for g in range(n_groups)]

    gs_rows = []
    for s_g in groups:
        m1 = jnp.max(s_g, axis=0, keepdims=True)
        first = jnp.min(jnp.where(s_g == m1, sub, pg), axis=0, keepdims=True)
        m2 = jnp.max(jnp.where(sub == first, neg, s_g), axis=0, keepdims=True)
        gs_rows.append(m1 + m2)
    gs = jnp.concatenate(gs_rows, axis=0)
    gidx = lax.broadcasted_iota(jnp.int32, (n_groups, tm), 0)
    ok = jnp.zeros((n_groups, tm), F32)
    for _ in range(topk_groups):
        m = jnp.max(gs, axis=0, keepdims=True)
        first = jnp.min(jnp.where(gs == m, gidx, n_groups), axis=0, keepdims=True)
        hit = gidx == first
        ok = jnp.where(hit, 1.0, ok)
        gs = jnp.where(hit, neg, gs)

    cur = [jnp.where(ok[g:g + 1, :] > 0.0, groups[g], neg) for g in range(n_groups)]
    eidx = [sub + g * pg for g in range(n_groups)]
    chosen = [jnp.zeros((pg, tm), F32) for _ in range(n_groups)]
    e_rows, s_rows = [], []
    for _ in range(top_k):
        m = jnp.max(functools.reduce(jnp.maximum, cur), axis=0, keepdims=True)
        cand = functools.reduce(jnp.minimum, [jnp.where(cur[g] == m, eidx[g], e) for g in range(n_groups)])
        first = jnp.min(cand, axis=0, keepdims=True)
        picked = jnp.zeros((pg, tm), F32)
        for g in range(n_groups):
            hit = eidx[g] == first
            picked = picked + jnp.where(hit, sgroups[g], 0.0)
            chosen[g] = jnp.where(hit, 1.0, chosen[g])
            cur[g] = jnp.where(hit, neg, cur[g])
        e_rows.append(first)
        s_rows.append(jnp.sum(picked, axis=0, keepdims=True))
    w = jnp.concatenate(s_rows, axis=0)
    te_ref[...] = jnp.concatenate(e_rows, axis=0)
    w_ref[...] = w / jnp.sum(w, axis=0, keepdims=True) * ROUTED_SCALE

    sel_mask = jnp.concatenate(chosen, axis=0).astype(BF16)
    before = _dot(sel_mask, tri_ref[...])
    rk_rows = []
    for k in range(top_k):
        acc = jnp.zeros((pg, tm), F32)
        for g in range(n_groups):
            acc = acc + jnp.where(eidx[g] == e_rows[k], before[g * pg:(g + 1) * pg, :], 0.0)
        rk_rows.append(jnp.sum(acc, axis=0, keepdims=True))
    rk_ref[...] = jnp.concatenate(rk_rows, axis=0).astype(jnp.int32)
    cnt_ref[...] = _dot_nt(jnp.ones((8, tm), BF16), sel_mask)


def _router(x, mod_i, ctx_row, rwt, bias):
    b, l, d = x.shape
    e = rwt.shape[0]
    tm = min(ROW_TILE, l)
    assert l % tm == 0 and e % N_EXPERT_GROUPS == 0
    nt = l // tm
    t = b * l
    row = lax.broadcasted_iota(jnp.int32, (tm, tm), 0)
    col = lax.broadcasted_iota(jnp.int32, (tm, tm), 1)
    tri = (row < col).astype(BF16)
    kern = functools.partial(_router_kernel, n_groups=N_EXPERT_GROUPS, topk_groups=TOPK_GROUPS, top_k=TOP_K)
    tok_spec = pl.BlockSpec((TOP_K, tm), lambda bb, tt: (0, bb * nt + tt))
    return pl.pallas_call(
        kern,
        grid=(b, nt),
        in_specs=[
            pl.BlockSpec((None, tm, d), lambda bb, tt: (bb, tt, 0)),
            _mod_spec(3, d, ctx_row, 0),
            _mod_spec(4, d, ctx_row, 0),
            pl.BlockSpec((e, d), lambda bb, tt: (0, 0)),
            pl.BlockSpec((e, 1), lambda bb, tt: (0, 0)),
            pl.BlockSpec((tm, tm), lambda bb, tt: (0, 0)),
        ],
        out_specs=[
            pl.BlockSpec((None, tm, d // 2), lambda bb, tt: (bb, tt, 0)),
            tok_spec, tok_spec, tok_spec,
            pl.BlockSpec((None, 8, e), lambda bb, tt: (bb * nt + tt, 0, 0)),
        ],
        out_shape=[
            jax.ShapeDtypeStruct((b, l, d // 2), I32),
            jax.ShapeDtypeStruct((TOP_K, t), jnp.int32),
            jax.ShapeDtypeStruct((TOP_K, t), F32),
            jax.ShapeDtypeStruct((TOP_K, t), jnp.int32),
            jax.ShapeDtypeStruct((b * nt, 8, e), F32),
        ],
        compiler_params=_cparams(2),
        name="moe_router",
    )(x, mod_i, mod_i, rwt, bias.reshape(e, 1).astype(F32), tri)


def _expert_layout(cnt, tmb, n_blocks):
    cnt_i = cnt[:, 0, :].astype(jnp.int32)
    e = cnt_i.shape[1]
    counts = cnt_i.sum(axis=0)
    padded = (counts + tmb - 1) // tmb * tmb
    pend = jnp.cumsum(padded)
    base = (pend - padded)[None, :] + jnp.cumsum(cnt_i, axis=0) - cnt_i
    n_used = (pend[-1] // tmb).astype(jnp.int32)
    blk = jnp.arange(n_blocks, dtype=jnp.int32)
    block_e = jnp.sum((blk[:, None] * tmb >= pend[None, :]).astype(jnp.int32), axis=1)
    last_e = jnp.sum((jnp.maximum(n_used - 1, 0) * tmb >= pend).astype(jnp.int32))
    block_e = jnp.clip(jnp.where(blk < n_used, block_e, last_e), 0, e - 1).astype(jnp.int32)
    seg_end = (pend - padded + counts)[block_e]
    block_valid = jnp.clip(seg_end - blk * tmb, 0, tmb).astype(jnp.int32)
    return base.reshape(-1).astype(jnp.int32), block_e, block_valid, n_used.reshape(1)


def _pos_kernel(base_ref, te_ref, rk_ref, pos_ref, *, n_experts):
    i = pl.program_id(0)
    te = te_ref[...]
    pos = rk_ref[...]
    for e in range(n_experts):
        pos = pos + jnp.where(te == e, base_ref[i * n_experts + e], 0)
    pos_ref[...] = pos


def _positions(base, top_e, rank, n_experts, tm):
    k, t = top_e.shape
    spec = pl.BlockSpec((k, tm), lambda i, base_ref: (0, i))
    return pl.pallas_call(
        functools.partial(_pos_kernel, n_experts=n_experts),
        grid_spec=pltpu.PrefetchScalarGridSpec(
            num_scalar_prefetch=1, grid=(t // tm,), in_specs=[spec, spec], out_specs=spec),
        out_shape=jax.ShapeDtypeStruct((k, t), jnp.int32),
        compiler_params=_cparams(1),
        name="moe_positions",
    )(base, top_e, rank)


def _sc_dispatch(rows, pos3, cap):
    t, w = rows.shape
    n_chunks, top_k, n = pos3.shape
    n_workers = SC_CORES * SC_SUBCORES
    assert n_chunks * n == t and n <= LANES
    per_worker = -(-n_chunks // n_workers)
    mesh = plsc.VectorSubcoreMesh(core_axis_name="core", subcore_axis_name="subcore",
                                  num_cores=SC_CORES, num_subcores=SC_SUBCORES)

    def body(rows_hbm, pos_hbm, out_hbm, idx_v, rows_v, sem):
        wid = lax.axis_index("subcore") * SC_CORES + lax.axis_index("core")

        @pl.loop(0, per_worker)
        def _(j):
            c = wid * per_worker + j

            @pl.when(c < n_chunks)
            def _():
                pltpu.sync_copy(pos_hbm.at[c], idx_v)
                pltpu.sync_copy(rows_hbm.at[pl.ds(c * n, n)], rows_v)
                copies = [pltpu.async_copy(rows_v, out_hbm.at[idx_v.at[k]], sem) for k in range(top_k)]
                for cp in copies:
                    cp.wait()

    return pl.kernel(
        body,
        out_type=jax.ShapeDtypeStruct((cap, w), I32),
        mesh=mesh,
        scratch_types=[pltpu.VMEM((top_k, n), I32), pltpu.VMEM((n, w), I32), pltpu.SemaphoreType.DMA],
        name="moe_dispatch_sc",
    )(rows, pos3)


def _sc_combine_gather(rows, pos3):
    cap, w = rows.shape
    n_chunks, top_k, n = pos3.shape
    t = n_chunks * n
    n_workers = SC_CORES * SC_SUBCORES
    assert n <= LANES
    per_worker = -(-n_chunks // n_workers)
    mesh = plsc.VectorSubcoreMesh(core_axis_name="core", subcore_axis_name="subcore",
                                  num_cores=SC_CORES, num_subcores=SC_SUBCORES)

    def body(rows_hbm, pos_hbm, out_hbm, idx_v, buf0, buf1, gsem0, gsem1, wsem0, wsem1):
        wid = lax.axis_index("subcore") * SC_CORES + lax.axis_index("core")
        bufs, gsems, wsems = (buf0, buf1), (gsem0, gsem1), (wsem0, wsem1)

        @pl.loop(0, per_worker)
        def _(j):
            c = wid * per_worker + j

            @pl.when(c < n_chunks)
            def _():
                pltpu.sync_copy(pos_hbm.at[c], idx_v)
                gathers = [None] * top_k
                writes = [None] * top_k
                gathers[0] = pltpu.async_copy(rows_hbm.at[idx_v.at[0]], bufs[0], gsems[0])
                for k in range(top_k):
                    if k + 1 < top_k:
                        if k >= 1:
                            writes[k - 1].wait()
                        gathers[k + 1] = pltpu.async_copy(rows_hbm.at[idx_v.at[k + 1]], bufs[(k + 1) % 2],
                                                          gsems[(k + 1) % 2])
                    gathers[k].wait()
                    writes[k] = pltpu.async_copy(bufs[k % 2], out_hbm.at[k, pl.ds(c * n, n)], wsems[k % 2])
                writes[top_k - 2].wait()
                writes[top_k - 1].wait()

    return pl.kernel(
        body,
        out_type=jax.ShapeDtypeStruct((top_k, t, w), I32),
        mesh=mesh,
        scratch_types=[pltpu.VMEM((top_k, n), I32), pltpu.VMEM((n, w), I32), pltpu.VMEM((n, w), I32),
                       pltpu.SemaphoreType.DMA, pltpu.SemaphoreType.DMA,
                       pltpu.SemaphoreType.DMA, pltpu.SemaphoreType.DMA],
        name="moe_combine_sc",
    )(rows, pos3)


def _gmm_kernel(be_ref, bv_ref, nu_ref, x_ref, wg_ref, wu_ref, wd_ref, o_ref):
    i = pl.program_id(0)

    @pl.when(i < nu_ref[0])
    def _():
        words = x_ref[...]
        tmb, half = words.shape
        live = lax.broadcasted_iota(I32, (tmb, half), 0) < bv_ref[i]
        xa, xb = _unpack_halves(jnp.where(live, words, 0))
        g = _dot(xa, wg_ref[:half, :]) + _dot(xb, wg_ref[half:, :])
        u = _dot(xa, wu_ref[:half, :]) + _dot(xb, wu_ref[half:, :])
        o_ref[...] = _pack_halves(_dot((_silu(g) * u).astype(BF16), wd_ref[...]))


def _grouped_experts(x_sorted, block_e, block_valid, n_used, wg, wu, wd, tmb):
    cap, half = x_sorted.shape
    d = 2 * half
    ff = wg.shape[2]
    n_blocks = cap // tmb

    def row_map(i, be, bv, nu):
        return (jnp.minimum(i, nu[0] - 1), 0)

    def w_map(i, be, bv, nu):
        return (be[i], 0, 0)

    return pl.pallas_call(
        _gmm_kernel,
        grid_spec=pltpu.PrefetchScalarGridSpec(
            num_scalar_prefetch=3,
            grid=(n_blocks,),
            in_specs=[
                pl.BlockSpec((tmb, half), row_map),
                pl.BlockSpec((None, d, ff), w_map),
                pl.BlockSpec((None, d, ff), w_map),
                pl.BlockSpec((None, ff, d), w_map),
            ],
            out_specs=pl.BlockSpec((tmb, half), row_map),
        ),
        out_shape=jax.ShapeDtypeStruct((cap, half), I32),
        compiler_params=_cparams(1),
        name="moe_experts",
    )(block_e, block_valid, n_used, x_sorted, wg, wu, wd)


def _moe_out_kernel(u_ref, w_ref, *refs, alpha, top_k):
    y_refs = refs[:top_k]
    sg_ref, su_ref, sd_ref, x_ref, gate_ref, g_ref, b_ref, o_ref = refs[top_k:]
    ua, ub = _unpack_halves(u_ref[...])
    half = ua.shape[1]
    g = _dot(ua, sg_ref[:half, :]) + _dot(ub, sg_ref[half:, :])
    s = _dot(ua, su_ref[:half, :]) + _dot(ub, su_ref[half:, :])
    y = _dot((_silu(g) * s).astype(BF16), sd_ref[...])
    w = w_ref[...]
    ya = y[:, :half]
    yb = y[:, half:]
    for k in range(top_k):
        words = y_refs[k][...]
        wk = w[:, k:k + 1]
        ya = ya + wk * pltpu.bitcast(words & jnp.int32(-65536), F32)
        yb = yb + wk * pltpu.bitcast(lax.shift_left(words, 16), F32)
    x = x_ref[...]
    gate = gate_ref[...]
    za = alpha * x[:, :half] + gate[:, :half] * ya
    zb = alpha * x[:, half:] + gate[:, half:] * yb
    z = jnp.concatenate([za, zb], axis=1)
    o_ref[...] = _normalize(z, LN_EPS) * g_ref[...] + b_ref[...]


def _moe_out(u2, wts, ys, sg, su, sd, x, mod_i, ctx_row, ln_g, ln_b, alpha):
    b, l, d = x.shape
    ff = sg.shape[1]
    top_k = ys.shape[0]
    tm = min(ROW_TILE, l)
    assert l % tm == 0
    tok = pl.BlockSpec((None, tm, d), lambda bb, t: (bb, t, 0))
    words = pl.BlockSpec((None, tm, d // 2), lambda bb, t: (bb, t, 0))
    picks = [pl.BlockSpec((None, None, tm, d // 2), functools.partial(lambda bb, t, k: (k, bb, t, 0), k=k))
             for k in range(top_k)]
    return pl.pallas_call(
        functools.partial(_moe_out_kernel, alpha=alpha, top_k=top_k),
        grid=(b, l // tm),
        in_specs=[words, pl.BlockSpec((None, tm, top_k), lambda bb, t: (bb, t, 0))] + picks + [
            pl.BlockSpec((d, ff), lambda bb, t: (0, 0)),
            pl.BlockSpec((d, ff), lambda bb, t: (0, 0)),
            pl.BlockSpec((ff, d), lambda bb, t: (0, 0)),
            tok,
            _mod_spec(5, d, ctx_row, 0),
            _row_spec(d),
            _row_spec(d),
        ],
        out_specs=tok,
        out_shape=jax.ShapeDtypeStruct((b, l, d), F32),
        compiler_params=_cparams(2),
        name="moe_shared_post",
    )(u2, wts, *([ys] * top_k), sg, su, sd, x, mod_i, ln_g, ln_b)


def _moe_layer(x, mod_i, ctx_row, rwt, rbias, wg, wu, wd, sg, su, sd, ln_g, ln_b, alpha):
    b, l, d = x.shape
    t = b * l
    e = rwt.shape[0]
    tmb = MOE_ROW_TILE
    n_blocks = -(-(t * TOP_K) // tmb) + e
    u2, top_e, wts, rank, cnt = _router(x, mod_i, ctx_row, rwt, rbias)
    base, block_e, block_valid, n_used = _expert_layout(cnt, tmb, n_blocks)
    pos = _positions(base, top_e, rank, e, min(ROW_TILE, l))
    pos_d = pos.reshape(TOP_K, t // DISPATCH_CHUNK, DISPATCH_CHUNK).transpose(1, 0, 2)
    pos_c = pos.reshape(TOP_K, t // COMBINE_CHUNK, COMBINE_CHUNK).transpose(1, 0, 2)
    x_sorted = _sc_dispatch(u2.reshape(t, d // 2), pos_d, n_blocks * tmb)
    y_sorted = _grouped_experts(x_sorted, block_e, block_valid, n_used, wg, wu, wd, tmb)
    ys = _sc_combine_gather(y_sorted, pos_c).reshape(TOP_K, b, l, d // 2)
    return _moe_out(u2, wts.T.reshape(b, l, TOP_K), ys, sg, su, sd, x, mod_i, ctx_row, ln_g, ln_b, alpha)


def _rope_tables(l, hd, n_q_heads, n_k_heads, rope):
    axis_rot = hd // 2
    qscale = hd ** -0.5
    if rope:
        t = jnp.arange(l, dtype=jnp.int32)
        r = (t // GRID_W).astype(F32)
        col = (t % GRID_W).astype(F32)
        inv = ROPE_BASE ** (-jnp.arange(0, axis_rot, 2, dtype=F32) / axis_rot)
        ar = r[:, None] * inv
        ac = col[:, None] * inv
        ang = jnp.concatenate([ar, ar, ac, ac], axis=-1)
        cos, sin = jnp.cos(ang), jnp.sin(ang)
    else:
        cos, sin = jnp.ones((l, hd), F32), jnp.zeros((l, hd), F32)
    cos_t = jnp.concatenate([jnp.tile(cos, (1, n_q_heads)) * qscale, jnp.tile(cos, (1, n_k_heads))], axis=1)
    sin_t = jnp.concatenate([jnp.tile(sin, (1, n_q_heads)) * qscale, jnp.tile(sin, (1, n_k_heads))], axis=1)
    return cos_t, sin_t


def _rot_columns(w, hd):
    d, n = w.shape
    q = hd // 4
    w4 = w.reshape(d, n // (2 * q), 2, q)
    return jnp.stack([-w4[:, :, 1, :], w4[:, :, 0, :]], axis=2).reshape(d, n)


def kernel(x, c, ctx, c_ctx, w_ada, b_ada, ln_g, ln_b, attn_w_qkv, attn_w_o, attn_sinks, fnet_w,
           router_w, router_bias, exp_w_gate, exp_w_up, exp_w_down, sh_w_gate, sh_w_up, sh_w_down):
    b, l, d = x.shape
    cl = ctx.shape[1]
    depth = w_ada.shape[0]
    n_heads = attn_sinks.shape[1]
    hd = d // n_heads
    qw = n_heads * hd
    kvw = N_KV_HEADS * hd
    alpha = (2.0 * depth) ** 0.25
    gc = d // F_GROUPS

    mp = -(-(b + 1) // 16) * 16
    cvec = jnp.concatenate([c, c_ctx[None, :], jnp.zeros((mp - b - 1, d), F32)], axis=0)
    mod = _ada_all(cvec, w_ada, b_ada).reshape(depth, mp, 6, 1, d)

    def run_stream(x, h, mod):
        b = x.shape[0]
        for i in range(depth):
            kind = i % N_MIXERS
            j = i // N_MIXERS
            update_ctx = any((m % N_MIXERS) == 0 for m in range(i + 1, depth))
            mod_i = mod[i]
            g1 = ln_g[i, 0].reshape(1, d)
            b1 = ln_b[i, 0].reshape(1, d)
            g2 = ln_g[i, 1].reshape(1, d)
            b2 = ln_b[i, 1].reshape(1, d)

            if kind == 0:
                w_all, w_o = attn_w[j]
                q, k, v = _qkv_proj(x, mod_i, None, w_all, cos_l, sin_l, qw, kvw)
                q_c, k_c, v_c = _qkv_proj(h, mod_i, b, w_all, cos_c, sin_c, qw, kvw)
                o = _attention(q, k, v, k_c, v_c, attn_sinks[j], True)
                x = _proj_post(o, w_o, x, mod_i, 2, None, g1, b1, alpha)
                if update_ctx:
                    o_c = _attention(q_c, None, None, k_c, v_c, attn_sinks[j], False)
                    h = _proj_post(o_c, w_o, h, mod_i, 2, b, g1, b1, alpha)
            else:
                streams = [(x, None)] + ([(h, b)] if update_ctx else [])
                outs = []
                for s, ctx_row in streams:
                    c_l, s_l = dft_seq[s.shape[1]]
                    ab = _fnet_a(s, mod_i, ctx_row, cs)
                    outs.append(_fnet_b(c_l, s_l, ab, fnet_wb[j], s, mod_i, 2, ctx_row, g1, b1, alpha))
                x = outs[0]
                if update_ctx:
                    h = outs[1]

            x = _moe_layer(x, mod_i, None, *moe_w[i], g2, b2, alpha)
            if update_ctx:
                h = _moe_layer(h, mod_i, b, *moe_w[i], g2, b2, alpha)
        return x

    attn_w = []
    for j in range(attn_w_qkv.shape[0]):
        w = attn_w_qkv[j]
        w_all = jnp.concatenate([w, _rot_columns(w[:, :qw + kvw], hd)], axis=1).astype(BF16)
        attn_w.append((w_all, attn_w_o[j].astype(BF16)))
    cos_l, sin_l = _rope_tables(l, hd, n_heads, N_KV_HEADS, True)
    cos_c, sin_c = _rope_tables(cl, hd, n_heads, N_KV_HEADS, False)
    fnet_wb = [fnet_w[j].astype(BF16) for j in range(fnet_w.shape[0])]
    cc, sc = _dft_tables(gc, gc ** -0.5)
    cs = jnp.concatenate([cc, sc], axis=1).astype(BF16)
    dft_seq = {}
    for ls in (l, cl):
        c_l, s_l = _dft_tables(ls, ls ** -0.5)
        dft_seq[ls] = (c_l.astype(BF16), (-s_l).astype(BF16))
    moe_w = [(router_w[i].T, router_bias[i], exp_w_gate[i].astype(BF16), exp_w_up[i].astype(BF16),
              exp_w_down[i].astype(BF16), sh_w_gate[i].astype(BF16), sh_w_up[i].astype(BF16),
              sh_w_down[i].astype(BF16)) for i in range(depth)]

    n_streams = N_STREAMS if b % N_STREAMS == 0 else 1
    bs = b // n_streams
    outs = []
    for s in range(n_streams):
        mod_s = jnp.concatenate([mod[:, s * bs:(s + 1) * bs], mod[:, b:b + 1]], axis=1)
        outs.append(run_stream(x[s * bs:(s + 1) * bs], ctx[s * bs:(s + 1) * bs], mod_s))
    return outs[0] if n_streams == 1 else jnp.concatenate(outs, axis=0)
```

```python
import functools
import math

import jax
import jax.numpy as jnp
from jax import lax
from jax.experimental import pallas as pl
from jax.experimental.pallas import tpu as pltpu
from jax.experimental.pallas import tpu_sc as plsc

F32 = jnp.float32
BF16 = jnp.bfloat16
I32 = jnp.int32

N_KV_HEADS = 4
WINDOW = 128
GRID_W = 64
ROPE_BASE = 10000.0
F_GROUPS = 4
TOP_K = 8
N_EXPERT_GROUPS = 8
TOPK_GROUPS = 4
ROUTED_SCALE = 2.5
N_MIXERS = 2
LN_EPS = 1e-5
MOD_EPS = 1e-6

LANES = 128
VMEM_LIMIT_BYTES = 52 * 1024 * 1024
ROW_TILE = 512
QKV_ROW_TILE = 256
Q_TILE = 128
ATTN_ROW_CHUNK = 32
ATTN_UNROLL = 16
MOE_ROW_TILE = 512
MOE_BLOCKS_PER_STEP = 2
DISPATCH_CHUNK = 128
COMBINE_CHUNK = 64
N_STREAMS = 2
SC_CORES = 2
SC_SUBCORES = 16
NEG_BIG = -1e30


def _cparams(n_axes):
    return pltpu.CompilerParams(dimension_semantics=("arbitrary",) * n_axes,
                                vmem_limit_bytes=VMEM_LIMIT_BYTES)


def _dot(a, b):
    return jnp.dot(a, b, preferred_element_type=F32)


def _dot_nt(a, b):
    return lax.dot_general(a, b, (((1,), (1,)), ((), ())), preferred_element_type=F32)


def _split_bf16(a):
    hi = a.astype(BF16)
    lo = (a - hi.astype(F32)).astype(BF16)
    return hi, lo


def _normalize(x, eps):
    mu = jnp.mean(x, axis=-1, keepdims=True)
    xc = x - mu
    var = jnp.mean(xc * xc, axis=-1, keepdims=True)
    return xc * lax.rsqrt(var + eps)


def _modulate(x, shift, scale):
    return _normalize(x, MOD_EPS) * (1.0 + scale) + shift


def _silu(x):
    return x * jax.nn.sigmoid(x)


def _pack_halves(x):
    n = x.shape[1] // 2
    r = x.astype(BF16).astype(F32)
    hi = pltpu.bitcast(r[:, :n], I32)
    lo = pltpu.bitcast(r[:, n:], I32)
    return hi | lax.shift_right_logical(lo, 16)


def _unpack_halves(w):
    a = pltpu.bitcast(w & jnp.int32(-65536), F32).astype(BF16)
    b = pltpu.bitcast(lax.shift_left(w, 16), F32).astype(BF16)
    return a, b


def _ada_kernel(c_ref, w_ref, b_ref, o_ref):
    s = _silu(c_ref[...])
    sh, sl = _split_bf16(s)
    wh, wl = _split_bf16(w_ref[...])
    o_ref[...] = _dot(sh, wh) + _dot(sl, wh) + _dot(sh, wl) + b_ref[...]


def _ada_all(cvec, w_ada, b_ada):
    depth, d, n = w_ada.shape
    mp = cvec.shape[0]
    tn = 1536
    assert n % tn == 0
    return pl.pallas_call(
        _ada_kernel,
        grid=(depth, n // tn),
        in_specs=[
            pl.BlockSpec((mp, d), lambda i, j: (0, 0)),
            pl.BlockSpec((None, d, tn), lambda i, j: (i, 0, j)),
            pl.BlockSpec((None, 1, tn), lambda i, j: (i, 0, j)),
        ],
        out_specs=pl.BlockSpec((None, mp, tn), lambda i, j: (i, 0, j)),
        out_shape=jax.ShapeDtypeStruct((depth, mp, n), F32),
        compiler_params=_cparams(2),
        name="ada_mod",
    )(cvec, w_ada, b_ada.reshape(depth, 1, n))


def _mod_spec(j, d, ctx_row, batch_axis):
    if ctx_row is None:
        return pl.BlockSpec((None, None, 1, d), lambda *g: (g[batch_axis], j, 0, 0))
    return pl.BlockSpec((None, None, 1, d), lambda *g: (ctx_row, j, 0, 0))


def _row_spec(d):
    return pl.BlockSpec((1, d), lambda *g: (0, 0))


def _qkv_kernel(x_ref, sh_ref, sc_ref, w_ref, cos_ref, sin_ref, q_ref, k_ref, v_ref, *, qw, kvw):
    u = _modulate(x_ref[...], sh_ref[...], sc_ref[...]).astype(BF16)
    r = _dot(u, w_ref[...])
    qk = r[:, :qw + kvw] * cos_ref[...] + r[:, qw + 2 * kvw:] * sin_ref[...]
    q_ref[...] = qk[:, :qw].astype(BF16)
    k_ref[...] = qk[:, qw:].astype(BF16)
    v_ref[...] = r[:, qw + kvw:qw + 2 * kvw].astype(BF16)


def _qkv_proj(x, mod_i, ctx_row, w_all, cos_t, sin_t, qw, kvw):
    b, l, d = x.shape
    tm = min(QKV_ROW_TILE, l)
    assert l % tm == 0
    n_all = w_all.shape[1]
    return pl.pallas_call(
        functools.partial(_qkv_kernel, qw=qw, kvw=kvw),
        grid=(l // tm, b),
        in_specs=[
            pl.BlockSpec((None, tm, d), lambda t, bb: (bb, t, 0)),
            _mod_spec(0, d, ctx_row, 1),
            _mod_spec(1, d, ctx_row, 1),
            pl.BlockSpec((d, n_all), lambda t, bb: (0, 0)),
            pl.BlockSpec((tm, qw + kvw), lambda t, bb: (t, 0)),
            pl.BlockSpec((tm, qw + kvw), lambda t, bb: (t, 0)),
        ],
        out_specs=[
            pl.BlockSpec((None, tm, qw), lambda t, bb: (bb, t, 0)),
            pl.BlockSpec((None, tm, kvw), lambda t, bb: (bb, t, 0)),
            pl.BlockSpec((None, tm, kvw), lambda t, bb: (bb, t, 0)),
        ],
        out_shape=[
            jax.ShapeDtypeStruct((b, l, qw), BF16),
            jax.ShapeDtypeStruct((b, l, kvw), BF16),
            jax.ShapeDtypeStruct((b, l, kvw), BF16),
        ],
        compiler_params=_cparams(2),
        name="qkv_rope",
    )(x, mod_i, mod_i, w_all, cos_t, sin_t)


def _attn_kernel(sink_ref, q_ref, *refs, tq, seq, n_kv, group, hd, has_window):
    if has_window:
        k_ref, v_ref, kc_ref, vc_ref, o_ref, s_scr, p_scr, m_scr, bias_scr = refs
    else:
        kc_ref, vc_ref, o_ref, s_scr, p_scr, m_scr = refs
    q = q_ref[...]
    kc = kc_ref[...]
    vc = vc_ref[...]
    span = tq + 2 * WINDOW if has_window else 0
    rows = group * tq
    if has_window:
        q0 = pl.program_id(1) * tq
        start = pl.multiple_of(jnp.clip(q0 - WINDOW, 0, seq - span), LANES)
        kw = k_ref[pl.ds(start, span), :]
        vw = v_ref[pl.ds(start, span), :]
        qpos = q0 + lax.broadcasted_iota(jnp.int32, (tq, span), 0)
        kpos = start + lax.broadcasted_iota(jnp.int32, (tq, span), 1)
        bias_scr[...] = jnp.where(jnp.abs(qpos - kpos) <= WINDOW, 0.0, NEG_BIG)

    for h in range(n_kv):
        heads = [h * group + g for g in range(group)]
        qh = jnp.concatenate([q[:, j * hd:(j + 1) * hd] for j in heads], axis=0)
        if has_window:
            s_scr[h, :, :span] = _dot_nt(qh, kw[:, h * hd:(h + 1) * hd])
        s_scr[h, :, span:] = _dot_nt(qh, kc[:, h * hd:(h + 1) * hd])

    chunks_per_head = tq // ATTN_ROW_CHUNK
    n_chunks = rows // ATTN_ROW_CHUNK
    n_tiles = (span + kc.shape[0]) // LANES
    win_tiles = span // LANES

    def logit_tiles(h, r):
        row = pl.multiple_of(r * ATTN_ROW_CHUNK, ATTN_ROW_CHUNK)
        s = s_scr[h, pl.ds(row, ATTN_ROW_CHUNK), :]
        tiles = [s[:, i * LANES:(i + 1) * LANES] for i in range(n_tiles)]
        if has_window:
            brow = pl.multiple_of((r % chunks_per_head) * ATTN_ROW_CHUNK, ATTN_ROW_CHUNK)
            bias = bias_scr[pl.ds(brow, ATTN_ROW_CHUNK), :]
            tiles = [t + bias[:, i * LANES:(i + 1) * LANES] if i < win_tiles else t for i, t in enumerate(tiles)]
        return row, tiles

    for h in range(n_kv):
        def row_max(r, carry, h=h):
            row, tiles = logit_tiles(h, r)
            sink = sink_ref[h * group + r // chunks_per_head]
            m = jnp.max(functools.reduce(jnp.maximum, tiles), axis=-1, keepdims=True)
            m_scr[h, pl.ds(row, ATTN_ROW_CHUNK), :] = jnp.broadcast_to(jnp.maximum(m, sink),
                                                                     (ATTN_ROW_CHUNK, LANES))
            return carry

        lax.fori_loop(0, n_chunks, row_max, 0, unroll=ATTN_UNROLL)

    for h in range(n_kv):
        def probs(r, carry, h=h):
            row, tiles = logit_tiles(h, r)
            sink = sink_ref[h * group + r // chunks_per_head]
            m = m_scr[h, pl.ds(row, ATTN_ROW_CHUNK), :]
            es = [jnp.exp(t - m) for t in tiles]
            for i, e in enumerate(es):
                p_scr[h, pl.ds(row, ATTN_ROW_CHUNK), i * LANES:(i + 1) * LANES] = e.astype(BF16)
            den = jnp.sum(functools.reduce(jnp.add, es), axis=-1, keepdims=True) + jnp.exp(sink - m)
            m_scr[h, pl.ds(row, ATTN_ROW_CHUNK), :] = 1.0 / den
            return carry

        lax.fori_loop(0, n_chunks, probs, 0, unroll=ATTN_UNROLL)

    for h in range(n_kv):
        o = _dot(p_scr[h, :, span:], vc[:, h * hd:(h + 1) * hd])
        if has_window:
            o = o + _dot(p_scr[h, :, :span], vw[:, h * hd:(h + 1) * hd])
        o = o * m_scr[h, :, :hd]
        for g in range(group):
            j = h * group + g
            o_ref[:, j * hd:(j + 1) * hd] = o[g * tq:(g + 1) * tq, :].astype(o_ref.dtype)


def _attention(q, k, v, kc, vc, sinks, has_window):
    b, l, qw = q.shape
    c, kvw = kc.shape[1], kc.shape[2]
    hd = kvw // N_KV_HEADS
    group = qw // kvw
    tq = Q_TILE if has_window else l
    assert l % tq == 0
    if has_window:
        assert l >= tq + 2 * WINDOW
    kern = functools.partial(_attn_kernel, tq=tq, seq=l, n_kv=N_KV_HEADS, group=group, hd=hd,
                             has_window=has_window)
    in_specs = [pl.BlockSpec(memory_space=pltpu.SMEM),
                pl.BlockSpec((None, tq, qw), lambda bb, t: (bb, t, 0))]
    args = [sinks, q]
    if has_window:
        in_specs += [pl.BlockSpec((None, l, kvw), lambda bb, t: (bb, 0, 0)),
                     pl.BlockSpec((None, l, kvw), lambda bb, t: (bb, 0, 0))]
        args += [k, v]
    in_specs += [pl.BlockSpec((None, c, kvw), lambda bb, t: (bb, 0, 0)),
                 pl.BlockSpec((None, c, kvw), lambda bb, t: (bb, 0, 0))]
    args += [kc, vc]
    n_keys = c + (tq + 2 * WINDOW if has_window else 0)
    scratch = [pltpu.VMEM((N_KV_HEADS, group * tq, n_keys), F32),
               pltpu.VMEM((N_KV_HEADS, group * tq, n_keys), BF16),
               pltpu.VMEM((N_KV_HEADS, group * tq, LANES), F32)]
    if has_window:
        scratch.append(pltpu.VMEM((tq, tq + 2 * WINDOW), F32))
    return pl.pallas_call(
        kern,
        grid=(b, l // tq),
        in_specs=in_specs,
        out_specs=pl.BlockSpec((None, tq, qw), lambda bb, t: (bb, t, 0)),
        out_shape=jax.ShapeDtypeStruct((b, l, qw), BF16),
        scratch_shapes=scratch,
        compiler_params=_cparams(2),
        name="win_attn" if has_window else "ctx_attn",
    )(*args)


def _proj_post_kernel(a_ref, w_ref, x_ref, gate_ref, g_ref, b_ref, *refs, alpha):
    route_in, o_ref, route_out = refs[:N_ROUTE_IN], refs[N_ROUTE_IN], refs[N_ROUTE_IN + 1:]
    y = _dot(a_ref[...], w_ref[...])
    z = alpha * x_ref[...] + gate_ref[...] * y
    x_new = _normalize(z, LN_EPS) * g_ref[...] + b_ref[...]
    o_ref[...] = x_new
    _route_tokens(x_new, *route_in, *route_out)


def _proj_post(a, w, x, mod_i, gate_j, ctx_row, ln_g, ln_b, alpha, rwt, rbias):
    b, l, d = x.shape
    ka = a.shape[2]
    tm = min(ROW_TILE, l)
    assert l % tm == 0
    r_in, r_args, r_out, r_shape = _route_io(b, l, d, tm, mod_i, ctx_row, rwt, rbias)
    res = pl.pallas_call(
        functools.partial(_proj_post_kernel, alpha=alpha),
        grid=(b, l // tm),
        in_specs=[
            pl.BlockSpec((None, tm, ka), lambda bb, t: (bb, t, 0)),
            pl.BlockSpec((ka, d), lambda bb, t: (0, 0)),
            pl.BlockSpec((None, tm, d), lambda bb, t: (bb, t, 0)),
            _mod_spec(gate_j, d, ctx_row, 0),
            _row_spec(d),
            _row_spec(d),
        ] + r_in,
        out_specs=[pl.BlockSpec((None, tm, d), lambda bb, t: (bb, t, 0))] + r_out,
        out_shape=[jax.ShapeDtypeStruct((b, l, d), F32)] + r_shape,
        compiler_params=_cparams(2),
        name="proj_post_route",
    )(a, w, x, mod_i, ln_g, ln_b, *r_args)
    return res[0], res[1:]


def _fnet_a_kernel(x_ref, sh_ref, sc_ref, cs_ref, o_ref, *, d, gc):
    u = _modulate(x_ref[...], sh_ref[...], sc_ref[...]).astype(BF16)
    cs = cs_ref[...]
    for g in range(d // gc):
        r = _dot(u[:, g * gc:(g + 1) * gc], cs)
        o_ref[:, g * gc:(g + 1) * gc] = r[:, :gc].astype(BF16)
        o_ref[:, d + g * gc:d + (g + 1) * gc] = r[:, gc:].astype(BF16)


def _fnet_a(x, mod_i, ctx_row, cs):
    b, l, d = x.shape
    gc = d // F_GROUPS
    tm = min(ROW_TILE, l)
    assert l % tm == 0
    return pl.pallas_call(
        functools.partial(_fnet_a_kernel, d=d, gc=gc),
        grid=(b, l // tm),
        in_specs=[
            pl.BlockSpec((None, tm, d), lambda bb, t: (bb, t, 0)),
            _mod_spec(0, d, ctx_row, 0),
            _mod_spec(1, d, ctx_row, 0),
            pl.BlockSpec((gc, 2 * gc), lambda bb, t: (0, 0)),
        ],
        out_specs=pl.BlockSpec((None, tm, 2 * d), lambda bb, t: (bb, t, 0)),
        out_shape=jax.ShapeDtypeStruct((b, l, 2 * d), BF16),
        compiler_params=_cparams(2),
        name="fnet_chan_dft",
    )(x, mod_i, mod_i, cs)


def _fnet_b_kernel(cl_ref, sl_ref, ab_ref, wf_ref, x_ref, gate_ref, g_ref, b_ref, *refs, d, alpha):
    route_in, o_ref, route_out = refs[:N_ROUTE_IN], refs[N_ROUTE_IN], refs[N_ROUTE_IN + 1:]
    f = _dot(cl_ref[...], ab_ref[:, :d]) + _dot(sl_ref[...], ab_ref[:, d:])
    y = _dot(f.astype(BF16), wf_ref[...])
    z = alpha * x_ref[...] + gate_ref[...] * y
    x_new = _normalize(z, LN_EPS) * g_ref[...] + b_ref[...]
    o_ref[...] = x_new
    _route_tokens(x_new, *route_in, *route_out)


def _fnet_b(cl, sl, ab, wf, x, mod_i, gate_j, ctx_row, ln_g, ln_b, alpha, rwt, rbias):
    b, l, d = x.shape
    tm = min(ROW_TILE, l)
    assert l % tm == 0
    r_in, r_args, r_out, r_shape = _route_io(b, l, d, tm, mod_i, ctx_row, rwt, rbias)
    res = pl.pallas_call(
        functools.partial(_fnet_b_kernel, d=d, alpha=alpha),
        grid=(b, l // tm),
        in_specs=[
            pl.BlockSpec((tm, l), lambda bb, t: (t, 0)),
            pl.BlockSpec((tm, l), lambda bb, t: (t, 0)),
            pl.BlockSpec((None, l, 2 * d), lambda bb, t: (bb, 0, 0)),
            pl.BlockSpec((d, d), lambda bb, t: (0, 0)),
            pl.BlockSpec((None, tm, d), lambda bb, t: (bb, t, 0)),
            _mod_spec(gate_j, d, ctx_row, 0),
            _row_spec(d),
            _row_spec(d),
        ] + r_in,
        out_specs=[pl.BlockSpec((None, tm, d), lambda bb, t: (bb, t, 0))] + r_out,
        out_shape=[jax.ShapeDtypeStruct((b, l, d), F32)] + r_shape,
        compiler_params=_cparams(2),
        name="fnet_seq_dft_route",
    )(cl, sl, ab, wf, x, mod_i, ln_g, ln_b, *r_args)
    return res[0], res[1:]


def _dft_tables(n, scale):
    j = jnp.arange(n, dtype=jnp.int32)
    ang = ((j[:, None] * j[None, :]) % n).astype(F32) * (2.0 * math.pi / n)
    return jnp.cos(ang) * scale, jnp.sin(ang) * scale


def _route_tokens(x, sh_ref, sc_ref, rwt_ref, bias_ref, tri_ref, u_ref, te_ref, w_ref, rk_ref, cnt_ref):
    n_groups, topk_groups, top_k = N_EXPERT_GROUPS, TOPK_GROUPS, TOP_K
    u = _modulate(x, sh_ref[...], sc_ref[...])
    uh, ul = _split_bf16(u)
    u_ref[...] = _pack_halves(u)
    wh, wl = _split_bf16(rwt_ref[...])
    logits = _dot_nt(wh, uh) + _dot_nt(wl, uh) + _dot_nt(wh, ul)
    e, tm = logits.shape
    pg = e // n_groups
    neg = -jnp.inf
    scores = jax.nn.sigmoid(logits)
    sel = scores + bias_ref[...]
    sub = lax.broadcasted_iota(jnp.int32, (pg, tm), 0)
    groups = [sel[g * pg:(g + 1) * pg, :] for g in range(n_groups)]
    sgroups = [scores[g * pg:(g + 1) * pg, :] for g in range(n_groups)]

    gs_rows = []
    for s_g in groups:
        m1 = jnp.max(s_g, axis=0, keepdims=True)
        first = jnp.min(jnp.where(s_g == m1, sub, pg), axis=0, keepdims=True)
        m2 = jnp.max(jnp.where(sub == first, neg, s_g), axis=0, keepdims=True)
        gs_rows.append(m1 + m2)
    gs = jnp.concatenate(gs_rows, axis=0)
    gidx = lax.broadcasted_iota(jnp.int32, (n_groups, tm), 0)
    ok = jnp.zeros((n_groups, tm), F32)
    for _ in range(topk_groups):
        m = jnp.max(gs, axis=0, keepdims=True)
        first = jnp.min(jnp.where(gs == m, gidx, n_groups), axis=0, keepdims=True)
        hit = gidx == first
        ok = jnp.where(hit, 1.0, ok)
        gs = jnp.where(hit, neg, gs)

    cur = [jnp.where(ok[g:g + 1, :] > 0.0, groups[g], neg) for g in range(n_groups)]
    eidx = [sub + g * pg for g in range(n_groups)]
    chosen = [jnp.zeros((pg, tm), F32) for _ in range(n_groups)]
    e_rows, s_rows = [], []
    for _ in range(top_k):
        m = jnp.max(functools.reduce(jnp.maximum, cur), axis=0, keepdims=True)
        cand = functools.reduce(jnp.minimum, [jnp.where(cur[g] == m, eidx[g], e) for g in range(n_groups)])
        first = jnp.min(cand, axis=0, keepdims=True)
        picked = jnp.zeros((pg, tm), F32)
        for g in range(n_groups):
            hit = eidx[g] == first
            picked = picked + jnp.where(hit, sgroups[g], 0.0)
            chosen[g] = jnp.where(hit, 1.0, chosen[g])
            cur[g] = jnp.where(hit, neg, cur[g])
        e_rows.append(first)
        s_rows.append(jnp.sum(picked, axis=0, keepdims=True))
    w = jnp.concatenate(s_rows, axis=0)
    te_ref[...] = jnp.concatenate(e_rows, axis=0)
    w_ref[...] = w / jnp.sum(w, axis=0, keepdims=True) * ROUTED_SCALE

    sel_mask = jnp.concatenate(chosen, axis=0).astype(BF16)
    before = _dot(sel_mask, tri_ref[...])
    rk_rows = []
    for k in range(top_k):
        acc = jnp.zeros((pg, tm), F32)
        for g in range(n_groups):
            acc = acc + jnp.where(eidx[g] == e_rows[k], before[g * pg:(g + 1) * pg, :], 0.0)
        rk_rows.append(jnp.sum(acc, axis=0, keepdims=True))
    rk_ref[...] = jnp.concatenate(rk_rows, axis=0).astype(jnp.int32)
    cnt_ref[...] = _dot_nt(jnp.ones((8, tm), BF16), sel_mask)


N_ROUTE_IN = 5
N_ROUTE_OUT = 5


def _route_io(b, l, d, tm, mod_i, ctx_row, rwt, bias):
    e = rwt.shape[0]
    assert l % tm == 0 and e % N_EXPERT_GROUPS == 0
    nt = l // tm
    t = b * l
    row = lax.broadcasted_iota(jnp.int32, (tm, tm), 0)
    col = lax.broadcasted_iota(jnp.int32, (tm, tm), 1)
    tri = (row < col).astype(BF16)
    in_specs = [
        _mod_spec(3, d, ctx_row, 0),
        _mod_spec(4, d, ctx_row, 0),
        pl.BlockSpec((e, d), lambda bb, tt: (0, 0)),
        pl.BlockSpec((e, 1), lambda bb, tt: (0, 0)),
        pl.BlockSpec((tm, tm), lambda bb, tt: (0, 0)),
    ]
    args = [mod_i, mod_i, rwt, bias.reshape(e, 1).astype(F32), tri]
    tok_spec = pl.BlockSpec((TOP_K, tm), lambda bb, tt: (0, bb * nt + tt))
    out_specs = [
        pl.BlockSpec((None, tm, d // 2), lambda bb, tt: (bb, tt, 0)),
        tok_spec, tok_spec, tok_spec,
        pl.BlockSpec((None, 8, e), lambda bb, tt: (bb * nt + tt, 0, 0)),
    ]
    out_shape = [
        jax.ShapeDtypeStruct((b, l, d // 2), I32),
        jax.ShapeDtypeStruct((TOP_K, t), jnp.int32),
        jax.ShapeDtypeStruct((TOP_K, t), F32),
        jax.ShapeDtypeStruct((TOP_K, t), jnp.int32),
        jax.ShapeDtypeStruct((b * nt, 8, e), F32),
    ]
    return in_specs, args, out_specs, out_shape


def _expert_layout(cnt, tmb, n_blocks):
    cnt_i = cnt[:, 0, :].astype(jnp.int32)
    e = cnt_i.shape[1]
    counts = cnt_i.sum(axis=0)
    padded = (counts + tmb - 1) // tmb * tmb
    pend = jnp.cumsum(padded)
    base = (pend - padded)[None, :] + jnp.cumsum(cnt_i, axis=0) - cnt_i
    n_used = (pend[-1] // tmb).astype(jnp.int32)
    blk = jnp.arange(n_blocks, dtype=jnp.int32)
    block_e = jnp.sum((blk[:, None] * tmb >= pend[None, :]).astype(jnp.int32), axis=1)
    last_e = jnp.sum((jnp.maximum(n_used - 1, 0) * tmb >= pend).astype(jnp.int32))
    block_e = jnp.clip(jnp.where(blk < n_used, block_e, last_e), 0, e - 1).astype(jnp.int32)
    seg_end = (pend - padded + counts)[block_e]
    block_valid = jnp.clip(seg_end - blk * tmb, 0, tmb).astype(jnp.int32)
    return base.reshape(-1).astype(jnp.int32), block_e, block_valid, n_used.reshape(1)


def _pos_kernel(base_ref, te_ref, rk_ref, pos_ref, *, n_experts):
    i = pl.program_id(0)
    te = te_ref[...]
    pos = rk_ref[...]
    for e in range(n_experts):
        pos = pos + jnp.where(te == e, base_ref[i * n_experts + e], 0)
    pos_ref[...] = pos


def _positions(base, top_e, rank, n_experts, tm):
    k, t = top_e.shape
    spec = pl.BlockSpec((k, tm), lambda i, base_ref: (0, i))
    return pl.pallas_call(
        functools.partial(_pos_kernel, n_experts=n_experts),
        grid_spec=pltpu.PrefetchScalarGridSpec(
            num_scalar_prefetch=1, grid=(t // tm,), in_specs=[spec, spec], out_specs=spec),
        out_shape=jax.ShapeDtypeStruct((k, t), jnp.int32),
        compiler_params=_cparams(1),
        name="moe_positions",
    )(base, top_e, rank)


def _sc_dispatch(rows, pos3, cap):
    t, w = rows.shape
    n_chunks, top_k, n = pos3.shape
    n_workers = SC_CORES * SC_SUBCORES
    assert n_chunks * n == t and n <= LANES
    per_worker = -(-n_chunks // n_workers)
    mesh = plsc.VectorSubcoreMesh(core_axis_name="core", subcore_axis_name="subcore",
                                  num_cores=SC_CORES, num_subcores=SC_SUBCORES)

    def body(rows_hbm, pos_hbm, out_hbm, idx_v, rows_v, sem):
        wid = lax.axis_index("subcore") * SC_CORES + lax.axis_index("core")

        @pl.loop(0, per_worker)
        def _(j):
            c = wid * per_worker + j

            @pl.when(c < n_chunks)
            def _():
                pltpu.sync_copy(pos_hbm.at[c], idx_v)
                pltpu.sync_copy(rows_hbm.at[pl.ds(c * n, n)], rows_v)
                copies = [pltpu.async_copy(rows_v, out_hbm.at[idx_v.at[k]], sem) for k in range(top_k)]
                for cp in copies:
                    cp.wait()

    return pl.kernel(
        body,
        out_type=jax.ShapeDtypeStruct((cap, w), I32),
        mesh=mesh,
        scratch_types=[pltpu.VMEM((top_k, n), I32), pltpu.VMEM((n, w), I32), pltpu.SemaphoreType.DMA],
        name="moe_dispatch_sc",
    )(rows, pos3)


def _sc_combine_gather(rows, pos3):
    cap, w = rows.shape
    n_chunks, top_k, n = pos3.shape
    t = n_chunks * n
    n_workers = SC_CORES * SC_SUBCORES
    assert n <= LANES
    per_worker = -(-n_chunks // n_workers)
    mesh = plsc.VectorSubcoreMesh(core_axis_name="core", subcore_axis_name="subcore",
                                  num_cores=SC_CORES, num_subcores=SC_SUBCORES)

    def body(rows_hbm, pos_hbm, out_hbm, idx_v, buf0, buf1, gsem0, gsem1, wsem0, wsem1):
        wid = lax.axis_index("subcore") * SC_CORES + lax.axis_index("core")
        bufs, gsems, wsems = (buf0, buf1), (gsem0, gsem1), (wsem0, wsem1)

        @pl.loop(0, per_worker)
        def _(j):
            c = wid * per_worker + j

            @pl.when(c < n_chunks)
            def _():
                pltpu.sync_copy(pos_hbm.at[c], idx_v)
                gathers = [None] * top_k
                writes = [None] * top_k
                gathers[0] = pltpu.async_copy(rows_hbm.at[idx_v.at[0]], bufs[0], gsems[0])
                for k in range(top_k):
                    if k + 1 < top_k:
                        if k >= 1:
                            writes[k - 1].wait()
                        gathers[k + 1] = pltpu.async_copy(rows_hbm.at[idx_v.at[k + 1]], bufs[(k + 1) % 2],
                                                          gsems[(k + 1) % 2])
                    gathers[k].wait()
                    writes[k] = pltpu.async_copy(bufs[k % 2], out_hbm.at[k, pl.ds(c * n, n)], wsems[k % 2])
                writes[top_k - 2].wait()
                writes[top_k - 1].wait()

    return pl.kernel(
        body,
        out_type=jax.ShapeDtypeStruct((top_k, t, w), I32),
        mesh=mesh,
        scratch_types=[pltpu.VMEM((top_k, n), I32), pltpu.VMEM((n, w), I32), pltpu.VMEM((n, w), I32),
                       pltpu.SemaphoreType.DMA, pltpu.SemaphoreType.DMA,
                       pltpu.SemaphoreType.DMA, pltpu.SemaphoreType.DMA],
        name="moe_combine_sc",
    )(rows, pos3)


def _gmm_kernel(be_ref, bv_ref, nu_ref, x_ref, *refs, tmb, per_step):
    w_refs, o_ref = refs[:-1], refs[-1]
    i = pl.program_id(0)

    @pl.when(i * per_step < nu_ref[0])
    def _():
        for s in range(per_step):
            wg_ref, wu_ref, wd_ref = w_refs[3 * s:3 * s + 3]
            words = x_ref[s * tmb:(s + 1) * tmb, :]
            half = words.shape[1]
            live = lax.broadcasted_iota(I32, (tmb, half), 0) < bv_ref[i * per_step + s]
            xa, xb = _unpack_halves(jnp.where(live, words, 0))
            g = _dot(xa, wg_ref[:half, :]) + _dot(xb, wg_ref[half:, :])
            u = _dot(xa, wu_ref[:half, :]) + _dot(xb, wu_ref[half:, :])
            o_ref[s * tmb:(s + 1) * tmb, :] = _pack_halves(_dot((_silu(g) * u).astype(BF16), wd_ref[...]))


def _grouped_experts(x_sorted, block_e, block_valid, n_used, wg, wu, wd, tmb):
    cap, half = x_sorted.shape
    d = 2 * half
    ff = wg.shape[2]
    n_blocks = cap // tmb
    per_step = MOE_BLOCKS_PER_STEP if n_blocks % MOE_BLOCKS_PER_STEP == 0 else 1
    n_steps = n_blocks // per_step

    def row_map(i, be, bv, nu):
        return (jnp.minimum(i, (nu[0] + per_step - 1) // per_step - 1), 0)

    w_specs = []
    for s in range(per_step):
        w_map = functools.partial(lambda i, be, bv, nu, s: (be[i * per_step + s], 0, 0), s=s)
        w_specs += [pl.BlockSpec((None, d, ff), w_map), pl.BlockSpec((None, d, ff), w_map),
                    pl.BlockSpec((None, ff, d), w_map)]

    return pl.pallas_call(
        functools.partial(_gmm_kernel, tmb=tmb, per_step=per_step),
        grid_spec=pltpu.PrefetchScalarGridSpec(
            num_scalar_prefetch=3,
            grid=(n_steps,),
            in_specs=[pl.BlockSpec((per_step * tmb, half), row_map)] + w_specs,
            out_specs=pl.BlockSpec((per_step * tmb, half), row_map),
        ),
        out_shape=jax.ShapeDtypeStruct((cap, half), I32),
        compiler_params=_cparams(1),
        name="moe_experts",
    )(block_e, block_valid, n_used, x_sorted, *([wg, wu, wd] * per_step))


def _moe_out_kernel(u_ref, w_ref, *refs, alpha, top_k):
    y_refs = refs[:top_k]
    sg_ref, su_ref, sd_ref, x_ref, gate_ref, g_ref, b_ref, o_ref = refs[top_k:]
    ua, ub = _unpack_halves(u_ref[...])
    half = ua.shape[1]
    g = _dot(ua, sg_ref[:half, :]) + _dot(ub, sg_ref[half:, :])
    s = _dot(ua, su_ref[:half, :]) + _dot(ub, su_ref[half:, :])
    y = _dot((_silu(g) * s).astype(BF16), sd_ref[...])
    w = w_ref[...]
    ya = y[:, :half]
    yb = y[:, half:]
    for k in range(top_k):
        words = y_refs[k][...]
        wk = w[:, k:k + 1]
        ya = ya + wk * pltpu.bitcast(words & jnp.int32(-65536), F32)
        yb = yb + wk * pltpu.bitcast(lax.shift_left(words, 16), F32)
    x = x_ref[...]
    gate = gate_ref[...]
    za = alpha * x[:, :half] + gate[:, :half] * ya
    zb = alpha * x[:, half:] + gate[:, half:] * yb
    z = jnp.concatenate([za, zb], axis=1)
    o_ref[...] = _normalize(z, LN_EPS) * g_ref[...] + b_ref[...]


def _moe_out(u2, wts, ys, sg, su, sd, x, mod_i, ctx_row, ln_g, ln_b, alpha):
    b, l, d = x.shape
    ff = sg.shape[1]
    top_k = ys.shape[0]
    tm = min(ROW_TILE, l)
    assert l % tm == 0
    tok = pl.BlockSpec((None, tm, d), lambda bb, t: (bb, t, 0))
    words = pl.BlockSpec((None, tm, d // 2), lambda bb, t: (bb, t, 0))
    picks = [pl.BlockSpec((None, None, tm, d // 2), functools.partial(lambda bb, t, k: (k, bb, t, 0), k=k))
             for k in range(top_k)]
    return pl.pallas_call(
        functools.partial(_moe_out_kernel, alpha=alpha, top_k=top_k),
        grid=(b, l // tm),
        in_specs=[words, pl.BlockSpec((None, tm, top_k), lambda bb, t: (bb, t, 0))] + picks + [
            pl.BlockSpec((d, ff), lambda bb, t: (0, 0)),
            pl.BlockSpec((d, ff), lambda bb, t: (0, 0)),
            pl.BlockSpec((ff, d), lambda bb, t: (0, 0)),
            tok,
            _mod_spec(5, d, ctx_row, 0),
            _row_spec(d),
            _row_spec(d),
        ],
        out_specs=tok,
        out_shape=jax.ShapeDtypeStruct((b, l, d), F32),
        compiler_params=_cparams(2),
        name="moe_shared_post",
    )(u2, wts, *([ys] * top_k), sg, su, sd, x, mod_i, ln_g, ln_b)


def _moe_layer(x, routing, mod_i, ctx_row, rwt, rbias, wg, wu, wd, sg, su, sd, ln_g, ln_b, alpha):
    b, l, d = x.shape
    t = b * l
    e = rwt.shape[0]
    tmb = MOE_ROW_TILE
    n_blocks = -(-(t * TOP_K) // tmb) + e
    u2, top_e, wts, rank, cnt = routing
    base, block_e, block_valid, n_used = _expert_layout(cnt, tmb, n_blocks)
    pos = _positions(base, top_e, rank, e, min(ROW_TILE, l))
    pos_d = pos.reshape(TOP_K, t // DISPATCH_CHUNK, DISPATCH_CHUNK).transpose(1, 0, 2)
    pos_c = pos.reshape(TOP_K, t // COMBINE_CHUNK, COMBINE_CHUNK).transpose(1, 0, 2)
    x_sorted = _sc_dispatch(u2.reshape(t, d // 2), pos_d, n_blocks * tmb)
    y_sorted = _grouped_experts(x_sorted, block_e, block_valid, n_used, wg, wu, wd, tmb)
    ys = _sc_combine_gather(y_sorted, pos_c).reshape(TOP_K, b, l, d // 2)
    return _moe_out(u2, wts.T.reshape(b, l, TOP_K), ys, sg, su, sd, x, mod_i, ctx_row, ln_g, ln_b, alpha)


def _rope_tables(l, hd, n_q_heads, n_k_heads, rope):
    axis_rot = hd // 2
    qscale = hd ** -0.5
    if rope:
        t = jnp.arange(l, dtype=jnp.int32)
        r = (t // GRID_W).astype(F32)
        col = (t % GRID_W).astype(F32)
        inv = ROPE_BASE ** (-jnp.arange(0, axis_rot, 2, dtype=F32) / axis_rot)
        ar = r[:, None] * inv
        ac = col[:, None] * inv
        ang = jnp.concatenate([ar, ar, ac, ac], axis=-1)
        cos, sin = jnp.cos(ang), jnp.sin(ang)
    else:
        cos, sin = jnp.ones((l, hd), F32), jnp.zeros((l, hd), F32)
    cos_t = jnp.concatenate([jnp.tile(cos, (1, n_q_heads)) * qscale, jnp.tile(cos, (1, n_k_heads))], axis=1)
    sin_t = jnp.concatenate([jnp.tile(sin, (1, n_q_heads)) * qscale, jnp.tile(sin, (1, n_k_heads))], axis=1)
    return cos_t, sin_t


def _rot_columns(w, hd):
    d, n = w.shape
    q = hd // 4
    w4 = w.reshape(d, n // (2 * q), 2, q)
    return jnp.stack([-w4[:, :, 1, :], w4[:, :, 0, :]], axis=2).reshape(d, n)


def kernel(x, c, ctx, c_ctx, w_ada, b_ada, ln_g, ln_b, attn_w_qkv, attn_w_o, attn_sinks, fnet_w,
           router_w, router_bias, exp_w_gate, exp_w_up, exp_w_down, sh_w_gate, sh_w_up, sh_w_down):
    b, l, d = x.shape
    cl = ctx.shape[1]
    depth = w_ada.shape[0]
    n_heads = attn_sinks.shape[1]
    hd = d // n_heads
    qw = n_heads * hd
    kvw = N_KV_HEADS * hd
    alpha = (2.0 * depth) ** 0.25
    gc = d // F_GROUPS

    mp = -(-(b + 1) // 16) * 16
    cvec = jnp.concatenate([c, c_ctx[None, :], jnp.zeros((mp - b - 1, d), F32)], axis=0)
    mod = _ada_all(cvec, w_ada, b_ada).reshape(depth, mp, 6, 1, d)

    def run_stream(x, h, mod):
        b = x.shape[0]
        for i in range(depth):
            kind = i % N_MIXERS
            j = i // N_MIXERS
            update_ctx = any((m % N_MIXERS) == 0 for m in range(i + 1, depth))
            mod_i = mod[i]
            g1 = ln_g[i, 0].reshape(1, d)
            b1 = ln_b[i, 0].reshape(1, d)
            g2 = ln_g[i, 1].reshape(1, d)
            b2 = ln_b[i, 1].reshape(1, d)
            rwt, rbias = moe_w[i][:2]

            if kind == 0:
                w_all, w_o = attn_w[j]
                q, k, v = _qkv_proj(x, mod_i, None, w_all, cos_l, sin_l, qw, kvw)
                q_c, k_c, v_c = _qkv_proj(h, mod_i, b, w_all, cos_c, sin_c, qw, kvw)
                o = _attention(q, k, v, k_c, v_c, attn_sinks[j], True)
                x, x_route = _proj_post(o, w_o, x, mod_i, 2, None, g1, b1, alpha, rwt, rbias)
                if update_ctx:
                    o_c = _attention(q_c, None, None, k_c, v_c, attn_sinks[j], False)
                    h, h_route = _proj_post(o_c, w_o, h, mod_i, 2, b, g1, b1, alpha, rwt, rbias)
            else:
                streams = [(x, None)] + ([(h, b)] if update_ctx else [])
                outs = []
                for s, ctx_row in streams:
                    c_l, s_l = dft_seq[s.shape[1]]
                    ab = _fnet_a(s, mod_i, ctx_row, cs)
                    outs.append(_fnet_b(c_l, s_l, ab, fnet_wb[j], s, mod_i, 2, ctx_row, g1, b1, alpha, rwt, rbias))
                x, x_route = outs[0]
                if update_ctx:
                    h, h_route = outs[1]

            x = _moe_layer(x, x_route, mod_i, None, *moe_w[i], g2, b2, alpha)
            if update_ctx:
                h = _moe_layer(h, h_route, mod_i, b, *moe_w[i], g2, b2, alpha)
        return x

    attn_w = []
    for j in range(attn_w_qkv.shape[0]):
        w = attn_w_qkv[j]
        w_all = jnp.concatenate([w, _rot_columns(w[:, :qw + kvw], hd)], axis=1).astype(BF16)
        attn_w.append((w_all, attn_w_o[j].astype(BF16)))
    cos_l, sin_l = _rope_tables(l, hd, n_heads, N_KV_HEADS, True)
    cos_c, sin_c = _rope_tables(cl, hd, n_heads, N_KV_HEADS, False)
    fnet_wb = [fnet_w[j].astype(BF16) for j in range(fnet_w.shape[0])]
    cc, sc = _dft_tables(gc, gc ** -0.5)
    cs = jnp.concatenate([cc, sc], axis=1).astype(BF16)
    dft_seq = {}
    for ls in (l, cl):
        c_l, s_l = _dft_tables(ls, ls ** -0.5)
        dft_seq[ls] = (c_l.astype(BF16), (-s_l).astype(BF16))
    moe_w = [(router_w[i].T, router_bias[i], exp_w_gate[i].astype(BF16), exp_w_up[i].astype(BF16),
              exp_w_down[i].astype(BF16), sh_w_gate[i].astype(BF16), sh_w_up[i].astype(BF16),
              sh_w_down[i].astype(BF16)) for i in range(depth)]

    n_streams = N_STREAMS if b % N_STREAMS == 0 else 1
    bs = b // n_streams
    outs = []
    for s in range(n_streams):
        mod_s = jnp.concatenate([mod[:, s * bs:(s + 1) * bs], mod[:, b:b + 1]], axis=1)
        outs.append(run_stream(x[s * bs:(s + 1) * bs], ctx[s * bs:(s + 1) * bs], mod_s))
    return outs[0] if n_streams == 1 else jnp.concatenate(outs, axis=0)
```

```python
import functools
import math

import jax
import jax.numpy as jnp
from jax import lax
from jax.experimental import pallas as pl
from jax.experimental.pallas import tpu as pltpu
from jax.experimental.pallas import tpu_sc as plsc

F32 = jnp.float32
BF16 = jnp.bfloat16
I32 = jnp.int32

N_KV_HEADS = 4
WINDOW = 128
GRID_W = 64
ROPE_BASE = 10000.0
F_GROUPS = 4
TOP_K = 8
N_EXPERT_GROUPS = 8
TOPK_GROUPS = 4
ROUTED_SCALE = 2.5
N_MIXERS = 2
LN_EPS = 1e-5
MOD_EPS = 1e-6

LANES = 128
VMEM_LIMIT_BYTES = 52 * 1024 * 1024
ROW_TILE = 512
QKV_ROW_TILE = 256
Q_TILE = 128
ATTN_ROW_CHUNK = 32
ATTN_UNROLL = 16
MOE_ROW_TILE = 512
MOE_BLOCKS_PER_STEP = 2
DISPATCH_CHUNK = 128
COMBINE_CHUNK = 16
N_STREAMS = 2
SC_CORES = 2
SC_SUBCORES = 16
SC_LANES = 16
NEG_BIG = -1e30


def _cparams(n_axes):
    return pltpu.CompilerParams(dimension_semantics=("arbitrary",) * n_axes,
                                vmem_limit_bytes=VMEM_LIMIT_BYTES)


def _dot(a, b):
    return jnp.dot(a, b, preferred_element_type=F32)


def _dot_nt(a, b):
    return lax.dot_general(a, b, (((1,), (1,)), ((), ())), preferred_element_type=F32)


def _split_bf16(a):
    hi = a.astype(BF16)
    lo = (a - hi.astype(F32)).astype(BF16)
    return hi, lo


def _normalize(x, eps):
    mu = jnp.mean(x, axis=-1, keepdims=True)
    xc = x - mu
    var = jnp.mean(xc * xc, axis=-1, keepdims=True)
    return xc * lax.rsqrt(var + eps)


def _modulate(x, shift, scale):
    return _normalize(x, MOD_EPS) * (1.0 + scale) + shift


def _silu(x):
    return x * jax.nn.sigmoid(x)


def _pack_halves(x):
    n = x.shape[1] // 2
    r = x.astype(BF16).astype(F32)
    hi = pltpu.bitcast(r[:, :n], I32)
    lo = pltpu.bitcast(r[:, n:], I32)
    return hi | lax.shift_right_logical(lo, 16)


def _unpack_halves(w):
    a = pltpu.bitcast(w & jnp.int32(-65536), F32).astype(BF16)
    b = pltpu.bitcast(lax.shift_left(w, 16), F32).astype(BF16)
    return a, b


def _ada_kernel(c_ref, w_ref, b_ref, o_ref):
    s = _silu(c_ref[...])
    sh, sl = _split_bf16(s)
    wh, wl = _split_bf16(w_ref[...])
    o_ref[...] = _dot(sh, wh) + _dot(sl, wh) + _dot(sh, wl) + b_ref[...]


def _ada_all(cvec, w_ada, b_ada):
    depth, d, n = w_ada.shape
    mp = cvec.shape[0]
    tn = 1536
    assert n % tn == 0
    return pl.pallas_call(
        _ada_kernel,
        grid=(depth, n // tn),
        in_specs=[
            pl.BlockSpec((mp, d), lambda i, j: (0, 0)),
            pl.BlockSpec((None, d, tn), lambda i, j: (i, 0, j)),
            pl.BlockSpec((None, 1, tn), lambda i, j: (i, 0, j)),
        ],
        out_specs=pl.BlockSpec((None, mp, tn), lambda i, j: (i, 0, j)),
        out_shape=jax.ShapeDtypeStruct((depth, mp, n), F32),
        compiler_params=_cparams(2),
        name="ada_mod",
    )(cvec, w_ada, b_ada.reshape(depth, 1, n))


def _mod_spec(j, d, ctx_row, batch_axis):
    if ctx_row is None:
        return pl.BlockSpec((None, None, 1, d), lambda *g: (g[batch_axis], j, 0, 0))
    return pl.BlockSpec((None, None, 1, d), lambda *g: (ctx_row, j, 0, 0))


def _row_spec(d):
    return pl.BlockSpec((1, d), lambda *g: (0, 0))


def _qkv_kernel(x_ref, sh_ref, sc_ref, w_ref, cos_ref, sin_ref, q_ref, k_ref, v_ref, *, qw, kvw):
    u = _modulate(x_ref[...], sh_ref[...], sc_ref[...]).astype(BF16)
    r = _dot(u, w_ref[...])
    qk = r[:, :qw + kvw] * cos_ref[...] + r[:, qw + 2 * kvw:] * sin_ref[...]
    q_ref[...] = qk[:, :qw].astype(BF16)
    k_ref[...] = qk[:, qw:].astype(BF16)
    v_ref[...] = r[:, qw + kvw:qw + 2 * kvw].astype(BF16)


def _qkv_proj(x, mod_i, ctx_row, w_all, cos_t, sin_t, qw, kvw):
    b, l, d = x.shape
    tm = min(QKV_ROW_TILE, l)
    assert l % tm == 0
    n_all = w_all.shape[1]
    return pl.pallas_call(
        functools.partial(_qkv_kernel, qw=qw, kvw=kvw),
        grid=(l // tm, b),
        in_specs=[
            pl.BlockSpec((None, tm, d), lambda t, bb: (bb, t, 0)),
            _mod_spec(0, d, ctx_row, 1),
            _mod_spec(1, d, ctx_row, 1),
            pl.BlockSpec((d, n_all), lambda t, bb: (0, 0)),
            pl.BlockSpec((tm, qw + kvw), lambda t, bb: (t, 0)),
            pl.BlockSpec((tm, qw + kvw), lambda t, bb: (t, 0)),
        ],
        out_specs=[
            pl.BlockSpec((None, tm, qw), lambda t, bb: (bb, t, 0)),
            pl.BlockSpec((None, tm, kvw), lambda t, bb: (bb, t, 0)),
            pl.BlockSpec((None, tm, kvw), lambda t, bb: (bb, t, 0)),
        ],
        out_shape=[
            jax.ShapeDtypeStruct((b, l, qw), BF16),
            jax.ShapeDtypeStruct((b, l, kvw), BF16),
            jax.ShapeDtypeStruct((b, l, kvw), BF16),
        ],
        compiler_params=_cparams(2),
        name="qkv_rope",
    )(x, mod_i, mod_i, w_all, cos_t, sin_t)


def _attn_kernel(sink_ref, q_ref, *refs, tq, seq, n_kv, group, hd, has_window):
    if has_window:
        k_ref, v_ref, kc_ref, vc_ref, o_ref, s_scr, p_scr, m_scr, bias_scr = refs
    else:
        kc_ref, vc_ref, o_ref, s_scr, p_scr, m_scr = refs
    q = q_ref[...]
    kc = kc_ref[...]
    vc = vc_ref[...]
    span = tq + 2 * WINDOW if has_window else 0
    rows = group * tq
    if has_window:
        q0 = pl.program_id(1) * tq
        start = pl.multiple_of(jnp.clip(q0 - WINDOW, 0, seq - span), LANES)
        kw = k_ref[pl.ds(start, span), :]
        vw = v_ref[pl.ds(start, span), :]
        qpos = q0 + lax.broadcasted_iota(jnp.int32, (tq, span), 0)
        kpos = start + lax.broadcasted_iota(jnp.int32, (tq, span), 1)
        bias_scr[...] = jnp.where(jnp.abs(qpos - kpos) <= WINDOW, 0.0, NEG_BIG)

    for h in range(n_kv):
        heads = [h * group + g for g in range(group)]
        qh = jnp.concatenate([q[:, j * hd:(j + 1) * hd] for j in heads], axis=0)
        if has_window:
            s_scr[h, :, :span] = _dot_nt(qh, kw[:, h * hd:(h + 1) * hd])
        s_scr[h, :, span:] = _dot_nt(qh, kc[:, h * hd:(h + 1) * hd])

    chunks_per_head = tq // ATTN_ROW_CHUNK
    n_chunks = rows // ATTN_ROW_CHUNK
    n_tiles = (span + kc.shape[0]) // LANES
    win_tiles = span // LANES

    def logit_tiles(h, r):
        row = pl.multiple_of(r * ATTN_ROW_CHUNK, ATTN_ROW_CHUNK)
        s = s_scr[h, pl.ds(row, ATTN_ROW_CHUNK), :]
        tiles = [s[:, i * LANES:(i + 1) * LANES] for i in range(n_tiles)]
        if has_window:
            brow = pl.multiple_of((r % chunks_per_head) * ATTN_ROW_CHUNK, ATTN_ROW_CHUNK)
            bias = bias_scr[pl.ds(brow, ATTN_ROW_CHUNK), :]
            tiles = [t + bias[:, i * LANES:(i + 1) * LANES] if i < win_tiles else t for i, t in enumerate(tiles)]
        return row, tiles

    for h in range(n_kv):
        def row_max(r, carry, h=h):
            row, tiles = logit_tiles(h, r)
            sink = sink_ref[h * group + r // chunks_per_head]
            m = jnp.max(functools.reduce(jnp.maximum, tiles), axis=-1, keepdims=True)
            m_scr[h, pl.ds(row, ATTN_ROW_CHUNK), :] = jnp.broadcast_to(jnp.maximum(m, sink),
                                                                     (ATTN_ROW_CHUNK, LANES))
            return carry

        lax.fori_loop(0, n_chunks, row_max, 0, unroll=ATTN_UNROLL)

    for h in range(n_kv):
        def probs(r, carry, h=h):
            row, tiles = logit_tiles(h, r)
            sink = sink_ref[h * group + r // chunks_per_head]
            m = m_scr[h, pl.ds(row, ATTN_ROW_CHUNK), :]
            es = [jnp.exp(t - m) for t in tiles]
            for i, e in enumerate(es):
                p_scr[h, pl.ds(row, ATTN_ROW_CHUNK), i * LANES:(i + 1) * LANES] = e.astype(BF16)
            den = jnp.sum(functools.reduce(jnp.add, es), axis=-1, keepdims=True) + jnp.exp(sink - m)
            m_scr[h, pl.ds(row, ATTN_ROW_CHUNK), :] = 1.0 / den
            return carry

        lax.fori_loop(0, n_chunks, probs, 0, unroll=ATTN_UNROLL)

    for h in range(n_kv):
        o = _dot(p_scr[h, :, span:], vc[:, h * hd:(h + 1) * hd])
        if has_window:
            o = o + _dot(p_scr[h, :, :span], vw[:, h * hd:(h + 1) * hd])
        o = o * m_scr[h, :, :hd]
        for g in range(group):
            j = h * group + g
            o_ref[:, j * hd:(j + 1) * hd] = o[g * tq:(g + 1) * tq, :].astype(o_ref.dtype)


def _attention(q, k, v, kc, vc, sinks, has_window):
    b, l, qw = q.shape
    c, kvw = kc.shape[1], kc.shape[2]
    hd = kvw // N_KV_HEADS
    group = qw // kvw
    tq = Q_TILE if has_window else l
    assert l % tq == 0
    if has_window:
        assert l >= tq + 2 * WINDOW
    kern = functools.partial(_attn_kernel, tq=tq, seq=l, n_kv=N_KV_HEADS, group=group, hd=hd,
                             has_window=has_window)
    in_specs = [pl.BlockSpec(memory_space=pltpu.SMEM),
                pl.BlockSpec((None, tq, qw), lambda bb, t: (bb, t, 0))]
    args = [sinks, q]
    if has_window:
        in_specs += [pl.BlockSpec((None, l, kvw), lambda bb, t: (bb, 0, 0)),
                     pl.BlockSpec((None, l, kvw), lambda bb, t: (bb, 0, 0))]
        args += [k, v]
    in_specs += [pl.BlockSpec((None, c, kvw), lambda bb, t: (bb, 0, 0)),
                 pl.BlockSpec((None, c, kvw), lambda bb, t: (bb, 0, 0))]
    args += [kc, vc]
    n_keys = c + (tq + 2 * WINDOW if has_window else 0)
    scratch = [pltpu.VMEM((N_KV_HEADS, group * tq, n_keys), F32),
               pltpu.VMEM((N_KV_HEADS, group * tq, n_keys), BF16),
               pltpu.VMEM((N_KV_HEADS, group * tq, LANES), F32)]
    if has_window:
        scratch.append(pltpu.VMEM((tq, tq + 2 * WINDOW), F32))
    return pl.pallas_call(
        kern,
        grid=(b, l // tq),
        in_specs=in_specs,
        out_specs=pl.BlockSpec((None, tq, qw), lambda bb, t: (bb, t, 0)),
        out_shape=jax.ShapeDtypeStruct((b, l, qw), BF16),
        scratch_shapes=scratch,
        compiler_params=_cparams(2),
        name="win_attn" if has_window else "ctx_attn",
    )(*args)


def _proj_post_kernel(a_ref, w_ref, x_ref, gate_ref, g_ref, b_ref, *refs, alpha):
    route_in, o_ref, route_out = refs[:N_ROUTE_IN], refs[N_ROUTE_IN], refs[N_ROUTE_IN + 1:]
    y = _dot(a_ref[...], w_ref[...])
    z = alpha * x_ref[...] + gate_ref[...] * y
    x_new = _normalize(z, LN_EPS) * g_ref[...] + b_ref[...]
    o_ref[...] = x_new
    _route_tokens(x_new, *route_in, *route_out)


def _proj_post(a, w, x, mod_i, gate_j, ctx_row, ln_g, ln_b, alpha, rwt, rbias):
    b, l, d = x.shape
    ka = a.shape[2]
    tm = min(ROW_TILE, l)
    assert l % tm == 0
    r_in, r_args, r_out, r_shape = _route_io(b, l, d, tm, mod_i, ctx_row, rwt, rbias)
    res = pl.pallas_call(
        functools.partial(_proj_post_kernel, alpha=alpha),
        grid=(b, l // tm),
        in_specs=[
            pl.BlockSpec((None, tm, ka), lambda bb, t: (bb, t, 0)),
            pl.BlockSpec((ka, d), lambda bb, t: (0, 0)),
            pl.BlockSpec((None, tm, d), lambda bb, t: (bb, t, 0)),
            _mod_spec(gate_j, d, ctx_row, 0),
            _row_spec(d),
            _row_spec(d),
        ] + r_in,
        out_specs=[pl.BlockSpec((None, tm, d), lambda bb, t: (bb, t, 0))] + r_out,
        out_shape=[jax.ShapeDtypeStruct((b, l, d), F32)] + r_shape,
        compiler_params=_cparams(2),
        name="proj_post_route",
    )(a, w, x, mod_i, ln_g, ln_b, *r_args)
    return res[0], res[1:]


def _fnet_a_kernel(x_ref, sh_ref, sc_ref, cs_ref, o_ref, *, d, gc):
    u = _modulate(x_ref[...], sh_ref[...], sc_ref[...]).astype(BF16)
    cs = cs_ref[...]
    for g in range(d // gc):
        r = _dot(u[:, g * gc:(g + 1) * gc], cs)
        o_ref[:, g * gc:(g + 1) * gc] = r[:, :gc].astype(BF16)
        o_ref[:, d + g * gc:d + (g + 1) * gc] = r[:, gc:].astype(BF16)


def _fnet_a(x, mod_i, ctx_row, cs):
    b, l, d = x.shape
    gc = d // F_GROUPS
    tm = min(ROW_TILE, l)
    assert l % tm == 0
    return pl.pallas_call(
        functools.partial(_fnet_a_kernel, d=d, gc=gc),
        grid=(b, l // tm),
        in_specs=[
            pl.BlockSpec((None, tm, d), lambda bb, t: (bb, t, 0)),
            _mod_spec(0, d, ctx_row, 0),
            _mod_spec(1, d, ctx_row, 0),
            pl.BlockSpec((gc, 2 * gc), lambda bb, t: (0, 0)),
        ],
        out_specs=pl.BlockSpec((None, tm, 2 * d), lambda bb, t: (bb, t, 0)),
        out_shape=jax.ShapeDtypeStruct((b, l, 2 * d), BF16),
        compiler_params=_cparams(2),
        name="fnet_chan_dft",
    )(x, mod_i, mod_i, cs)


def _fnet_b_kernel(cl_ref, sl_ref, ab_ref, wf_ref, x_ref, gate_ref, g_ref, b_ref, *refs, d, alpha):
    route_in, o_ref, route_out = refs[:N_ROUTE_IN], refs[N_ROUTE_IN], refs[N_ROUTE_IN + 1:]
    f = _dot(cl_ref[...], ab_ref[:, :d]) + _dot(sl_ref[...], ab_ref[:, d:])
    y = _dot(f.astype(BF16), wf_ref[...])
    z = alpha * x_ref[...] + gate_ref[...] * y
    x_new = _normalize(z, LN_EPS) * g_ref[...] + b_ref[...]
    o_ref[...] = x_new
    _route_tokens(x_new, *route_in, *route_out)


def _fnet_b(cl, sl, ab, wf, x, mod_i, gate_j, ctx_row, ln_g, ln_b, alpha, rwt, rbias):
    b, l, d = x.shape
    tm = min(ROW_TILE, l)
    assert l % tm == 0
    r_in, r_args, r_out, r_shape = _route_io(b, l, d, tm, mod_i, ctx_row, rwt, rbias)
    res = pl.pallas_call(
        functools.partial(_fnet_b_kernel, d=d, alpha=alpha),
        grid=(b, l // tm),
        in_specs=[
            pl.BlockSpec((tm, l), lambda bb, t: (t, 0)),
            pl.BlockSpec((tm, l), lambda bb, t: (t, 0)),
            pl.BlockSpec((None, l, 2 * d), lambda bb, t: (bb, 0, 0)),
            pl.BlockSpec((d, d), lambda bb, t: (0, 0)),
            pl.BlockSpec((None, tm, d), lambda bb, t: (bb, t, 0)),
            _mod_spec(gate_j, d, ctx_row, 0),
            _row_spec(d),
            _row_spec(d),
        ] + r_in,
        out_specs=[pl.BlockSpec((None, tm, d), lambda bb, t: (bb, t, 0))] + r_out,
        out_shape=[jax.ShapeDtypeStruct((b, l, d), F32)] + r_shape,
        compiler_params=_cparams(2),
        name="fnet_seq_dft_route",
    )(cl, sl, ab, wf, x, mod_i, ln_g, ln_b, *r_args)
    return res[0], res[1:]


def _dft_tables(n, scale):
    j = jnp.arange(n, dtype=jnp.int32)
    ang = ((j[:, None] * j[None, :]) % n).astype(F32) * (2.0 * math.pi / n)
    return jnp.cos(ang) * scale, jnp.sin(ang) * scale


def _route_tokens(x, sh_ref, sc_ref, rwt_ref, bias_ref, tri_ref, u_ref, te_ref, w_ref, rk_ref, cnt_ref):
    n_groups, topk_groups, top_k = N_EXPERT_GROUPS, TOPK_GROUPS, TOP_K
    u = _modulate(x, sh_ref[...], sc_ref[...])
    uh, ul = _split_bf16(u)
    u_ref[...] = _pack_halves(u)
    wh, wl = _split_bf16(rwt_ref[...])
    logits = _dot_nt(wh, uh) + _dot_nt(wl, uh) + _dot_nt(wh, ul)
    e, tm = logits.shape
    pg = e // n_groups
    neg = -jnp.inf
    scores = jax.nn.sigmoid(logits)
    sel = scores + bias_ref[...]
    sub = lax.broadcasted_iota(jnp.int32, (pg, tm), 0)
    groups = [sel[g * pg:(g + 1) * pg, :] for g in range(n_groups)]
    sgroups = [scores[g * pg:(g + 1) * pg, :] for g in range(n_groups)]

    gs_rows = []
    for s_g in groups:
        m1 = jnp.max(s_g, axis=0, keepdims=True)
        first = jnp.min(jnp.where(s_g == m1, sub, pg), axis=0, keepdims=True)
        m2 = jnp.max(jnp.where(sub == first, neg, s_g), axis=0, keepdims=True)
        gs_rows.append(m1 + m2)
    gs = jnp.concatenate(gs_rows, axis=0)
    gidx = lax.broadcasted_iota(jnp.int32, (n_groups, tm), 0)
    ok = jnp.zeros((n_groups, tm), F32)
    for _ in range(topk_groups):
        m = jnp.max(gs, axis=0, keepdims=True)
        first = jnp.min(jnp.where(gs == m, gidx, n_groups), axis=0, keepdims=True)
        hit = gidx == first
        ok = jnp.where(hit, 1.0, ok)
        gs = jnp.where(hit, neg, gs)

    cur = [jnp.where(ok[g:g + 1, :] > 0.0, groups[g], neg) for g in range(n_groups)]
    eidx = [sub + g * pg for g in range(n_groups)]
    chosen = [jnp.zeros((pg, tm), F32) for _ in range(n_groups)]
    e_rows, s_rows = [], []
    for _ in range(top_k):
        m = jnp.max(functools.reduce(jnp.maximum, cur), axis=0, keepdims=True)
        cand = functools.reduce(jnp.minimum, [jnp.where(cur[g] == m, eidx[g], e) for g in range(n_groups)])
        first = jnp.min(cand, axis=0, keepdims=True)
        picked = jnp.zeros((pg, tm), F32)
        for g in range(n_groups):
            hit = eidx[g] == first
            picked = picked + jnp.where(hit, sgroups[g], 0.0)
            chosen[g] = jnp.where(hit, 1.0, chosen[g])
            cur[g] = jnp.where(hit, neg, cur[g])
        e_rows.append(first)
        s_rows.append(jnp.sum(picked, axis=0, keepdims=True))
    w = jnp.concatenate(s_rows, axis=0)
    te_ref[...] = jnp.concatenate(e_rows, axis=0)
    w_ref[...] = w / jnp.sum(w, axis=0, keepdims=True) * ROUTED_SCALE

    sel_mask = jnp.concatenate(chosen, axis=0).astype(BF16)
    before = _dot(sel_mask, tri_ref[...])
    rk_rows = []
    for k in range(top_k):
        acc = jnp.zeros((pg, tm), F32)
        for g in range(n_groups):
            acc = acc + jnp.where(eidx[g] == e_rows[k], before[g * pg:(g + 1) * pg, :], 0.0)
        rk_rows.append(jnp.sum(acc, axis=0, keepdims=True))
    rk_ref[...] = jnp.concatenate(rk_rows, axis=0).astype(jnp.int32)
    cnt_ref[...] = _dot_nt(jnp.ones((8, tm), BF16), sel_mask)


N_ROUTE_IN = 5
N_ROUTE_OUT = 5


def _route_io(b, l, d, tm, mod_i, ctx_row, rwt, bias):
    e = rwt.shape[0]
    assert l % tm == 0 and e % N_EXPERT_GROUPS == 0
    nt = l // tm
    t = b * l
    row = lax.broadcasted_iota(jnp.int32, (tm, tm), 0)
    col = lax.broadcasted_iota(jnp.int32, (tm, tm), 1)
    tri = (row < col).astype(BF16)
    in_specs = [
        _mod_spec(3, d, ctx_row, 0),
        _mod_spec(4, d, ctx_row, 0),
        pl.BlockSpec((e, d), lambda bb, tt: (0, 0)),
        pl.BlockSpec((e, 1), lambda bb, tt: (0, 0)),
        pl.BlockSpec((tm, tm), lambda bb, tt: (0, 0)),
    ]
    args = [mod_i, mod_i, rwt, bias.reshape(e, 1).astype(F32), tri]
    tok_spec = pl.BlockSpec((TOP_K, tm), lambda bb, tt: (0, bb * nt + tt))
    out_specs = [
        pl.BlockSpec((None, tm, d // 2), lambda bb, tt: (bb, tt, 0)),
        tok_spec, tok_spec, tok_spec,
        pl.BlockSpec((None, 8, e), lambda bb, tt: (bb * nt + tt, 0, 0)),
    ]
    out_shape = [
        jax.ShapeDtypeStruct((b, l, d // 2), I32),
        jax.ShapeDtypeStruct((TOP_K, t), jnp.int32),
        jax.ShapeDtypeStruct((TOP_K, t), F32),
        jax.ShapeDtypeStruct((TOP_K, t), jnp.int32),
        jax.ShapeDtypeStruct((b * nt, 8, e), F32),
    ]
    return in_specs, args, out_specs, out_shape


def _expert_layout(cnt, tmb, n_blocks):
    cnt_i = cnt[:, 0, :].astype(jnp.int32)
    e = cnt_i.shape[1]
    counts = cnt_i.sum(axis=0)
    padded = (counts + tmb - 1) // tmb * tmb
    pend = jnp.cumsum(padded)
    base = (pend - padded)[None, :] + jnp.cumsum(cnt_i, axis=0) - cnt_i
    n_used = (pend[-1] // tmb).astype(jnp.int32)
    blk = jnp.arange(n_blocks, dtype=jnp.int32)
    block_e = jnp.sum((blk[:, None] * tmb >= pend[None, :]).astype(jnp.int32), axis=1)
    last_e = jnp.sum((jnp.maximum(n_used - 1, 0) * tmb >= pend).astype(jnp.int32))
    block_e = jnp.clip(jnp.where(blk < n_used, block_e, last_e), 0, e - 1).astype(jnp.int32)
    seg_end = (pend - padded + counts)[block_e]
    block_valid = jnp.clip(seg_end - blk * tmb, 0, tmb).astype(jnp.int32)
    return base.reshape(-1).astype(jnp.int32), block_e, block_valid, n_used.reshape(1)


def _pos_kernel(base_ref, te_ref, rk_ref, pos_ref, *, n_experts):
    i = pl.program_id(0)
    te = te_ref[...]
    pos = rk_ref[...]
    for e in range(n_experts):
        pos = pos + jnp.where(te == e, base_ref[i * n_experts + e], 0)
    pos_ref[...] = pos


def _positions(base, top_e, rank, n_experts, tm):
    k, t = top_e.shape
    spec = pl.BlockSpec((k, tm), lambda i, base_ref: (0, i))
    return pl.pallas_call(
        functools.partial(_pos_kernel, n_experts=n_experts),
        grid_spec=pltpu.PrefetchScalarGridSpec(
            num_scalar_prefetch=1, grid=(t // tm,), in_specs=[spec, spec], out_specs=spec),
        out_shape=jax.ShapeDtypeStruct((k, t), jnp.int32),
        compiler_params=_cparams(1),
        name="moe_positions",
    )(base, top_e, rank)


def _sc_dispatch(rows, pos3, cap):
    t, w = rows.shape
    n_chunks, top_k, n = pos3.shape
    n_workers = SC_CORES * SC_SUBCORES
    assert n_chunks * n == t and n <= LANES
    per_worker = -(-n_chunks // n_workers)
    mesh = plsc.VectorSubcoreMesh(core_axis_name="core", subcore_axis_name="subcore",
                                  num_cores=SC_CORES, num_subcores=SC_SUBCORES)

    def body(rows_hbm, pos_hbm, out_hbm, idx_v, rows_v, sem):
        wid = lax.axis_index("subcore") * SC_CORES + lax.axis_index("core")

        @pl.loop(0, per_worker)
        def _(j):
            c = wid * per_worker + j

            @pl.when(c < n_chunks)
            def _():
                pltpu.sync_copy(pos_hbm.at[c], idx_v)
                pltpu.sync_copy(rows_hbm.at[pl.ds(c * n, n)], rows_v)
                copies = [pltpu.async_copy(rows_v, out_hbm.at[idx_v.at[k]], sem) for k in range(top_k)]
                for cp in copies:
                    cp.wait()

    return pl.kernel(
        body,
        out_type=jax.ShapeDtypeStruct((cap, w), I32),
        mesh=mesh,
        scratch_types=[pltpu.VMEM((top_k, n), I32), pltpu.VMEM((n, w), I32), pltpu.SemaphoreType.DMA],
        name="moe_dispatch_sc",
    )(rows, pos3)


def _sc_combine(rows, pos3, wts3):
    cap, w = rows.shape
    n_chunks, top_k, n = pos3.shape
    t = n_chunks * n
    n_workers = SC_CORES * SC_SUBCORES
    assert n <= SC_LANES and w % SC_LANES == 0
    per_worker = -(-n_chunks // n_workers)
    mesh = plsc.VectorSubcoreMesh(core_axis_name="core", subcore_axis_name="subcore",
                                  num_cores=SC_CORES, num_subcores=SC_SUBCORES)

    def body(rows_hbm, pos_hbm, wts_hbm, out_hbm, idx_v, wts_v, acc_v, *rest):
        bufs, sem = rest[:top_k], rest[top_k]
        wid = lax.axis_index("subcore") * SC_CORES + lax.axis_index("core")

        @pl.loop(0, per_worker)
        def _(it):
            c = wid * per_worker + it

            @pl.when(c < n_chunks)
            def _():
                pltpu.sync_copy(pos_hbm.at[c], idx_v)
                pltpu.sync_copy(wts_hbm.at[c], wts_v)
                gathers = [pltpu.async_copy(rows_hbm.at[idx_v.at[k]], bufs[k], sem) for k in range(top_k)]
                for g in gathers:
                    g.wait()

                @pl.loop(0, n)
                def _(j):
                    tok = jnp.full((SC_LANES,), j, I32)
                    wk = [plsc.load_gather(wts_v, [jnp.full((SC_LANES,), k, I32), tok]) for k in range(top_k)]

                    @pl.loop(0, w // SC_LANES)
                    def _(g):
                        col = g * SC_LANES
                        hi = jnp.zeros((SC_LANES,), F32)
                        lo = jnp.zeros((SC_LANES,), F32)
                        for k in range(top_k):
                            words = bufs[k][j, pl.ds(col, SC_LANES)]
                            hi = hi + wk[k] * plsc.bitcast(words & jnp.int32(-65536), F32)
                            lo = lo + wk[k] * plsc.bitcast(lax.shift_left(words, 16), F32)
                        acc_v[j, pl.ds(col, SC_LANES)] = hi
                        acc_v[j, pl.ds(w + col, SC_LANES)] = lo

                pltpu.sync_copy(acc_v, out_hbm.at[pl.ds(c * n, n)])

    return pl.kernel(
        body,
        out_type=jax.ShapeDtypeStruct((t, 2 * w), F32),
        mesh=mesh,
        scratch_types=[pltpu.VMEM((top_k, n), I32), pltpu.VMEM((top_k, n), F32), pltpu.VMEM((n, 2 * w), F32)]
        + [pltpu.VMEM((n, w), I32) for _ in range(top_k)] + [pltpu.SemaphoreType.DMA],
        compiler_params=pltpu.CompilerParams(needs_layout_passes=False),
        name="moe_combine_sc",
    )(rows, pos3, wts3)


def _gmm_kernel(be_ref, bv_ref, nu_ref, x_ref, *refs, tmb, per_step):
    w_refs, o_ref = refs[:-1], refs[-1]
    i = pl.program_id(0)

    @pl.when(i * per_step < nu_ref[0])
    def _():
        for s in range(per_step):
            wg_ref, wu_ref, wd_ref = w_refs[3 * s:3 * s + 3]
            words = x_ref[s * tmb:(s + 1) * tmb, :]
            half = words.shape[1]
            live = lax.broadcasted_iota(I32, (tmb, half), 0) < bv_ref[i * per_step + s]
            xa, xb = _unpack_halves(jnp.where(live, words, 0))
            g = _dot(xa, wg_ref[:half, :]) + _dot(xb, wg_ref[half:, :])
            u = _dot(xa, wu_ref[:half, :]) + _dot(xb, wu_ref[half:, :])
            o_ref[s * tmb:(s + 1) * tmb, :] = _pack_halves(_dot((_silu(g) * u).astype(BF16), wd_ref[...]))


def _grouped_experts(x_sorted, block_e, block_valid, n_used, wg, wu, wd, tmb):
    cap, half = x_sorted.shape
    d = 2 * half
    ff = wg.shape[2]
    n_blocks = cap // tmb
    per_step = MOE_BLOCKS_PER_STEP if n_blocks % MOE_BLOCKS_PER_STEP == 0 else 1
    n_steps = n_blocks // per_step

    def row_map(i, be, bv, nu):
        return (jnp.minimum(i, (nu[0] + per_step - 1) // per_step - 1), 0)

    w_specs = []
    for s in range(per_step):
        w_map = functools.partial(lambda i, be, bv, nu, s: (be[i * per_step + s], 0, 0), s=s)
        w_specs += [pl.BlockSpec((None, d, ff), w_map), pl.BlockSpec((None, d, ff), w_map),
                    pl.BlockSpec((None, ff, d), w_map)]

    return pl.pallas_call(
        functools.partial(_gmm_kernel, tmb=tmb, per_step=per_step),
        grid_spec=pltpu.PrefetchScalarGridSpec(
            num_scalar_prefetch=3,
            grid=(n_steps,),
            in_specs=[pl.BlockSpec((per_step * tmb, half), row_map)] + w_specs,
            out_specs=pl.BlockSpec((per_step * tmb, half), row_map),
        ),
        out_shape=jax.ShapeDtypeStruct((cap, half), I32),
        compiler_params=_cparams(1),
        name="moe_experts",
    )(block_e, block_valid, n_used, x_sorted, *([wg, wu, wd] * per_step))


def _moe_out_kernel(u_ref, r_ref, sg_ref, su_ref, sd_ref, x_ref, gate_ref, g_ref, b_ref, o_ref, *, alpha):
    ua, ub = _unpack_halves(u_ref[...])
    half = ua.shape[1]
    g = _dot(ua, sg_ref[:half, :]) + _dot(ub, sg_ref[half:, :])
    s = _dot(ua, su_ref[:half, :]) + _dot(ub, su_ref[half:, :])
    y = r_ref[...] + _dot((_silu(g) * s).astype(BF16), sd_ref[...])
    z = alpha * x_ref[...] + gate_ref[...] * y
    o_ref[...] = _normalize(z, LN_EPS) * g_ref[...] + b_ref[...]


def _moe_out(u2, routed, sg, su, sd, x, mod_i, ctx_row, ln_g, ln_b, alpha):
    b, l, d = x.shape
    ff = sg.shape[1]
    tm = min(ROW_TILE, l)
    assert l % tm == 0
    tok = pl.BlockSpec((None, tm, d), lambda bb, t: (bb, t, 0))
    return pl.pallas_call(
        functools.partial(_moe_out_kernel, alpha=alpha),
        grid=(b, l // tm),
        in_specs=[
            pl.BlockSpec((None, tm, d // 2), lambda bb, t: (bb, t, 0)),
            tok,
            pl.BlockSpec((d, ff), lambda bb, t: (0, 0)),
            pl.BlockSpec((d, ff), lambda bb, t: (0, 0)),
            pl.BlockSpec((ff, d), lambda bb, t: (0, 0)),
            tok,
            _mod_spec(5, d, ctx_row, 0),
            _row_spec(d),
            _row_spec(d),
        ],
        out_specs=tok,
        out_shape=jax.ShapeDtypeStruct((b, l, d), F32),
        compiler_params=_cparams(2),
        name="moe_shared_post",
    )(u2, routed, sg, su, sd, x, mod_i, ln_g, ln_b)


def _moe_layer(x, routing, mod_i, ctx_row, rwt, rbias, wg, wu, wd, sg, su, sd, ln_g, ln_b, alpha):
    b, l, d = x.shape
    t = b * l
    e = rwt.shape[0]
    tmb = MOE_ROW_TILE
    n_blocks = -(-(t * TOP_K) // tmb) + e
    u2, top_e, wts, rank, cnt = routing
    base, block_e, block_valid, n_used = _expert_layout(cnt, tmb, n_blocks)
    pos = _positions(base, top_e, rank, e, min(ROW_TILE, l))
    pos_d = pos.reshape(TOP_K, t // DISPATCH_CHUNK, DISPATCH_CHUNK).transpose(1, 0, 2)
    pos_c = pos.reshape(TOP_K, t // COMBINE_CHUNK, COMBINE_CHUNK).transpose(1, 0, 2)
    wts_c = wts.reshape(TOP_K, t // COMBINE_CHUNK, COMBINE_CHUNK).transpose(1, 0, 2)
    x_sorted = _sc_dispatch(u2.reshape(t, d // 2), pos_d, n_blocks * tmb)
    y_sorted = _grouped_experts(x_sorted, block_e, block_valid, n_used, wg, wu, wd, tmb)
    routed = _sc_combine(y_sorted, pos_c, wts_c).reshape(b, l, d)
    return _moe_out(u2, routed, sg, su, sd, x, mod_i, ctx_row, ln_g, ln_b, alpha)


def _rope_tables(l, hd, n_q_heads, n_k_heads, rope):
    axis_rot = hd // 2
    qscale = hd ** -0.5
    if rope:
        t = jnp.arange(l, dtype=jnp.int32)
        r = (t // GRID_W).astype(F32)
        col = (t % GRID_W).astype(F32)
        inv = ROPE_BASE ** (-jnp.arange(0, axis_rot, 2, dtype=F32) / axis_rot)
        ar = r[:, None] * inv
        ac = col[:, None] * inv
        ang = jnp.concatenate([ar, ar, ac, ac], axis=-1)
        cos, sin = jnp.cos(ang), jnp.sin(ang)
    else:
        cos, sin = jnp.ones((l, hd), F32), jnp.zeros((l, hd), F32)
    cos_t = jnp.concatenate([jnp.tile(cos, (1, n_q_heads)) * qscale, jnp.tile(cos, (1, n_k_heads))], axis=1)
    sin_t = jnp.concatenate([jnp.tile(sin, (1, n_q_heads)) * qscale, jnp.tile(sin, (1, n_k_heads))], axis=1)
    return cos_t, sin_t


def _rot_columns(w, hd):
    d, n = w.shape
    q = hd // 4
    w4 = w.reshape(d, n // (2 * q), 2, q)
    return jnp.stack([-w4[:, :, 1, :], w4[:, :, 0, :]], axis=2).reshape(d, n)


def kernel(x, c, ctx, c_ctx, w_ada, b_ada, ln_g, ln_b, attn_w_qkv, attn_w_o, attn_sinks, fnet_w,
           router_w, router_bias, exp_w_gate, exp_w_up, exp_w_down, sh_w_gate, sh_w_up, sh_w_down):
    b, l, d = x.shape
    cl = ctx.shape[1]
    depth = w_ada.shape[0]
    n_heads = attn_sinks.shape[1]
    hd = d // n_heads
    qw = n_heads * hd
    kvw = N_KV_HEADS * hd
    alpha = (2.0 * depth) ** 0.25
    gc = d // F_GROUPS

    mp = -(-(b + 1) // 16) * 16
    cvec = jnp.concatenate([c, c_ctx[None, :], jnp.zeros((mp - b - 1, d), F32)], axis=0)
    mod = _ada_all(cvec, w_ada, b_ada).reshape(depth, mp, 6, 1, d)

    def run_stream(x, h, mod):
        b = x.shape[0]
        for i in range(depth):
            kind = i % N_MIXERS
            j = i // N_MIXERS
            update_ctx = any((m % N_MIXERS) == 0 for m in range(i + 1, depth))
            mod_i = mod[i]
            g1 = ln_g[i, 0].reshape(1, d)
            b1 = ln_b[i, 0].reshape(1, d)
            g2 = ln_g[i, 1].reshape(1, d)
            b2 = ln_b[i, 1].reshape(1, d)
            rwt, rbias = moe_w[i][:2]

            if kind == 0:
                w_all, w_o = attn_w[j]
                q, k, v = _qkv_proj(x, mod_i, None, w_all, cos_l, sin_l, qw, kvw)
                q_c, k_c, v_c = _qkv_proj(h, mod_i, b, w_all, cos_c, sin_c, qw, kvw)
                o = _attention(q, k, v, k_c, v_c, attn_sinks[j], True)
                x, x_route = _proj_post(o, w_o, x, mod_i, 2, None, g1, b1, alpha, rwt, rbias)
                if update_ctx:
                    o_c = _attention(q_c, None, None, k_c, v_c, attn_sinks[j], False)
                    h, h_route = _proj_post(o_c, w_o, h, mod_i, 2, b, g1, b1, alpha, rwt, rbias)
            else:
                streams = [(x, None)] + ([(h, b)] if update_ctx else [])
                outs = []
                for s, ctx_row in streams:
                    c_l, s_l = dft_seq[s.shape[1]]
                    ab = _fnet_a(s, mod_i, ctx_row, cs)
                    outs.append(_fnet_b(c_l, s_l, ab, fnet_wb[j], s, mod_i, 2, ctx_row, g1, b1, alpha, rwt, rbias))
                x, x_route = outs[0]
                if update_ctx:
                    h, h_route = outs[1]

            x = _moe_layer(x, x_route, mod_i, None, *moe_w[i], g2, b2, alpha)
            if update_ctx:
                h = _moe_layer(h, h_route, mod_i, b, *moe_w[i], g2, b2, alpha)
        return x

    attn_w = []
    for j in range(attn_w_qkv.shape[0]):
        w = attn_w_qkv[j]
        w_all = jnp.concatenate([w, _rot_columns(w[:, :qw + kvw], hd)], axis=1).astype(BF16)
        attn_w.append((w_all, attn_w_o[j].astype(BF16)))
    cos_l, sin_l = _rope_tables(l, hd, n_heads, N_KV_HEADS, True)
    cos_c, sin_c = _rope_tables(cl, hd, n_heads, N_KV_HEADS, False)
    fnet_wb = [fnet_w[j].astype(BF16) for j in range(fnet_w.shape[0])]
    cc, sc = _dft_tables(gc, gc ** -0.5)
    cs = jnp.concatenate([cc, sc], axis=1).astype(BF16)
    dft_seq = {}
    for ls in (l, cl):
        c_l, s_l = _dft_tables(ls, ls ** -0.5)
        dft_seq[ls] = (c_l.astype(BF16), (-s_l).astype(BF16))
    moe_w = [(router_w[i].T, router_bias[i], exp_w_gate[i].astype(BF16), exp_w_up[i].astype(BF16),
              exp_w_down[i].astype(BF16), sh_w_gate[i].astype(BF16), sh_w_up[i].astype(BF16),
              sh_w_down[i].astype(BF16)) for i in range(depth)]

    n_streams = N_STREAMS if b % N_STREAMS == 0 else 1
    bs = b // n_streams
    outs = []
    for s in range(n_streams):
        mod_s = jnp.concatenate([mod[:, s * bs:(s + 1) * bs], mod[:, b:b + 1]], axis=1)
        outs.append(run_stream(x[s * bs:(s + 1) * bs], ctx[s * bs:(s + 1) * bs], mod_s))
    return outs[0] if n_streams == 1 else jnp.concatenate(outs, axis=0)
```

```python
import functools
import math

import jax
import jax.numpy as jnp
from jax import lax
from jax.experimental import pallas as pl
from jax.experimental.pallas import tpu as pltpu
from jax.experimental.pallas import tpu_sc as plsc

F32 = jnp.float32
BF16 = jnp.bfloat16
I32 = jnp.int32

N_KV_HEADS = 4
WINDOW = 128
GRID_W = 64
ROPE_BASE = 10000.0
F_GROUPS = 4
TOP_K = 8
N_EXPERT_GROUPS = 8
TOPK_GROUPS = 4
ROUTED_SCALE = 2.5
N_MIXERS = 2
LN_EPS = 1e-5
MOD_EPS = 1e-6

LANES = 128
VMEM_LIMIT_BYTES = 52 * 1024 * 1024
ROW_TILE = 512
QKV_ROW_TILE = 256
Q_TILE = 128
ATTN_ROW_CHUNK = 32
ATTN_UNROLL = 16
MOE_ROW_TILE = 512
MOE_BLOCKS_PER_STEP = 2
DISPATCH_CHUNK = 128
COMBINE_CHUNK = 16
N_STREAMS = 2
SC_CORES = 2
SC_SUBCORES = 16
SC_LANES = 16
NEG_BIG = -1e30


def _cparams(n_axes):
    return pltpu.CompilerParams(dimension_semantics=("arbitrary",) * n_axes,
                                vmem_limit_bytes=VMEM_LIMIT_BYTES)


def _dot(a, b):
    return jnp.dot(a, b, preferred_element_type=F32)


def _dot_nt(a, b):
    return lax.dot_general(a, b, (((1,), (1,)), ((), ())), preferred_element_type=F32)


def _split_bf16(a):
    hi = a.astype(BF16)
    lo = (a - hi.astype(F32)).astype(BF16)
    return hi, lo


def _normalize(x, eps):
    mu = jnp.mean(x, axis=-1, keepdims=True)
    xc = x - mu
    var = jnp.mean(xc * xc, axis=-1, keepdims=True)
    return xc * lax.rsqrt(var + eps)


def _modulate(x, shift, scale):
    return _normalize(x, MOD_EPS) * (1.0 + scale) + shift


def _silu(x):
    return x * jax.nn.sigmoid(x)


def _pack_halves(x):
    n = x.shape[1] // 2
    r = x.astype(BF16).astype(F32)
    hi = pltpu.bitcast(r[:, :n], I32)
    lo = pltpu.bitcast(r[:, n:], I32)
    return hi | lax.shift_right_logical(lo, 16)


def _unpack_halves(w):
    a = pltpu.bitcast(w & jnp.int32(-65536), F32).astype(BF16)
    b = pltpu.bitcast(lax.shift_left(w, 16), F32).astype(BF16)
    return a, b


def _ada_kernel(c_ref, w_ref, b_ref, o_ref):
    s = _silu(c_ref[...])
    sh, sl = _split_bf16(s)
    wh, wl = _split_bf16(w_ref[...])
    o_ref[...] = _dot(sh, wh) + _dot(sl, wh) + _dot(sh, wl) + b_ref[...]


def _ada_all(cvec, w_ada, b_ada):
    depth, d, n = w_ada.shape
    mp = cvec.shape[0]
    tn = 1536
    assert n % tn == 0
    return pl.pallas_call(
        _ada_kernel,
        grid=(depth, n // tn),
        in_specs=[
            pl.BlockSpec((mp, d), lambda i, j: (0, 0)),
            pl.BlockSpec((None, d, tn), lambda i, j: (i, 0, j)),
            pl.BlockSpec((None, 1, tn), lambda i, j: (i, 0, j)),
        ],
        out_specs=pl.BlockSpec((None, mp, tn), lambda i, j: (i, 0, j)),
        out_shape=jax.ShapeDtypeStruct((depth, mp, n), F32),
        compiler_params=_cparams(2),
        name="ada_mod",
    )(cvec, w_ada, b_ada.reshape(depth, 1, n))


def _mod_spec(j, d, ctx_row, batch_axis):
    if ctx_row is None:
        return pl.BlockSpec((None, None, 1, d), lambda *g: (g[batch_axis], j, 0, 0))
    return pl.BlockSpec((None, None, 1, d), lambda *g: (ctx_row, j, 0, 0))


def _row_spec(d):
    return pl.BlockSpec((1, d), lambda *g: (0, 0))


def _qkv_kernel(x_ref, sh_ref, sc_ref, w_ref, cos_ref, sin_ref, q_ref, k_ref, v_ref, *, qw, kvw):
    u = _modulate(x_ref[...], sh_ref[...], sc_ref[...]).astype(BF16)
    r = _dot(u, w_ref[...])
    qk = r[:, :qw + kvw] * cos_ref[...] + r[:, qw + 2 * kvw:] * sin_ref[...]
    q_ref[...] = qk[:, :qw].astype(BF16)
    k_ref[...] = qk[:, qw:].astype(BF16)
    v_ref[...] = r[:, qw + kvw:qw + 2 * kvw].astype(BF16)


def _qkv_proj(x, mod_i, ctx_row, w_all, cos_t, sin_t, qw, kvw):
    b, l, d = x.shape
    tm = min(QKV_ROW_TILE, l)
    assert l % tm == 0
    n_all = w_all.shape[1]
    return pl.pallas_call(
        functools.partial(_qkv_kernel, qw=qw, kvw=kvw),
        grid=(l // tm, b),
        in_specs=[
            pl.BlockSpec((None, tm, d), lambda t, bb: (bb, t, 0)),
            _mod_spec(0, d, ctx_row, 1),
            _mod_spec(1, d, ctx_row, 1),
            pl.BlockSpec((d, n_all), lambda t, bb: (0, 0)),
            pl.BlockSpec((tm, qw + kvw), lambda t, bb: (t, 0)),
            pl.BlockSpec((tm, qw + kvw), lambda t, bb: (t, 0)),
        ],
        out_specs=[
            pl.BlockSpec((None, tm, qw), lambda t, bb: (bb, t, 0)),
            pl.BlockSpec((None, tm, kvw), lambda t, bb: (bb, t, 0)),
            pl.BlockSpec((None, tm, kvw), lambda t, bb: (bb, t, 0)),
        ],
        out_shape=[
            jax.ShapeDtypeStruct((b, l, qw), BF16),
            jax.ShapeDtypeStruct((b, l, kvw), BF16),
            jax.ShapeDtypeStruct((b, l, kvw), BF16),
        ],
        compiler_params=_cparams(2),
        name="qkv_rope",
    )(x, mod_i, mod_i, w_all, cos_t, sin_t)


def _attn_kernel(sink_ref, q_ref, *refs, tq, seq, n_kv, group, hd, has_window):
    if has_window:
        k_ref, v_ref, kc_ref, vc_ref, o_ref, s_scr, p_scr, m_scr, bias_scr = refs
    else:
        kc_ref, vc_ref, o_ref, s_scr, p_scr, m_scr = refs
    q = q_ref[...]
    kc = kc_ref[...]
    vc = vc_ref[...]
    span = tq + 2 * WINDOW if has_window else 0
    rows = group * tq
    if has_window:
        q0 = pl.program_id(1) * tq
        start = pl.multiple_of(jnp.clip(q0 - WINDOW, 0, seq - span), LANES)
        kw = k_ref[pl.ds(start, span), :]
        vw = v_ref[pl.ds(start, span), :]
        qpos = q0 + lax.broadcasted_iota(jnp.int32, (tq, span), 0)
        kpos = start + lax.broadcasted_iota(jnp.int32, (tq, span), 1)
        bias_scr[...] = jnp.where(jnp.abs(qpos - kpos) <= WINDOW, 0.0, NEG_BIG)

    for h in range(n_kv):
        heads = [h * group + g for g in range(group)]
        qh = jnp.concatenate([q[:, j * hd:(j + 1) * hd] for j in heads], axis=0)
        if has_window:
            s_scr[h, :, :span] = _dot_nt(qh, kw[:, h * hd:(h + 1) * hd])
        s_scr[h, :, span:] = _dot_nt(qh, kc[:, h * hd:(h + 1) * hd])

    chunks_per_head = tq // ATTN_ROW_CHUNK
    n_chunks = rows // ATTN_ROW_CHUNK
    n_tiles = (span + kc.shape[0]) // LANES
    win_tiles = span // LANES

    def logit_tiles(h, r):
        row = pl.multiple_of(r * ATTN_ROW_CHUNK, ATTN_ROW_CHUNK)
        s = s_scr[h, pl.ds(row, ATTN_ROW_CHUNK), :]
        tiles = [s[:, i * LANES:(i + 1) * LANES] for i in range(n_tiles)]
        if has_window:
            brow = pl.multiple_of((r % chunks_per_head) * ATTN_ROW_CHUNK, ATTN_ROW_CHUNK)
            bias = bias_scr[pl.ds(brow, ATTN_ROW_CHUNK), :]
            tiles = [t + bias[:, i * LANES:(i + 1) * LANES] if i < win_tiles else t for i, t in enumerate(tiles)]
        return row, tiles

    for h in range(n_kv):
        def row_max(r, carry, h=h):
            row, tiles = logit_tiles(h, r)
            sink = sink_ref[h * group + r // chunks_per_head]
            m = jnp.max(functools.reduce(jnp.maximum, tiles), axis=-1, keepdims=True)
            m_scr[h, pl.ds(row, ATTN_ROW_CHUNK), :] = jnp.broadcast_to(jnp.maximum(m, sink),
                                                                     (ATTN_ROW_CHUNK, LANES))
            return carry

        lax.fori_loop(0, n_chunks, row_max, 0, unroll=ATTN_UNROLL)

    for h in range(n_kv):
        def probs(r, carry, h=h):
            row, tiles = logit_tiles(h, r)
            sink = sink_ref[h * group + r // chunks_per_head]
            m = m_scr[h, pl.ds(row, ATTN_ROW_CHUNK), :]
            es = [jnp.exp(t - m) for t in tiles]
            for i, e in enumerate(es):
                p_scr[h, pl.ds(row, ATTN_ROW_CHUNK), i * LANES:(i + 1) * LANES] = e.astype(BF16)
            den = jnp.sum(functools.reduce(jnp.add, es), axis=-1, keepdims=True) + jnp.exp(sink - m)
            m_scr[h, pl.ds(row, ATTN_ROW_CHUNK), :] = 1.0 / den
            return carry

        lax.fori_loop(0, n_chunks, probs, 0, unroll=ATTN_UNROLL)

    for h in range(n_kv):
        o = _dot(p_scr[h, :, span:], vc[:, h * hd:(h + 1) * hd])
        if has_window:
            o = o + _dot(p_scr[h, :, :span], vw[:, h * hd:(h + 1) * hd])
        o = o * m_scr[h, :, :hd]
        for g in range(group):
            j = h * group + g
            o_ref[:, j * hd:(j + 1) * hd] = o[g * tq:(g + 1) * tq, :].astype(o_ref.dtype)


def _attention(q, k, v, kc, vc, sinks, has_window):
    b, l, qw = q.shape
    c, kvw = kc.shape[1], kc.shape[2]
    hd = kvw // N_KV_HEADS
    group = qw // kvw
    tq = Q_TILE if has_window else l
    assert l % tq == 0
    if has_window:
        assert l >= tq + 2 * WINDOW
    kern = functools.partial(_attn_kernel, tq=tq, seq=l, n_kv=N_KV_HEADS, group=group, hd=hd,
                             has_window=has_window)
    in_specs = [pl.BlockSpec(memory_space=pltpu.SMEM),
                pl.BlockSpec((None, tq, qw), lambda bb, t: (bb, t, 0))]
    args = [sinks, q]
    if has_window:
        in_specs += [pl.BlockSpec((None, l, kvw), lambda bb, t: (bb, 0, 0)),
                     pl.BlockSpec((None, l, kvw), lambda bb, t: (bb, 0, 0))]
        args += [k, v]
    in_specs += [pl.BlockSpec((None, c, kvw), lambda bb, t: (bb, 0, 0)),
                 pl.BlockSpec((None, c, kvw), lambda bb, t: (bb, 0, 0))]
    args += [kc, vc]
    n_keys = c + (tq + 2 * WINDOW if has_window else 0)
    scratch = [pltpu.VMEM((N_KV_HEADS, group * tq, n_keys), F32),
               pltpu.VMEM((N_KV_HEADS, group * tq, n_keys), BF16),
               pltpu.VMEM((N_KV_HEADS, group * tq, LANES), F32)]
    if has_window:
        scratch.append(pltpu.VMEM((tq, tq + 2 * WINDOW), F32))
    return pl.pallas_call(
        kern,
        grid=(b, l // tq),
        in_specs=in_specs,
        out_specs=pl.BlockSpec((None, tq, qw), lambda bb, t: (bb, t, 0)),
        out_shape=jax.ShapeDtypeStruct((b, l, qw), BF16),
        scratch_shapes=scratch,
        compiler_params=_cparams(2),
        name="win_attn" if has_window else "ctx_attn",
    )(*args)


def _proj_post_kernel(a_ref, w_ref, x_ref, gate_ref, g_ref, b_ref, *refs, alpha):
    route_in, o_ref, route_out = refs[:N_ROUTE_IN], refs[N_ROUTE_IN], refs[N_ROUTE_IN + 1:]
    y = _dot(a_ref[...], w_ref[...])
    z = alpha * x_ref[...] + gate_ref[...] * y
    x_new = _normalize(z, LN_EPS) * g_ref[...] + b_ref[...]
    o_ref[...] = x_new
    _route_tokens(x_new, *route_in, *route_out)


def _proj_post(a, w, x, mod_i, gate_j, ctx_row, ln_g, ln_b, alpha, rwt, rbias):
    b, l, d = x.shape
    ka = a.shape[2]
    tm = min(ROW_TILE, l)
    assert l % tm == 0
    r_in, r_args, r_out, r_shape = _route_io(b, l, d, tm, mod_i, ctx_row, rwt, rbias)
    res = pl.pallas_call(
        functools.partial(_proj_post_kernel, alpha=alpha),
        grid=(b, l // tm),
        in_specs=[
            pl.BlockSpec((None, tm, ka), lambda bb, t: (bb, t, 0)),
            pl.BlockSpec((ka, d), lambda bb, t: (0, 0)),
            pl.BlockSpec((None, tm, d), lambda bb, t: (bb, t, 0)),
            _mod_spec(gate_j, d, ctx_row, 0),
            _row_spec(d),
            _row_spec(d),
        ] + r_in,
        out_specs=[pl.BlockSpec((None, tm, d), lambda bb, t: (bb, t, 0))] + r_out,
        out_shape=[jax.ShapeDtypeStruct((b, l, d), F32)] + r_shape,
        compiler_params=_cparams(2),
        name="proj_post_route",
    )(a, w, x, mod_i, ln_g, ln_b, *r_args)
    return res[0], res[1:]


def _fnet_a_kernel(x_ref, sh_ref, sc_ref, cs_ref, o_ref, *, d, gc):
    u = _modulate(x_ref[...], sh_ref[...], sc_ref[...]).astype(BF16)
    cs = cs_ref[...]
    for g in range(d // gc):
        r = _dot(u[:, g * gc:(g + 1) * gc], cs)
        o_ref[:, g * gc:(g + 1) * gc] = r[:, :gc].astype(BF16)
        o_ref[:, d + g * gc:d + (g + 1) * gc] = r[:, gc:].astype(BF16)


def _fnet_a(x, mod_i, ctx_row, cs):
    b, l, d = x.shape
    gc = d // F_GROUPS
    tm = min(ROW_TILE, l)
    assert l % tm == 0
    return pl.pallas_call(
        functools.partial(_fnet_a_kernel, d=d, gc=gc),
        grid=(b, l // tm),
        in_specs=[
            pl.BlockSpec((None, tm, d), lambda bb, t: (bb, t, 0)),
            _mod_spec(0, d, ctx_row, 0),
            _mod_spec(1, d, ctx_row, 0),
            pl.BlockSpec((gc, 2 * gc), lambda bb, t: (0, 0)),
        ],
        out_specs=pl.BlockSpec((None, tm, 2 * d), lambda bb, t: (bb, t, 0)),
        out_shape=jax.ShapeDtypeStruct((b, l, 2 * d), BF16),
        compiler_params=_cparams(2),
        name="fnet_chan_dft",
    )(x, mod_i, mod_i, cs)


def _fnet_b_kernel(cl_ref, sl_ref, ab_ref, wf_ref, x_ref, gate_ref, g_ref, b_ref, *refs, d, alpha):
    route_in, o_ref, route_out = refs[:N_ROUTE_IN], refs[N_ROUTE_IN], refs[N_ROUTE_IN + 1:]
    f = _dot(cl_ref[...], ab_ref[:, :d]) + _dot(sl_ref[...], ab_ref[:, d:])
    y = _dot(f.astype(BF16), wf_ref[...])
    z = alpha * x_ref[...] + gate_ref[...] * y
    x_new = _normalize(z, LN_EPS) * g_ref[...] + b_ref[...]
    o_ref[...] = x_new
    _route_tokens(x_new, *route_in, *route_out)


def _fnet_b(cl, sl, ab, wf, x, mod_i, gate_j, ctx_row, ln_g, ln_b, alpha, rwt, rbias):
    b, l, d = x.shape
    tm = min(ROW_TILE, l)
    assert l % tm == 0
    r_in, r_args, r_out, r_shape = _route_io(b, l, d, tm, mod_i, ctx_row, rwt, rbias)
    res = pl.pallas_call(
        functools.partial(_fnet_b_kernel, d=d, alpha=alpha),
        grid=(b, l // tm),
        in_specs=[
            pl.BlockSpec((tm, l), lambda bb, t: (t, 0)),
            pl.BlockSpec((tm, l), lambda bb, t: (t, 0)),
            pl.BlockSpec((None, l, 2 * d), lambda bb, t: (bb, 0, 0)),
            pl.BlockSpec((d, d), lambda bb, t: (0, 0)),
            pl.BlockSpec((None, tm, d), lambda bb, t: (bb, t, 0)),
            _mod_spec(gate_j, d, ctx_row, 0),
            _row_spec(d),
            _row_spec(d),
        ] + r_in,
        out_specs=[pl.BlockSpec((None, tm, d), lambda bb, t: (bb, t, 0))] + r_out,
        out_shape=[jax.ShapeDtypeStruct((b, l, d), F32)] + r_shape,
        compiler_params=_cparams(2),
        name="fnet_seq_dft_route",
    )(cl, sl, ab, wf, x, mod_i, ln_g, ln_b, *r_args)
    return res[0], res[1:]


def _dft_tables(n, scale):
    j = jnp.arange(n, dtype=jnp.int32)
    ang = ((j[:, None] * j[None, :]) % n).astype(F32) * (2.0 * math.pi / n)
    return jnp.cos(ang) * scale, jnp.sin(ang) * scale


def _route_tokens(x, sh_ref, sc_ref, rwt_ref, bias_ref, tri_ref, u_ref, te_ref, w_ref, rk_ref, cnt_ref):
    n_groups, topk_groups, top_k = N_EXPERT_GROUPS, TOPK_GROUPS, TOP_K
    u = _modulate(x, sh_ref[...], sc_ref[...])
    uh, ul = _split_bf16(u)
    u_ref[...] = _pack_halves(u)
    wh, wl = _split_bf16(rwt_ref[...])
    logits = _dot_nt(wh, uh) + _dot_nt(wl, uh) + _dot_nt(wh, ul)
    e, tm = logits.shape
    pg = e // n_groups
    neg = -jnp.inf
    scores = jax.nn.sigmoid(logits)
    sel = scores + bias_ref[...]
    sub = lax.broadcasted_iota(jnp.int32, (pg, tm), 0)
    groups = [sel[g * pg:(g + 1) * pg, :] for g in range(n_groups)]
    sgroups = [scores[g * pg:(g + 1) * pg, :] for g in range(n_groups)]

    gs_rows = []
    for s_g in groups:
        m1 = jnp.max(s_g, axis=0, keepdims=True)
        first = jnp.min(jnp.where(s_g == m1, sub, pg), axis=0, keepdims=True)
        m2 = jnp.max(jnp.where(sub == first, neg, s_g), axis=0, keepdims=True)
        gs_rows.append(m1 + m2)
    gs = jnp.concatenate(gs_rows, axis=0)
    gidx = lax.broadcasted_iota(jnp.int32, (n_groups, tm), 0)
    ok = jnp.zeros((n_groups, tm), F32)
    for _ in range(topk_groups):
        m = jnp.max(gs, axis=0, keepdims=True)
        first = jnp.min(jnp.where(gs == m, gidx, n_groups), axis=0, keepdims=True)
        hit = gidx == first
        ok = jnp.where(hit, 1.0, ok)
        gs = jnp.where(hit, neg, gs)

    cur = [jnp.where(ok[g:g + 1, :] > 0.0, groups[g], neg) for g in range(n_groups)]
    eidx = [sub + g * pg for g in range(n_groups)]
    chosen = [jnp.zeros((pg, tm), F32) for _ in range(n_groups)]
    e_rows, s_rows = [], []
    for _ in range(top_k):
        m = jnp.max(functools.reduce(jnp.maximum, cur), axis=0, keepdims=True)
        cand = functools.reduce(jnp.minimum, [jnp.where(cur[g] == m, eidx[g], e) for g in range(n_groups)])
        first = jnp.min(cand, axis=0, keepdims=True)
        picked = jnp.zeros((pg, tm), F32)
        for g in range(n_groups):
            hit = eidx[g] == first
            picked = picked + jnp.where(hit, sgroups[g], 0.0)
            chosen[g] = jnp.where(hit, 1.0, chosen[g])
            cur[g] = jnp.where(hit, neg, cur[g])
        e_rows.append(first)
        s_rows.append(jnp.sum(picked, axis=0, keepdims=True))
    w = jnp.concatenate(s_rows, axis=0)
    te_ref[...] = jnp.concatenate(e_rows, axis=0)
    w_ref[...] = w / jnp.sum(w, axis=0, keepdims=True) * ROUTED_SCALE

    sel_mask = jnp.concatenate(chosen, axis=0).astype(BF16)
    before = _dot(sel_mask, tri_ref[...])
    rk_rows = []
    for k in range(top_k):
        acc = jnp.zeros((pg, tm), F32)
        for g in range(n_groups):
            acc = acc + jnp.where(eidx[g] == e_rows[k], before[g * pg:(g + 1) * pg, :], 0.0)
        rk_rows.append(jnp.sum(acc, axis=0, keepdims=True))
    rk_ref[...] = jnp.concatenate(rk_rows, axis=0).astype(jnp.int32)
    cnt_ref[...] = _dot_nt(jnp.ones((8, tm), BF16), sel_mask)


N_ROUTE_IN = 5
N_ROUTE_OUT = 5


def _route_io(b, l, d, tm, mod_i, ctx_row, rwt, bias):
    e = rwt.shape[0]
    assert l % tm == 0 and e % N_EXPERT_GROUPS == 0
    nt = l // tm
    t = b * l
    row = lax.broadcasted_iota(jnp.int32, (tm, tm), 0)
    col = lax.broadcasted_iota(jnp.int32, (tm, tm), 1)
    tri = (row < col).astype(BF16)
    in_specs = [
        _mod_spec(3, d, ctx_row, 0),
        _mod_spec(4, d, ctx_row, 0),
        pl.BlockSpec((e, d), lambda bb, tt: (0, 0)),
        pl.BlockSpec((e, 1), lambda bb, tt: (0, 0)),
        pl.BlockSpec((tm, tm), lambda bb, tt: (0, 0)),
    ]
    args = [mod_i, mod_i, rwt, bias.reshape(e, 1).astype(F32), tri]
    tok_spec = pl.BlockSpec((TOP_K, tm), lambda bb, tt: (0, bb * nt + tt))
    out_specs = [
        pl.BlockSpec((None, tm, d // 2), lambda bb, tt: (bb, tt, 0)),
        tok_spec, tok_spec, tok_spec,
        pl.BlockSpec((None, 8, e), lambda bb, tt: (bb * nt + tt, 0, 0)),
    ]
    out_shape = [
        jax.ShapeDtypeStruct((b, l, d // 2), I32),
        jax.ShapeDtypeStruct((TOP_K, t), jnp.int32),
        jax.ShapeDtypeStruct((TOP_K, t), F32),
        jax.ShapeDtypeStruct((TOP_K, t), jnp.int32),
        jax.ShapeDtypeStruct((b * nt, 8, e), F32),
    ]
    return in_specs, args, out_specs, out_shape


def _expert_layout(cnt, tmb, n_blocks):
    cnt_i = cnt[:, 0, :].astype(jnp.int32)
    e = cnt_i.shape[1]
    counts = cnt_i.sum(axis=0)
    padded = (counts + tmb - 1) // tmb * tmb
    pend = jnp.cumsum(padded)
    base = (pend - padded)[None, :] + jnp.cumsum(cnt_i, axis=0) - cnt_i
    n_used = (pend[-1] // tmb).astype(jnp.int32)
    blk = jnp.arange(n_blocks, dtype=jnp.int32)
    block_e = jnp.sum((blk[:, None] * tmb >= pend[None, :]).astype(jnp.int32), axis=1)
    last_e = jnp.sum((jnp.maximum(n_used - 1, 0) * tmb >= pend).astype(jnp.int32))
    block_e = jnp.clip(jnp.where(blk < n_used, block_e, last_e), 0, e - 1).astype(jnp.int32)
    seg_end = (pend - padded + counts)[block_e]
    block_valid = jnp.clip(seg_end - blk * tmb, 0, tmb).astype(jnp.int32)
    return base.reshape(-1).astype(jnp.int32), block_e, block_valid, n_used.reshape(1)


def _pos_kernel(base_ref, te_ref, rk_ref, pos_ref, *, n_experts):
    i = pl.program_id(0)
    te = te_ref[...]
    pos = rk_ref[...]
    for e in range(n_experts):
        pos = pos + jnp.where(te == e, base_ref[i * n_experts + e], 0)
    pos_ref[...] = pos


def _positions(base, top_e, rank, n_experts, tm):
    k, t = top_e.shape
    spec = pl.BlockSpec((k, tm), lambda i, base_ref: (0, i))
    return pl.pallas_call(
        functools.partial(_pos_kernel, n_experts=n_experts),
        grid_spec=pltpu.PrefetchScalarGridSpec(
            num_scalar_prefetch=1, grid=(t // tm,), in_specs=[spec, spec], out_specs=spec),
        out_shape=jax.ShapeDtypeStruct((k, t), jnp.int32),
        compiler_params=_cparams(1),
        name="moe_positions",
    )(base, top_e, rank)


def _sc_dispatch(rows, pos3, cap):
    t, w = rows.shape
    n_chunks, top_k, n = pos3.shape
    n_workers = SC_CORES * SC_SUBCORES
    assert n_chunks * n == t and n <= LANES
    per_worker = -(-n_chunks // n_workers)
    mesh = plsc.VectorSubcoreMesh(core_axis_name="core", subcore_axis_name="subcore",
                                  num_cores=SC_CORES, num_subcores=SC_SUBCORES)

    def body(rows_hbm, pos_hbm, out_hbm, idx_v, rows_v, sem):
        wid = lax.axis_index("subcore") * SC_CORES + lax.axis_index("core")

        @pl.loop(0, per_worker)
        def _(j):
            c = wid * per_worker + j

            @pl.when(c < n_chunks)
            def _():
                pltpu.sync_copy(pos_hbm.at[c], idx_v)
                pltpu.sync_copy(rows_hbm.at[pl.ds(c * n, n)], rows_v)
                copies = [pltpu.async_copy(rows_v, out_hbm.at[idx_v.at[k]], sem) for k in range(top_k)]
                for cp in copies:
                    cp.wait()

    return pl.kernel(
        body,
        out_type=jax.ShapeDtypeStruct((cap, w), I32),
        mesh=mesh,
        scratch_types=[pltpu.VMEM((top_k, n), I32), pltpu.VMEM((n, w), I32), pltpu.SemaphoreType.DMA],
        name="moe_dispatch_sc",
    )(rows, pos3)


def _sc_combine(rows, pos3, wts3):
    cap, w = rows.shape
    n_chunks, top_k, n = pos3.shape
    t = n_chunks * n
    n_workers = SC_CORES * SC_SUBCORES
    assert n <= SC_LANES and w % SC_LANES == 0
    per_worker = -(-n_chunks // n_workers)
    mesh = plsc.VectorSubcoreMesh(core_axis_name="core", subcore_axis_name="subcore",
                                  num_cores=SC_CORES, num_subcores=SC_SUBCORES)

    def body(rows_hbm, pos_hbm, wts_hbm, out_hbm, idx_v, wts_v, acc_v, *rest):
        bufs, sem = rest[:top_k], rest[top_k]
        wid = lax.axis_index("subcore") * SC_CORES + lax.axis_index("core")

        @pl.loop(0, per_worker)
        def _(it):
            c = wid * per_worker + it

            @pl.when(c < n_chunks)
            def _():
                pltpu.sync_copy(pos_hbm.at[c], idx_v)
                pltpu.sync_copy(wts_hbm.at[c], wts_v)
                gathers = [pltpu.async_copy(rows_hbm.at[idx_v.at[k]], bufs[k], sem) for k in range(top_k)]
                for g in gathers:
                    g.wait()

                @pl.loop(0, n)
                def _(j):
                    tok = jnp.full((SC_LANES,), j, I32)
                    wk = [plsc.load_gather(wts_v, [jnp.full((SC_LANES,), k, I32), tok]) for k in range(top_k)]

                    @plsc.parallel_loop(0, w // SC_LANES, unroll=4)
                    def _(g):
                        col = g * SC_LANES
                        hi = jnp.zeros((SC_LANES,), F32)
                        lo = jnp.zeros((SC_LANES,), F32)
                        for k in range(top_k):
                            words = bufs[k][j, pl.ds(col, SC_LANES)]
                            hi = hi + wk[k] * plsc.bitcast(words & jnp.int32(-65536), F32)
                            lo = lo + wk[k] * plsc.bitcast(lax.shift_left(words, 16), F32)
                        acc_v[j, pl.ds(col, SC_LANES)] = hi
                        acc_v[j, pl.ds(w + col, SC_LANES)] = lo

                pltpu.sync_copy(acc_v, out_hbm.at[pl.ds(c * n, n)])

    return pl.kernel(
        body,
        out_type=jax.ShapeDtypeStruct((t, 2 * w), F32),
        mesh=mesh,
        scratch_types=[pltpu.VMEM((top_k, n), I32), pltpu.VMEM((top_k, n), F32), pltpu.VMEM((n, 2 * w), F32)]
        + [pltpu.VMEM((n, w), I32) for _ in range(top_k)] + [pltpu.SemaphoreType.DMA],
        compiler_params=pltpu.CompilerParams(needs_layout_passes=False),
        name="moe_combine_sc",
    )(rows, pos3, wts3)


def _gmm_kernel(be_ref, bv_ref, nu_ref, x_ref, *refs, tmb, per_step):
    w_refs, o_ref = refs[:-1], refs[-1]
    i = pl.program_id(0)

    @pl.when(i * per_step < nu_ref[0])
    def _():
        for s in range(per_step):
            wg_ref, wu_ref, wd_ref = w_refs[3 * s:3 * s + 3]
            words = x_ref[s * tmb:(s + 1) * tmb, :]
            half = words.shape[1]
            live = lax.broadcasted_iota(I32, (tmb, half), 0) < bv_ref[i * per_step + s]
            xa, xb = _unpack_halves(jnp.where(live, words, 0))
            g = _dot(xa, wg_ref[:half, :]) + _dot(xb, wg_ref[half:, :])
            u = _dot(xa, wu_ref[:half, :]) + _dot(xb, wu_ref[half:, :])
            o_ref[s * tmb:(s + 1) * tmb, :] = _pack_halves(_dot((_silu(g) * u).astype(BF16), wd_ref[...]))


def _grouped_experts(x_sorted, block_e, block_valid, n_used, wg, wu, wd, tmb):
    cap, half = x_sorted.shape
    d = 2 * half
    ff = wg.shape[2]
    n_blocks = cap // tmb
    per_step = MOE_BLOCKS_PER_STEP if n_blocks % MOE_BLOCKS_PER_STEP == 0 else 1
    n_steps = n_blocks // per_step

    def row_map(i, be, bv, nu):
        return (jnp.minimum(i, (nu[0] + per_step - 1) // per_step - 1), 0)

    w_specs = []
    for s in range(per_step):
        w_map = functools.partial(lambda i, be, bv, nu, s: (be[i * per_step + s], 0, 0), s=s)
        w_specs += [pl.BlockSpec((None, d, ff), w_map), pl.BlockSpec((None, d, ff), w_map),
                    pl.BlockSpec((None, ff, d), w_map)]

    return pl.pallas_call(
        functools.partial(_gmm_kernel, tmb=tmb, per_step=per_step),
        grid_spec=pltpu.PrefetchScalarGridSpec(
            num_scalar_prefetch=3,
            grid=(n_steps,),
            in_specs=[pl.BlockSpec((per_step * tmb, half), row_map)] + w_specs,
            out_specs=pl.BlockSpec((per_step * tmb, half), row_map),
        ),
        out_shape=jax.ShapeDtypeStruct((cap, half), I32),
        compiler_params=_cparams(1),
        name="moe_experts",
    )(block_e, block_valid, n_used, x_sorted, *([wg, wu, wd] * per_step))


def _moe_out_kernel(u_ref, r_ref, sg_ref, su_ref, sd_ref, x_ref, gate_ref, g_ref, b_ref, o_ref, *, alpha):
    ua, ub = _unpack_halves(u_ref[...])
    half = ua.shape[1]
    g = _dot(ua, sg_ref[:half, :]) + _dot(ub, sg_ref[half:, :])
    s = _dot(ua, su_ref[:half, :]) + _dot(ub, su_ref[half:, :])
    y = r_ref[...] + _dot((_silu(g) * s).astype(BF16), sd_ref[...])
    z = alpha * x_ref[...] + gate_ref[...] * y
    o_ref[...] = _normalize(z, LN_EPS) * g_ref[...] + b_ref[...]


def _moe_out(u2, routed, sg, su, sd, x, mod_i, ctx_row, ln_g, ln_b, alpha):
    b, l, d = x.shape
    ff = sg.shape[1]
    tm = min(ROW_TILE, l)
    assert l % tm == 0
    tok = pl.BlockSpec((None, tm, d), lambda bb, t: (bb, t, 0))
    return pl.pallas_call(
        functools.partial(_moe_out_kernel, alpha=alpha),
        grid=(b, l // tm),
        in_specs=[
            pl.BlockSpec((None, tm, d // 2), lambda bb, t: (bb, t, 0)),
            tok,
            pl.BlockSpec((d, ff), lambda bb, t: (0, 0)),
            pl.BlockSpec((d, ff), lambda bb, t: (0, 0)),
            pl.BlockSpec((ff, d), lambda bb, t: (0, 0)),
            tok,
            _mod_spec(5, d, ctx_row, 0),
            _row_spec(d),
            _row_spec(d),
        ],
        out_specs=tok,
        out_shape=jax.ShapeDtypeStruct((b, l, d), F32),
        compiler_params=_cparams(2),
        name="moe_shared_post",
    )(u2, routed, sg, su, sd, x, mod_i, ln_g, ln_b)


def _moe_dispatch(shape, routing, n_experts):
    b, l, d = shape
    t = b * l
    tmb = MOE_ROW_TILE
    n_blocks = -(-(t * TOP_K) // tmb) + n_experts
    u2, top_e, wts, rank, cnt = routing
    base, block_e, block_valid, n_used = _expert_layout(cnt, tmb, n_blocks)
    pos = _positions(base, top_e, rank, n_experts, min(ROW_TILE, l))
    pos_d = pos.reshape(TOP_K, t // DISPATCH_CHUNK, DISPATCH_CHUNK).transpose(1, 0, 2)
    pos_c = pos.reshape(TOP_K, t // COMBINE_CHUNK, COMBINE_CHUNK).transpose(1, 0, 2)
    wts_c = wts.reshape(TOP_K, t // COMBINE_CHUNK, COMBINE_CHUNK).transpose(1, 0, 2)
    x_sorted = _sc_dispatch(u2.reshape(t, d // 2), pos_d, n_blocks * tmb)
    return x_sorted, block_e, block_valid, n_used, pos_c, wts_c


def _moe_experts(shape, dispatched, wg, wu, wd):
    x_sorted, block_e, block_valid, n_used, pos_c, wts_c = dispatched
    y_sorted = _grouped_experts(x_sorted, block_e, block_valid, n_used, wg, wu, wd, MOE_ROW_TILE)
    return _sc_combine(y_sorted, pos_c, wts_c).reshape(shape)


def _rope_tables(l, hd, n_q_heads, n_k_heads, rope):
    axis_rot = hd // 2
    qscale = hd ** -0.5
    if rope:
        t = jnp.arange(l, dtype=jnp.int32)
        r = (t // GRID_W).astype(F32)
        col = (t % GRID_W).astype(F32)
        inv = ROPE_BASE ** (-jnp.arange(0, axis_rot, 2, dtype=F32) / axis_rot)
        ar = r[:, None] * inv
        ac = col[:, None] * inv
        ang = jnp.concatenate([ar, ar, ac, ac], axis=-1)
        cos, sin = jnp.cos(ang), jnp.sin(ang)
    else:
        cos, sin = jnp.ones((l, hd), F32), jnp.zeros((l, hd), F32)
    cos_t = jnp.concatenate([jnp.tile(cos, (1, n_q_heads)) * qscale, jnp.tile(cos, (1, n_k_heads))], axis=1)
    sin_t = jnp.concatenate([jnp.tile(sin, (1, n_q_heads)) * qscale, jnp.tile(sin, (1, n_k_heads))], axis=1)
    return cos_t, sin_t


def _rot_columns(w, hd):
    d, n = w.shape
    q = hd // 4
    w4 = w.reshape(d, n // (2 * q), 2, q)
    return jnp.stack([-w4[:, :, 1, :], w4[:, :, 0, :]], axis=2).reshape(d, n)


def kernel(x, c, ctx, c_ctx, w_ada, b_ada, ln_g, ln_b, attn_w_qkv, attn_w_o, attn_sinks, fnet_w,
           router_w, router_bias, exp_w_gate, exp_w_up, exp_w_down, sh_w_gate, sh_w_up, sh_w_down):
    b, l, d = x.shape
    cl = ctx.shape[1]
    depth = w_ada.shape[0]
    n_heads = attn_sinks.shape[1]
    hd = d // n_heads
    qw = n_heads * hd
    kvw = N_KV_HEADS * hd
    alpha = (2.0 * depth) ** 0.25
    gc = d // F_GROUPS

    mp = -(-(b + 1) // 16) * 16
    cvec = jnp.concatenate([c, c_ctx[None, :], jnp.zeros((mp - b - 1, d), F32)], axis=0)
    mod = _ada_all(cvec, w_ada, b_ada).reshape(depth, mp, 6, 1, d)

    def run_stream(x, h, mod):
        b = x.shape[0]
        for i in range(depth):
            kind = i % N_MIXERS
            j = i // N_MIXERS
            update_ctx = any((m % N_MIXERS) == 0 for m in range(i + 1, depth))
            mod_i = mod[i]
            g1 = ln_g[i, 0].reshape(1, d)
            b1 = ln_b[i, 0].reshape(1, d)
            g2 = ln_g[i, 1].reshape(1, d)
            b2 = ln_b[i, 1].reshape(1, d)
            rwt, rbias = moe_w[i][:2]

            if kind == 0:
                w_all, w_o = attn_w[j]
                q, k, v = _qkv_proj(x, mod_i, None, w_all, cos_l, sin_l, qw, kvw)
                q_c, k_c, v_c = _qkv_proj(h, mod_i, b, w_all, cos_c, sin_c, qw, kvw)
                o = _attention(q, k, v, k_c, v_c, attn_sinks[j], True)
                x, x_route = _proj_post(o, w_o, x, mod_i, 2, None, g1, b1, alpha, rwt, rbias)
                if update_ctx:
                    o_c = _attention(q_c, None, None, k_c, v_c, attn_sinks[j], False)
                    h, h_route = _proj_post(o_c, w_o, h, mod_i, 2, b, g1, b1, alpha, rwt, rbias)
            else:
                streams = [(x, None)] + ([(h, b)] if update_ctx else [])
                outs = []
                for s, ctx_row in streams:
                    c_l, s_l = dft_seq[s.shape[1]]
                    ab = _fnet_a(s, mod_i, ctx_row, cs)
                    outs.append(_fnet_b(c_l, s_l, ab, fnet_wb[j], s, mod_i, 2, ctx_row, g1, b1, alpha, rwt, rbias))
                x, x_route = outs[0]
                if update_ctx:
                    h, h_route = outs[1]

            wg, wu, wd, sg, su, sd = moe_w[i][2:]
            n_experts = rwt.shape[0]
            x_disp = _moe_dispatch(x.shape, x_route, n_experts)
            h_disp = _moe_dispatch(h.shape, h_route, n_experts) if update_ctx else None
            yield None
            x_routed = _moe_experts(x.shape, x_disp, wg, wu, wd)
            h_routed = _moe_experts(h.shape, h_disp, wg, wu, wd) if update_ctx else None
            yield None
            x = _moe_out(x_route[0], x_routed, sg, su, sd, x, mod_i, None, g2, b2, alpha)
            if update_ctx:
                h = _moe_out(h_route[0], h_routed, sg, su, sd, h, mod_i, b, g2, b2, alpha)
        yield x

    attn_w = []
    for j in range(attn_w_qkv.shape[0]):
        w = attn_w_qkv[j]
        w_all = jnp.concatenate([w, _rot_columns(w[:, :qw + kvw], hd)], axis=1).astype(BF16)
        attn_w.append((w_all, attn_w_o[j].astype(BF16)))
    cos_l, sin_l = _rope_tables(l, hd, n_heads, N_KV_HEADS, True)
    cos_c, sin_c = _rope_tables(cl, hd, n_heads, N_KV_HEADS, False)
    fnet_wb = [fnet_w[j].astype(BF16) for j in range(fnet_w.shape[0])]
    cc, sc = _dft_tables(gc, gc ** -0.5)
    cs = jnp.concatenate([cc, sc], axis=1).astype(BF16)
    dft_seq = {}
    for ls in (l, cl):
        c_l, s_l = _dft_tables(ls, ls ** -0.5)
        dft_seq[ls] = (c_l.astype(BF16), (-s_l).astype(BF16))
    moe_w = [(router_w[i].T, router_bias[i], exp_w_gate[i].astype(BF16), exp_w_up[i].astype(BF16),
              exp_w_down[i].astype(BF16), sh_w_gate[i].astype(BF16), sh_w_up[i].astype(BF16),
              sh_w_down[i].astype(BF16)) for i in range(depth)]

    n_streams = N_STREAMS if b % N_STREAMS == 0 else 1
    bs = b // n_streams
    gens = []
    for s in range(n_streams):
        mod_s = jnp.concatenate([mod[:, s * bs:(s + 1) * bs], mod[:, b:b + 1]], axis=1)
        gens.append(run_stream(x[s * bs:(s + 1) * bs], ctx[s * bs:(s + 1) * bs], mod_s))
    outs = [None] * n_streams
    while any(o is None for o in outs):
        for s, g in enumerate(gens):
            if outs[s] is None:
                outs[s] = next(g)
    return outs[0] if n_streams == 1 else jnp.concatenate(outs, axis=0)
```

```python
import functools
import math

import jax
import jax.numpy as jnp
from jax import lax
from jax.experimental import pallas as pl
from jax.experimental.pallas import tpu as pltpu
from jax.experimental.pallas import tpu_sc as plsc

F32 = jnp.float32
BF16 = jnp.bfloat16
I32 = jnp.int32

N_KV_HEADS = 4
WINDOW = 128
GRID_W = 64
ROPE_BASE = 10000.0
F_GROUPS = 4
TOP_K = 8
N_EXPERT_GROUPS = 8
TOPK_GROUPS = 4
ROUTED_SCALE = 2.5
N_MIXERS = 2
LN_EPS = 1e-5
MOD_EPS = 1e-6

LANES = 128
VMEM_LIMIT_BYTES = 52 * 1024 * 1024
ROW_TILE = 512
QKV_ROW_TILE = 256
Q_TILE = 128
ATTN_ROW_CHUNK = 32
ATTN_UNROLL = 16
MOE_ROW_TILE = 512
MOE_BLOCKS_PER_STEP = 2
DISPATCH_CHUNK = 128
COMBINE_CHUNK = 16
N_STREAMS = 2
SC_CORES = 2
SC_SUBCORES = 16
SC_LANES = 16
NEG_BIG = -1e30


def _cparams(n_axes):
    return pltpu.CompilerParams(dimension_semantics=("arbitrary",) * n_axes,
                                vmem_limit_bytes=VMEM_LIMIT_BYTES)


def _dot(a, b):
    return jnp.dot(a, b, preferred_element_type=F32)


def _dot_nt(a, b):
    return lax.dot_general(a, b, (((1,), (1,)), ((), ())), preferred_element_type=F32)


def _split_bf16(a):
    hi = a.astype(BF16)
    lo = (a - hi.astype(F32)).astype(BF16)
    return hi, lo


def _normalize(x, eps):
    mu = jnp.mean(x, axis=-1, keepdims=True)
    xc = x - mu
    var = jnp.mean(xc * xc, axis=-1, keepdims=True)
    return xc * lax.rsqrt(var + eps)


def _modulate(x, shift, scale):
    return _normalize(x, MOD_EPS) * (1.0 + scale) + shift


def _silu(x):
    return x * jax.nn.sigmoid(x)


def _pack_halves(x):
    n = x.shape[1] // 2
    r = x.astype(BF16).astype(F32)
    hi = pltpu.bitcast(r[:, :n], I32)
    lo = pltpu.bitcast(r[:, n:], I32)
    return hi | lax.shift_right_logical(lo, 16)


def _unpack_halves(w):
    a = pltpu.bitcast(w & jnp.int32(-65536), F32).astype(BF16)
    b = pltpu.bitcast(lax.shift_left(w, 16), F32).astype(BF16)
    return a, b


def _ada_kernel(c_ref, w_ref, b_ref, o_ref):
    s = _silu(c_ref[...])
    sh, sl = _split_bf16(s)
    wh, wl = _split_bf16(w_ref[...])
    o_ref[...] = _dot(sh, wh) + _dot(sl, wh) + _dot(sh, wl) + b_ref[...]


def _ada_all(cvec, w_ada, b_ada):
    depth, d, n = w_ada.shape
    mp = cvec.shape[0]
    tn = 1536
    assert n % tn == 0
    return pl.pallas_call(
        _ada_kernel,
        grid=(depth, n // tn),
        in_specs=[
            pl.BlockSpec((mp, d), lambda i, j: (0, 0)),
            pl.BlockSpec((None, d, tn), lambda i, j: (i, 0, j)),
            pl.BlockSpec((None, 1, tn), lambda i, j: (i, 0, j)),
        ],
        out_specs=pl.BlockSpec((None, mp, tn), lambda i, j: (i, 0, j)),
        out_shape=jax.ShapeDtypeStruct((depth, mp, n), F32),
        compiler_params=_cparams(2),
        name="ada_mod",
    )(cvec, w_ada, b_ada.reshape(depth, 1, n))


def _mod_spec(j, d, ctx_row, batch_axis):
    if ctx_row is None:
        return pl.BlockSpec((None, None, 1, d), lambda *g: (g[batch_axis], j, 0, 0))
    return pl.BlockSpec((None, None, 1, d), lambda *g: (ctx_row, j, 0, 0))


def _row_spec(d):
    return pl.BlockSpec((1, d), lambda *g: (0, 0))


def _qkv_kernel(x_ref, sh_ref, sc_ref, w_ref, cos_ref, sin_ref, q_ref, k_ref, v_ref, *, qw, kvw):
    u = _modulate(x_ref[...], sh_ref[...], sc_ref[...]).astype(BF16)
    r = _dot(u, w_ref[...])
    qk = r[:, :qw + kvw] * cos_ref[...] + r[:, qw + 2 * kvw:] * sin_ref[...]
    q_ref[...] = qk[:, :qw].astype(BF16)
    k_ref[...] = qk[:, qw:].astype(BF16)
    v_ref[...] = r[:, qw + kvw:qw + 2 * kvw].astype(BF16)


def _qkv_proj(x, mod_i, ctx_row, w_all, cos_t, sin_t, qw, kvw):
    b, l, d = x.shape
    tm = min(QKV_ROW_TILE, l)
    assert l % tm == 0
    n_all = w_all.shape[1]
    return pl.pallas_call(
        functools.partial(_qkv_kernel, qw=qw, kvw=kvw),
        grid=(l // tm, b),
        in_specs=[
            pl.BlockSpec((None, tm, d), lambda t, bb: (bb, t, 0)),
            _mod_spec(0, d, ctx_row, 1),
            _mod_spec(1, d, ctx_row, 1),
            pl.BlockSpec((d, n_all), lambda t, bb: (0, 0)),
            pl.BlockSpec((tm, qw + kvw), lambda t, bb: (t, 0)),
            pl.BlockSpec((tm, qw + kvw), lambda t, bb: (t, 0)),
        ],
        out_specs=[
            pl.BlockSpec((None, tm, qw), lambda t, bb: (bb, t, 0)),
            pl.BlockSpec((None, tm, kvw), lambda t, bb: (bb, t, 0)),
            pl.BlockSpec((None, tm, kvw), lambda t, bb: (bb, t, 0)),
        ],
        out_shape=[
            jax.ShapeDtypeStruct((b, l, qw), BF16),
            jax.ShapeDtypeStruct((b, l, kvw), BF16),
            jax.ShapeDtypeStruct((b, l, kvw), BF16),
        ],
        compiler_params=_cparams(2),
        name="qkv_rope",
    )(x, mod_i, mod_i, w_all, cos_t, sin_t)


def _attn_kernel(sink_ref, q_ref, *refs, tq, seq, n_kv, group, hd, has_window):
    if has_window:
        k_ref, v_ref, kc_ref, vc_ref, o_ref, s_scr, p_scr, m_scr, bias_scr = refs
    else:
        kc_ref, vc_ref, o_ref, s_scr, p_scr, m_scr = refs
    q = q_ref[...]
    kc = kc_ref[...]
    vc = vc_ref[...]
    span = tq + 2 * WINDOW if has_window else 0
    rows = group * tq
    if has_window:
        q0 = pl.program_id(1) * tq
        start = pl.multiple_of(jnp.clip(q0 - WINDOW, 0, seq - span), LANES)
        kw = k_ref[pl.ds(start, span), :]
        vw = v_ref[pl.ds(start, span), :]
        qpos = q0 + lax.broadcasted_iota(jnp.int32, (tq, span), 0)
        kpos = start + lax.broadcasted_iota(jnp.int32, (tq, span), 1)
        bias_scr[...] = jnp.where(jnp.abs(qpos - kpos) <= WINDOW, 0.0, NEG_BIG)

    for h in range(n_kv):
        heads = [h * group + g for g in range(group)]
        qh = jnp.concatenate([q[:, j * hd:(j + 1) * hd] for j in heads], axis=0)
        if has_window:
            s_scr[h, :, :span] = _dot_nt(qh, kw[:, h * hd:(h + 1) * hd])
        s_scr[h, :, span:] = _dot_nt(qh, kc[:, h * hd:(h + 1) * hd])

    chunks_per_head = tq // ATTN_ROW_CHUNK
    n_chunks = rows // ATTN_ROW_CHUNK
    n_tiles = (span + kc.shape[0]) // LANES
    win_tiles = span // LANES

    def logit_tiles(h, r):
        row = pl.multiple_of(r * ATTN_ROW_CHUNK, ATTN_ROW_CHUNK)
        s = s_scr[h, pl.ds(row, ATTN_ROW_CHUNK), :]
        tiles = [s[:, i * LANES:(i + 1) * LANES] for i in range(n_tiles)]
        if has_window:
            brow = pl.multiple_of((r % chunks_per_head) * ATTN_ROW_CHUNK, ATTN_ROW_CHUNK)
            bias = bias_scr[pl.ds(brow, ATTN_ROW_CHUNK), :]
            tiles = [t + bias[:, i * LANES:(i + 1) * LANES] if i < win_tiles else t for i, t in enumerate(tiles)]
        return row, tiles

    for h in range(n_kv):
        def row_max(r, carry, h=h):
            row, tiles = logit_tiles(h, r)
            sink = sink_ref[h * group + r // chunks_per_head]
            m = jnp.max(functools.reduce(jnp.maximum, tiles), axis=-1, keepdims=True)
            m_scr[h, pl.ds(row, ATTN_ROW_CHUNK), :] = jnp.broadcast_to(jnp.maximum(m, sink),
                                                                     (ATTN_ROW_CHUNK, LANES))
            return carry

        lax.fori_loop(0, n_chunks, row_max, 0, unroll=ATTN_UNROLL)

    for h in range(n_kv):
        def probs(r, carry, h=h):
            row, tiles = logit_tiles(h, r)
            sink = sink_ref[h * group + r // chunks_per_head]
            m = m_scr[h, pl.ds(row, ATTN_ROW_CHUNK), :]
            es = [jnp.exp(t - m) for t in tiles]
            for i, e in enumerate(es):
                p_scr[h, pl.ds(row, ATTN_ROW_CHUNK), i * LANES:(i + 1) * LANES] = e.astype(BF16)
            den = jnp.sum(functools.reduce(jnp.add, es), axis=-1, keepdims=True) + jnp.exp(sink - m)
            m_scr[h, pl.ds(row, ATTN_ROW_CHUNK), :] = 1.0 / den
            return carry

        lax.fori_loop(0, n_chunks, probs, 0, unroll=ATTN_UNROLL)

    for h in range(n_kv):
        o = _dot(p_scr[h, :, span:], vc[:, h * hd:(h + 1) * hd])
        if has_window:
            o = o + _dot(p_scr[h, :, :span], vw[:, h * hd:(h + 1) * hd])
        o = o * m_scr[h, :, :hd]
        for g in range(group):
            j = h * group + g
            o_ref[:, j * hd:(j + 1) * hd] = o[g * tq:(g + 1) * tq, :].astype(o_ref.dtype)


def _attention(q, k, v, kc, vc, sinks, has_window):
    b, l, qw = q.shape
    c, kvw = kc.shape[1], kc.shape[2]
    hd = kvw // N_KV_HEADS
    group = qw // kvw
    tq = Q_TILE if has_window else l
    assert l % tq == 0
    if has_window:
        assert l >= tq + 2 * WINDOW
    kern = functools.partial(_attn_kernel, tq=tq, seq=l, n_kv=N_KV_HEADS, group=group, hd=hd,
                             has_window=has_window)
    in_specs = [pl.BlockSpec(memory_space=pltpu.SMEM),
                pl.BlockSpec((None, tq, qw), lambda bb, t: (bb, t, 0))]
    args = [sinks, q]
    if has_window:
        in_specs += [pl.BlockSpec((None, l, kvw), lambda bb, t: (bb, 0, 0)),
                     pl.BlockSpec((None, l, kvw), lambda bb, t: (bb, 0, 0))]
        args += [k, v]
    in_specs += [pl.BlockSpec((None, c, kvw), lambda bb, t: (bb, 0, 0)),
                 pl.BlockSpec((None, c, kvw), lambda bb, t: (bb, 0, 0))]
    args += [kc, vc]
    n_keys = c + (tq + 2 * WINDOW if has_window else 0)
    scratch = [pltpu.VMEM((N_KV_HEADS, group * tq, n_keys), F32),
               pltpu.VMEM((N_KV_HEADS, group * tq, n_keys), BF16),
               pltpu.VMEM((N_KV_HEADS, group * tq, LANES), F32)]
    if has_window:
        scratch.append(pltpu.VMEM((tq, tq + 2 * WINDOW), F32))
    return pl.pallas_call(
        kern,
        grid=(b, l // tq),
        in_specs=in_specs,
        out_specs=pl.BlockSpec((None, tq, qw), lambda bb, t: (bb, t, 0)),
        out_shape=jax.ShapeDtypeStruct((b, l, qw), BF16),
        scratch_shapes=scratch,
        compiler_params=_cparams(2),
        name="win_attn" if has_window else "ctx_attn",
    )(*args)


def _proj_post_kernel(a_ref, w_ref, x_ref, gate_ref, g_ref, b_ref, *refs, alpha):
    route_in, o_ref, route_out = refs[:N_ROUTE_IN], refs[N_ROUTE_IN], refs[N_ROUTE_IN + 1:]
    y = _dot(a_ref[...], w_ref[...])
    z = alpha * x_ref[...] + gate_ref[...] * y
    x_new = _normalize(z, LN_EPS) * g_ref[...] + b_ref[...]
    o_ref[...] = x_new
    _route_tokens(x_new, *route_in, *route_out)


def _proj_post(a, w, x, mod_i, gate_j, ctx_row, ln_g, ln_b, alpha, rwt, rbias):
    b, l, d = x.shape
    ka = a.shape[2]
    tm = min(ROW_TILE, l)
    assert l % tm == 0
    r_in, r_args, r_out, r_shape = _route_io(b, l, d, tm, mod_i, ctx_row, rwt, rbias)
    res = pl.pallas_call(
        functools.partial(_proj_post_kernel, alpha=alpha),
        grid=(b, l // tm),
        in_specs=[
            pl.BlockSpec((None, tm, ka), lambda bb, t: (bb, t, 0)),
            pl.BlockSpec((ka, d), lambda bb, t: (0, 0)),
            pl.BlockSpec((None, tm, d), lambda bb, t: (bb, t, 0)),
            _mod_spec(gate_j, d, ctx_row, 0),
            _row_spec(d),
            _row_spec(d),
        ] + r_in,
        out_specs=[pl.BlockSpec((None, tm, d), lambda bb, t: (bb, t, 0))] + r_out,
        out_shape=[jax.ShapeDtypeStruct((b, l, d), F32)] + r_shape,
        compiler_params=_cparams(2),
        name="proj_post_route",
    )(a, w, x, mod_i, ln_g, ln_b, *r_args)
    return res[0], res[1:]


def _fnet_a_kernel(x_ref, sh_ref, sc_ref, cs_ref, o_ref, *, d, gc):
    u = _modulate(x_ref[...], sh_ref[...], sc_ref[...]).astype(BF16)
    cs = cs_ref[...]
    for g in range(d // gc):
        r = _dot(u[:, g * gc:(g + 1) * gc], cs)
        o_ref[:, g * gc:(g + 1) * gc] = r[:, :gc].astype(BF16)
        o_ref[:, d + g * gc:d + (g + 1) * gc] = r[:, gc:].astype(BF16)


def _fnet_a(x, mod_i, ctx_row, cs):
    b, l, d = x.shape
    gc = d // F_GROUPS
    tm = min(ROW_TILE, l)
    assert l % tm == 0
    return pl.pallas_call(
        functools.partial(_fnet_a_kernel, d=d, gc=gc),
        grid=(b, l // tm),
        in_specs=[
            pl.BlockSpec((None, tm, d), lambda bb, t: (bb, t, 0)),
            _mod_spec(0, d, ctx_row, 0),
            _mod_spec(1, d, ctx_row, 0),
            pl.BlockSpec((gc, 2 * gc), lambda bb, t: (0, 0)),
        ],
        out_specs=pl.BlockSpec((None, tm, 2 * d), lambda bb, t: (bb, t, 0)),
        out_shape=jax.ShapeDtypeStruct((b, l, 2 * d), BF16),
        compiler_params=_cparams(2),
        name="fnet_chan_dft",
    )(x, mod_i, mod_i, cs)


def _fnet_b_kernel(cl_ref, sl_ref, ab_ref, wf_ref, x_ref, gate_ref, g_ref, b_ref, *refs, d, alpha):
    route_in, o_ref, route_out = refs[:N_ROUTE_IN], refs[N_ROUTE_IN], refs[N_ROUTE_IN + 1:]
    f = _dot(cl_ref[...], ab_ref[:, :d]) + _dot(sl_ref[...], ab_ref[:, d:])
    y = _dot(f.astype(BF16), wf_ref[...])
    z = alpha * x_ref[...] + gate_ref[...] * y
    x_new = _normalize(z, LN_EPS) * g_ref[...] + b_ref[...]
    o_ref[...] = x_new
    _route_tokens(x_new, *route_in, *route_out)


def _fnet_b(cl, sl, ab, wf, x, mod_i, gate_j, ctx_row, ln_g, ln_b, alpha, rwt, rbias):
    b, l, d = x.shape
    tm = min(ROW_TILE, l)
    assert l % tm == 0
    r_in, r_args, r_out, r_shape = _route_io(b, l, d, tm, mod_i, ctx_row, rwt, rbias)
    res = pl.pallas_call(
        functools.partial(_fnet_b_kernel, d=d, alpha=alpha),
        grid=(b, l // tm),
        in_specs=[
            pl.BlockSpec((tm, l), lambda bb, t: (t, 0)),
            pl.BlockSpec((tm, l), lambda bb, t: (t, 0)),
            pl.BlockSpec((None, l, 2 * d), lambda bb, t: (bb, 0, 0)),
            pl.BlockSpec((d, d), lambda bb, t: (0, 0)),
            pl.BlockSpec((None, tm, d), lambda bb, t: (bb, t, 0)),
            _mod_spec(gate_j, d, ctx_row, 0),
            _row_spec(d),
            _row_spec(d),
        ] + r_in,
        out_specs=[pl.BlockSpec((None, tm, d), lambda bb, t: (bb, t, 0))] + r_out,
        out_shape=[jax.ShapeDtypeStruct((b, l, d), F32)] + r_shape,
        compiler_params=_cparams(2),
        name="fnet_seq_dft_route",
    )(cl, sl, ab, wf, x, mod_i, ln_g, ln_b, *r_args)
    return res[0], res[1:]


def _dft_tables(n, scale):
    j = jnp.arange(n, dtype=jnp.int32)
    ang = ((j[:, None] * j[None, :]) % n).astype(F32) * (2.0 * math.pi / n)
    return jnp.cos(ang) * scale, jnp.sin(ang) * scale


def _route_tokens(x, sh_ref, sc_ref, rwt_ref, bias_ref, tri_ref, u_ref, te_ref, w_ref, rk_ref, cnt_ref):
    n_groups, topk_groups, top_k = N_EXPERT_GROUPS, TOPK_GROUPS, TOP_K
    u = _modulate(x, sh_ref[...], sc_ref[...])
    uh, ul = _split_bf16(u)
    u_ref[...] = _pack_halves(u)
    wh, wl = _split_bf16(rwt_ref[...])
    logits = _dot_nt(wh, uh) + _dot_nt(wl, uh) + _dot_nt(wh, ul)
    e, tm = logits.shape
    pg = e // n_groups
    neg = -jnp.inf
    scores = jax.nn.sigmoid(logits)
    sel = scores + bias_ref[...]
    sub = lax.broadcasted_iota(jnp.int32, (pg, tm), 0)
    groups = [sel[g * pg:(g + 1) * pg, :] for g in range(n_groups)]
    sgroups = [scores[g * pg:(g + 1) * pg, :] for g in range(n_groups)]

    gs_rows = []
    for s_g in groups:
        m1 = jnp.max(s_g, axis=0, keepdims=True)
        first = jnp.min(jnp.where(s_g == m1, sub, pg), axis=0, keepdims=True)
        m2 = jnp.max(jnp.where(sub == first, neg, s_g), axis=0, keepdims=True)
        gs_rows.append(m1 + m2)
    gs = jnp.concatenate(gs_rows, axis=0)
    gidx = lax.broadcasted_iota(jnp.int32, (n_groups, tm), 0)
    ok = jnp.zeros((n_groups, tm), F32)
    for _ in range(topk_groups):
        m = jnp.max(gs, axis=0, keepdims=True)
        first = jnp.min(jnp.where(gs == m, gidx, n_groups), axis=0, keepdims=True)
        hit = gidx == first
        ok = jnp.where(hit, 1.0, ok)
        gs = jnp.where(hit, neg, gs)

    cur = [jnp.where(ok[g:g + 1, :] > 0.0, groups[g], neg) for g in range(n_groups)]
    eidx = [sub + g * pg for g in range(n_groups)]
    chosen = [jnp.zeros((pg, tm), F32) for _ in range(n_groups)]
    e_rows, s_rows = [], []
    for _ in range(top_k):
        m = jnp.max(functools.reduce(jnp.maximum, cur), axis=0, keepdims=True)
        cand = functools.reduce(jnp.minimum, [jnp.where(cur[g] == m, eidx[g], e) for g in range(n_groups)])
        first = jnp.min(cand, axis=0, keepdims=True)
        picked = jnp.zeros((pg, tm), F32)
        for g in range(n_groups):
            hit = eidx[g] == first
            picked = picked + jnp.where(hit, sgroups[g], 0.0)
            chosen[g] = jnp.where(hit, 1.0, chosen[g])
            cur[g] = jnp.where(hit, neg, cur[g])
        e_rows.append(first)
        s_rows.append(jnp.sum(picked, axis=0, keepdims=True))
    w = jnp.concatenate(s_rows, axis=0)
    te_ref[...] = jnp.concatenate(e_rows, axis=0)
    w_ref[...] = w / jnp.sum(w, axis=0, keepdims=True) * ROUTED_SCALE

    sel_mask = jnp.concatenate(chosen, axis=0).astype(BF16)
    before = _dot(sel_mask, tri_ref[...])
    rk_rows = []
    for k in range(top_k):
        acc = jnp.zeros((pg, tm), F32)
        for g in range(n_groups):
            acc = acc + jnp.where(eidx[g] == e_rows[k], before[g * pg:(g + 1) * pg, :], 0.0)
        rk_rows.append(jnp.sum(acc, axis=0, keepdims=True))
    rk_ref[...] = jnp.concatenate(rk_rows, axis=0).astype(jnp.int32)
    cnt_ref[...] = _dot_nt(jnp.ones((8, tm), BF16), sel_mask)


N_ROUTE_IN = 5
N_ROUTE_OUT = 5


def _route_io(b, l, d, tm, mod_i, ctx_row, rwt, bias):
    e = rwt.shape[0]
    assert l % tm == 0 and e % N_EXPERT_GROUPS == 0
    nt = l // tm
    t = b * l
    row = lax.broadcasted_iota(jnp.int32, (tm, tm), 0)
    col = lax.broadcasted_iota(jnp.int32, (tm, tm), 1)
    tri = (row < col).astype(BF16)
    in_specs = [
        _mod_spec(3, d, ctx_row, 0),
        _mod_spec(4, d, ctx_row, 0),
        pl.BlockSpec((e, d), lambda bb, tt: (0, 0)),
        pl.BlockSpec((e, 1), lambda bb, tt: (0, 0)),
        pl.BlockSpec((tm, tm), lambda bb, tt: (0, 0)),
    ]
    args = [mod_i, mod_i, rwt, bias.reshape(e, 1).astype(F32), tri]
    tok_spec = pl.BlockSpec((TOP_K, tm), lambda bb, tt: (0, bb * nt + tt))
    out_specs = [
        pl.BlockSpec((None, tm, d // 2), lambda bb, tt: (bb, tt, 0)),
        tok_spec, tok_spec, tok_spec,
        pl.BlockSpec((None, 8, e), lambda bb, tt: (bb * nt + tt, 0, 0)),
    ]
    out_shape = [
        jax.ShapeDtypeStruct((b, l, d // 2), I32),
        jax.ShapeDtypeStruct((TOP_K, t), jnp.int32),
        jax.ShapeDtypeStruct((TOP_K, t), F32),
        jax.ShapeDtypeStruct((TOP_K, t), jnp.int32),
        jax.ShapeDtypeStruct((b * nt, 8, e), F32),
    ]
    return in_specs, args, out_specs, out_shape


def _expert_layout(cnt, tmb, n_blocks):
    cnt_i = cnt[:, 0, :].astype(jnp.int32)
    e = cnt_i.shape[1]
    counts = cnt_i.sum(axis=0)
    padded = (counts + tmb - 1) // tmb * tmb
    pend = jnp.cumsum(padded)
    base = (pend - padded)[None, :] + jnp.cumsum(cnt_i, axis=0) - cnt_i
    n_used = (pend[-1] // tmb).astype(jnp.int32)
    blk = jnp.arange(n_blocks, dtype=jnp.int32)
    block_e = jnp.sum((blk[:, None] * tmb >= pend[None, :]).astype(jnp.int32), axis=1)
    last_e = jnp.sum((jnp.maximum(n_used - 1, 0) * tmb >= pend).astype(jnp.int32))
    block_e = jnp.clip(jnp.where(blk < n_used, block_e, last_e), 0, e - 1).astype(jnp.int32)
    seg_end = (pend - padded + counts)[block_e]
    block_valid = jnp.clip(seg_end - blk * tmb, 0, tmb).astype(jnp.int32)
    return base.reshape(-1).astype(jnp.int32), block_e, block_valid, n_used.reshape(1)


def _pos_kernel(base_ref, te_ref, rk_ref, pos_ref, *, n_experts):
    i = pl.program_id(0)
    te = te_ref[...]
    pos = rk_ref[...]
    for e in range(n_experts):
        pos = pos + jnp.where(te == e, base_ref[i * n_experts + e], 0)
    pos_ref[...] = pos


def _positions(base, top_e, rank, n_experts, tm):
    k, t = top_e.shape
    spec = pl.BlockSpec((k, tm), lambda i, base_ref: (0, i))
    return pl.pallas_call(
        functools.partial(_pos_kernel, n_experts=n_experts),
        grid_spec=pltpu.PrefetchScalarGridSpec(
            num_scalar_prefetch=1, grid=(t // tm,), in_specs=[spec, spec], out_specs=spec),
        out_shape=jax.ShapeDtypeStruct((k, t), jnp.int32),
        compiler_params=_cparams(1),
        name="moe_positions",
    )(base, top_e, rank)


def _sc_dispatch(rows, pos3, cap):
    t, w = rows.shape
    n_chunks, top_k, n = pos3.shape
    n_workers = SC_CORES * SC_SUBCORES
    assert n_chunks * n == t and n <= LANES
    per_worker = -(-n_chunks // n_workers)
    mesh = plsc.VectorSubcoreMesh(core_axis_name="core", subcore_axis_name="subcore",
                                  num_cores=SC_CORES, num_subcores=SC_SUBCORES)

    def body(rows_hbm, pos_hbm, out_hbm, idx_v, rows_v, sem):
        wid = lax.axis_index("subcore") * SC_CORES + lax.axis_index("core")

        @pl.loop(0, per_worker)
        def _(j):
            c = wid * per_worker + j

            @pl.when(c < n_chunks)
            def _():
                pltpu.sync_copy(pos_hbm.at[c], idx_v)
                pltpu.sync_copy(rows_hbm.at[pl.ds(c * n, n)], rows_v)
                copies = [pltpu.async_copy(rows_v, out_hbm.at[idx_v.at[k]], sem) for k in range(top_k)]
                for cp in copies:
                    cp.wait()

    return pl.kernel(
        body,
        out_type=jax.ShapeDtypeStruct((cap, w), I32),
        mesh=mesh,
        scratch_types=[pltpu.VMEM((top_k, n), I32), pltpu.VMEM((n, w), I32), pltpu.SemaphoreType.DMA],
        name="moe_dispatch_sc",
    )(rows, pos3)


def _sc_combine(rows, pos3, wts3):
    cap, w = rows.shape
    n_chunks, top_k, n = pos3.shape
    t = n_chunks * n
    n_workers = SC_CORES * SC_SUBCORES
    assert n <= SC_LANES and w % SC_LANES == 0
    per_worker = -(-n_chunks // n_workers)
    mesh = plsc.VectorSubcoreMesh(core_axis_name="core", subcore_axis_name="subcore",
                                  num_cores=SC_CORES, num_subcores=SC_SUBCORES)

    def body(rows_hbm, pos_hbm, wts_hbm, out_hbm, idx_v, wts_v, acc_v, *rest):
        bufs, sem = rest[:top_k], rest[top_k]
        wid = lax.axis_index("subcore") * SC_CORES + lax.axis_index("core")

        @pl.loop(0, per_worker)
        def _(it):
            c = wid * per_worker + it

            @pl.when(c < n_chunks)
            def _():
                pltpu.sync_copy(pos_hbm.at[c], idx_v)
                pltpu.sync_copy(wts_hbm.at[c], wts_v)
                gathers = [pltpu.async_copy(rows_hbm.at[idx_v.at[k]], bufs[k], sem) for k in range(top_k)]
                for g in gathers:
                    g.wait()

                @pl.loop(0, n)
                def _(j):
                    tok = jnp.full((SC_LANES,), j, I32)
                    wk = [plsc.load_gather(wts_v, [jnp.full((SC_LANES,), k, I32), tok]) for k in range(top_k)]

                    @plsc.parallel_loop(0, w // SC_LANES, unroll=4)
                    def _(g):
                        col = g * SC_LANES
                        hi = jnp.zeros((SC_LANES,), F32)
                        lo = jnp.zeros((SC_LANES,), F32)
                        for k in range(top_k):
                            words = bufs[k][j, pl.ds(col, SC_LANES)]
                            hi = hi + wk[k] * plsc.bitcast(words & jnp.int32(-65536), F32)
                            lo = lo + wk[k] * plsc.bitcast(lax.shift_left(words, 16), F32)
                        acc_v[j, pl.ds(col, SC_LANES)] = hi
                        acc_v[j, pl.ds(w + col, SC_LANES)] = lo

                pltpu.sync_copy(acc_v, out_hbm.at[pl.ds(c * n, n)])

    return pl.kernel(
        body,
        out_type=jax.ShapeDtypeStruct((t, 2 * w), F32),
        mesh=mesh,
        scratch_types=[pltpu.VMEM((top_k, n), I32), pltpu.VMEM((top_k, n), F32), pltpu.VMEM((n, 2 * w), F32)]
        + [pltpu.VMEM((n, w), I32) for _ in range(top_k)] + [pltpu.SemaphoreType.DMA],
        compiler_params=pltpu.CompilerParams(needs_layout_passes=False),
        name="moe_combine_sc",
    )(rows, pos3, wts3)


def _gmm_kernel(be_ref, bv_ref, nu_ref, x_ref, *refs, tmb, per_step):
    w_refs, o_ref = refs[:-1], refs[-1]
    i = pl.program_id(0)

    @pl.when(i * per_step < nu_ref[0])
    def _():
        for s in range(per_step):
            wg_ref, wu_ref, wd_ref = w_refs[3 * s:3 * s + 3]
            words = x_ref[s * tmb:(s + 1) * tmb, :]
            half = words.shape[1]
            live = lax.broadcasted_iota(I32, (tmb, half), 0) < bv_ref[i * per_step + s]
            xa, xb = _unpack_halves(jnp.where(live, words, 0))
            g = _dot(xa, wg_ref[:half, :]) + _dot(xb, wg_ref[half:, :])
            u = _dot(xa, wu_ref[:half, :]) + _dot(xb, wu_ref[half:, :])
            o_ref[s * tmb:(s + 1) * tmb, :] = _pack_halves(_dot((_silu(g) * u).astype(BF16), wd_ref[...]))


def _grouped_experts(x_sorted, block_e, block_valid, n_used, wg, wu, wd, tmb):
    cap, half = x_sorted.shape
    d = 2 * half
    ff = wg.shape[2]
    n_blocks = cap // tmb
    per_step = MOE_BLOCKS_PER_STEP if n_blocks % MOE_BLOCKS_PER_STEP == 0 else 1
    n_steps = n_blocks // per_step

    def row_map(i, be, bv, nu):
        return (jnp.minimum(i, (nu[0] + per_step - 1) // per_step - 1), 0)

    w_specs = []
    for s in range(per_step):
        w_map = functools.partial(lambda i, be, bv, nu, s: (be[i * per_step + s], 0, 0), s=s)
        w_specs += [pl.BlockSpec((None, d, ff), w_map), pl.BlockSpec((None, d, ff), w_map),
                    pl.BlockSpec((None, ff, d), w_map)]

    return pl.pallas_call(
        functools.partial(_gmm_kernel, tmb=tmb, per_step=per_step),
        grid_spec=pltpu.PrefetchScalarGridSpec(
            num_scalar_prefetch=3,
            grid=(n_steps,),
            in_specs=[pl.BlockSpec((per_step * tmb, half), row_map)] + w_specs,
            out_specs=pl.BlockSpec((per_step * tmb, half), row_map),
        ),
        out_shape=jax.ShapeDtypeStruct((cap, half), I32),
        compiler_params=_cparams(1),
        name="moe_experts",
    )(block_e, block_valid, n_used, x_sorted, *([wg, wu, wd] * per_step))


def _moe_out_kernel(u_ref, r_ref, sg_ref, su_ref, sd_ref, x_ref, gate_ref, g_ref, b_ref, o_ref, *, alpha):
    ua, ub = _unpack_halves(u_ref[...])
    half = ua.shape[1]
    g = _dot(ua, sg_ref[:half, :]) + _dot(ub, sg_ref[half:, :])
    s = _dot(ua, su_ref[:half, :]) + _dot(ub, su_ref[half:, :])
    y = r_ref[...] + _dot((_silu(g) * s).astype(BF16), sd_ref[...])
    z = alpha * x_ref[...] + gate_ref[...] * y
    o_ref[...] = _normalize(z, LN_EPS) * g_ref[...] + b_ref[...]


def _moe_out(u2, routed, sg, su, sd, x, mod_i, ctx_row, ln_g, ln_b, alpha):
    b, l, d = x.shape
    ff = sg.shape[1]
    tm = min(ROW_TILE, l)
    assert l % tm == 0
    tok = pl.BlockSpec((None, tm, d), lambda bb, t: (bb, t, 0))
    return pl.pallas_call(
        functools.partial(_moe_out_kernel, alpha=alpha),
        grid=(b, l // tm),
        in_specs=[
            pl.BlockSpec((None, tm, d // 2), lambda bb, t: (bb, t, 0)),
            tok,
            pl.BlockSpec((d, ff), lambda bb, t: (0, 0)),
            pl.BlockSpec((d, ff), lambda bb, t: (0, 0)),
            pl.BlockSpec((ff, d), lambda bb, t: (0, 0)),
            tok,
            _mod_spec(5, d, ctx_row, 0),
            _row_spec(d),
            _row_spec(d),
        ],
        out_specs=tok,
        out_shape=jax.ShapeDtypeStruct((b, l, d), F32),
        compiler_params=_cparams(2),
        name="moe_shared_post",
    )(u2, routed, sg, su, sd, x, mod_i, ln_g, ln_b)


def _moe_dispatch(shape, routing, n_experts):
    b, l, d = shape
    t = b * l
    tmb = MOE_ROW_TILE
    n_blocks = -(-(t * TOP_K) // tmb) + n_experts
    u2, top_e, wts, rank, cnt = routing
    base, block_e, block_valid, n_used = _expert_layout(cnt, tmb, n_blocks)
    pos = _positions(base, top_e, rank, n_experts, min(ROW_TILE, l))
    pos_d = pos.reshape(TOP_K, t // DISPATCH_CHUNK, DISPATCH_CHUNK).transpose(1, 0, 2)
    pos_c = pos.reshape(TOP_K, t // COMBINE_CHUNK, COMBINE_CHUNK).transpose(1, 0, 2)
    wts_c = wts.reshape(TOP_K, t // COMBINE_CHUNK, COMBINE_CHUNK).transpose(1, 0, 2)
    x_sorted = _sc_dispatch(u2.reshape(t, d // 2), pos_d, n_blocks * tmb)
    return x_sorted, block_e, block_valid, n_used, pos_c, wts_c


def _moe_experts(shape, dispatched, wg, wu, wd):
    x_sorted, block_e, block_valid, n_used, pos_c, wts_c = dispatched
    y_sorted = _grouped_experts(x_sorted, block_e, block_valid, n_used, wg, wu, wd, MOE_ROW_TILE)
    return y_sorted, _sc_combine(y_sorted, pos_c, wts_c).reshape(shape)


def _after(token, value):
    if token is None:
        return value
    return lax.optimization_barrier((token, value))[1]


def _rope_tables(l, hd, n_q_heads, n_k_heads, rope):
    axis_rot = hd // 2
    qscale = hd ** -0.5
    if rope:
        t = jnp.arange(l, dtype=jnp.int32)
        r = (t // GRID_W).astype(F32)
        col = (t % GRID_W).astype(F32)
        inv = ROPE_BASE ** (-jnp.arange(0, axis_rot, 2, dtype=F32) / axis_rot)
        ar = r[:, None] * inv
        ac = col[:, None] * inv
        ang = jnp.concatenate([ar, ar, ac, ac], axis=-1)
        cos, sin = jnp.cos(ang), jnp.sin(ang)
    else:
        cos, sin = jnp.ones((l, hd), F32), jnp.zeros((l, hd), F32)
    cos_t = jnp.concatenate([jnp.tile(cos, (1, n_q_heads)) * qscale, jnp.tile(cos, (1, n_k_heads))], axis=1)
    sin_t = jnp.concatenate([jnp.tile(sin, (1, n_q_heads)) * qscale, jnp.tile(sin, (1, n_k_heads))], axis=1)
    return cos_t, sin_t


def _rot_columns(w, hd):
    d, n = w.shape
    q = hd // 4
    w4 = w.reshape(d, n // (2 * q), 2, q)
    return jnp.stack([-w4[:, :, 1, :], w4[:, :, 0, :]], axis=2).reshape(d, n)


def kernel(x, c, ctx, c_ctx, w_ada, b_ada, ln_g, ln_b, attn_w_qkv, attn_w_o, attn_sinks, fnet_w,
           router_w, router_bias, exp_w_gate, exp_w_up, exp_w_down, sh_w_gate, sh_w_up, sh_w_down):
    b, l, d = x.shape
    cl = ctx.shape[1]
    depth = w_ada.shape[0]
    n_heads = attn_sinks.shape[1]
    hd = d // n_heads
    qw = n_heads * hd
    kvw = N_KV_HEADS * hd
    alpha = (2.0 * depth) ** 0.25
    gc = d // F_GROUPS

    mp = -(-(b + 1) // 16) * 16
    cvec = jnp.concatenate([c, c_ctx[None, :], jnp.zeros((mp - b - 1, d), F32)], axis=0)
    mod = _ada_all(cvec, w_ada, b_ada).reshape(depth, mp, 6, 1, d)

    def run_stream(x, h, mod):
        b = x.shape[0]
        token = yield None
        x = _after(token, x)
        for i in range(depth):
            kind = i % N_MIXERS
            j = i // N_MIXERS
            update_ctx = any((m % N_MIXERS) == 0 for m in range(i + 1, depth))
            mod_i = mod[i]
            g1 = ln_g[i, 0].reshape(1, d)
            b1 = ln_b[i, 0].reshape(1, d)
            g2 = ln_g[i, 1].reshape(1, d)
            b2 = ln_b[i, 1].reshape(1, d)
            rwt, rbias = moe_w[i][:2]

            if kind == 0:
                w_all, w_o = attn_w[j]
                q, k, v = _qkv_proj(x, mod_i, None, w_all, cos_l, sin_l, qw, kvw)
                q_c, k_c, v_c = _qkv_proj(h, mod_i, b, w_all, cos_c, sin_c, qw, kvw)
                o = _attention(q, k, v, k_c, v_c, attn_sinks[j], True)
                x, x_route = _proj_post(o, w_o, x, mod_i, 2, None, g1, b1, alpha, rwt, rbias)
                if update_ctx:
                    o_c = _attention(q_c, None, None, k_c, v_c, attn_sinks[j], False)
                    h, h_route = _proj_post(o_c, w_o, h, mod_i, 2, b, g1, b1, alpha, rwt, rbias)
            else:
                streams = [(x, None)] + ([(h, b)] if update_ctx else [])
                outs = []
                for s, ctx_row in streams:
                    c_l, s_l = dft_seq[s.shape[1]]
                    ab = _fnet_a(s, mod_i, ctx_row, cs)
                    outs.append(_fnet_b(c_l, s_l, ab, fnet_wb[j], s, mod_i, 2, ctx_row, g1, b1, alpha, rwt, rbias))
                x, x_route = outs[0]
                if update_ctx:
                    h, h_route = outs[1]

            wg, wu, wd, sg, su, sd = moe_w[i][2:]
            n_experts = rwt.shape[0]
            x_disp = _moe_dispatch(x.shape, x_route, n_experts)
            h_disp = _moe_dispatch(h.shape, h_route, n_experts) if update_ctx else None
            token = yield x
            x_disp = (_after(token, x_disp[0]),) + x_disp[1:]
            y_sorted, x_routed = _moe_experts(x.shape, x_disp, wg, wu, wd)
            h_routed = _moe_experts(h.shape, h_disp, wg, wu, wd)[1] if update_ctx else None
            token = yield y_sorted
            x_routed = _after(token, x_routed)
            x = _moe_out(x_route[0], x_routed, sg, su, sd, x, mod_i, None, g2, b2, alpha)
            if update_ctx:
                h = _moe_out(h_route[0], h_routed, sg, su, sd, h, mod_i, b, g2, b2, alpha)
        return x

    attn_w = []
    for j in range(attn_w_qkv.shape[0]):
        w = attn_w_qkv[j]
        w_all = jnp.concatenate([w, _rot_columns(w[:, :qw + kvw], hd)], axis=1).astype(BF16)
        attn_w.append((w_all, attn_w_o[j].astype(BF16)))
    cos_l, sin_l = _rope_tables(l, hd, n_heads, N_KV_HEADS, True)
    cos_c, sin_c = _rope_tables(cl, hd, n_heads, N_KV_HEADS, False)
    fnet_wb = [fnet_w[j].astype(BF16) for j in range(fnet_w.shape[0])]
    cc, sc = _dft_tables(gc, gc ** -0.5)
    cs = jnp.concatenate([cc, sc], axis=1).astype(BF16)
    dft_seq = {}
    for ls in (l, cl):
        c_l, s_l = _dft_tables(ls, ls ** -0.5)
        dft_seq[ls] = (c_l.astype(BF16), (-s_l).astype(BF16))
    moe_w = [(router_w[i].T, router_bias[i], exp_w_gate[i].astype(BF16), exp_w_up[i].astype(BF16),
              exp_w_down[i].astype(BF16), sh_w_gate[i].astype(BF16), sh_w_up[i].astype(BF16),
              sh_w_down[i].astype(BF16)) for i in range(depth)]

    n_streams = N_STREAMS if b % N_STREAMS == 0 else 1
    bs = b // n_streams
    gens = []
    for s in range(n_streams):
        mod_s = jnp.concatenate([mod[:, s * bs:(s + 1) * bs], mod[:, b:b + 1]], axis=1)
        gens.append(run_stream(x[s * bs:(s + 1) * bs], ctx[s * bs:(s + 1) * bs], mod_s))
    for g in gens:
        next(g)
    outs = [None] * n_streams
    token = None
    while any(o is None for o in outs):
        for s, g in enumerate(gens):
            if outs[s] is None:
                try:
                    token = g.send(token)
                except StopIteration as done:
                    outs[s] = token = done.value
    return outs[0] if n_streams == 1 else jnp.concatenate(outs, axis=0)
```

```python
import functools
import math

import jax
import jax.numpy as jnp
from jax import lax
from jax.experimental import pallas as pl
from jax.experimental.pallas import tpu as pltpu
from jax.experimental.pallas import tpu_sc as plsc

F32 = jnp.float32
BF16 = jnp.bfloat16
I32 = jnp.int32

N_KV_HEADS = 4
WINDOW = 128
GRID_W = 64
ROPE_BASE = 10000.0
F_GROUPS = 4
TOP_K = 8
N_EXPERT_GROUPS = 8
TOPK_GROUPS = 4
ROUTED_SCALE = 2.5
N_MIXERS = 2
LN_EPS = 1e-5
MOD_EPS = 1e-6

LANES = 128
VMEM_LIMIT_BYTES = 52 * 1024 * 1024
ROW_TILE = 512
QKV_ROW_TILE = 256
Q_TILE = 128
ATTN_ROW_CHUNK = 32
ATTN_UNROLL = 16
MOE_ROW_TILE = 512
MOE_BLOCKS_PER_STEP = 2
DISPATCH_CHUNK = 128
COMBINE_CHUNK = 16
N_STREAMS = 2
SC_CORES = 2
SC_SUBCORES = 16
SC_LANES = 16
NEG_BIG = -1e30


def _cparams(n_axes):
    return pltpu.CompilerParams(dimension_semantics=("arbitrary",) * n_axes,
                                vmem_limit_bytes=VMEM_LIMIT_BYTES)


def _dot(a, b):
    return jnp.dot(a, b, preferred_element_type=F32)


def _dot_nt(a, b):
    return lax.dot_general(a, b, (((1,), (1,)), ((), ())), preferred_element_type=F32)


def _split_bf16(a):
    hi = a.astype(BF16)
    lo = (a - hi.astype(F32)).astype(BF16)
    return hi, lo


def _normalize(x, eps):
    mu = jnp.mean(x, axis=-1, keepdims=True)
    xc = x - mu
    var = jnp.mean(xc * xc, axis=-1, keepdims=True)
    return xc * lax.rsqrt(var + eps)


def _modulate(x, shift, scale):
    return _normalize(x, MOD_EPS) * (1.0 + scale) + shift


def _silu(x):
    return x * jax.nn.sigmoid(x)


def _pack_halves(x):
    n = x.shape[1] // 2
    r = x.astype(BF16).astype(F32)
    hi = pltpu.bitcast(r[:, :n], I32)
    lo = pltpu.bitcast(r[:, n:], I32)
    return hi | lax.shift_right_logical(lo, 16)


def _unpack_halves(w):
    a = pltpu.bitcast(w & jnp.int32(-65536), F32).astype(BF16)
    b = pltpu.bitcast(lax.shift_left(w, 16), F32).astype(BF16)
    return a, b


def _ada_kernel(c_ref, w_ref, b_ref, o_ref):
    s = _silu(c_ref[...])
    sh, sl = _split_bf16(s)
    wh, wl = _split_bf16(w_ref[...])
    o_ref[...] = _dot(sh, wh) + _dot(sl, wh) + _dot(sh, wl) + b_ref[...]


def _ada_all(cvec, w_ada, b_ada):
    depth, d, n = w_ada.shape
    mp = cvec.shape[0]
    tn = 1536
    assert n % tn == 0
    return pl.pallas_call(
        _ada_kernel,
        grid=(depth, n // tn),
        in_specs=[
            pl.BlockSpec((mp, d), lambda i, j: (0, 0)),
            pl.BlockSpec((None, d, tn), lambda i, j: (i, 0, j)),
            pl.BlockSpec((None, 1, tn), lambda i, j: (i, 0, j)),
        ],
        out_specs=pl.BlockSpec((None, mp, tn), lambda i, j: (i, 0, j)),
        out_shape=jax.ShapeDtypeStruct((depth, mp, n), F32),
        compiler_params=_cparams(2),
        name="ada_mod",
    )(cvec, w_ada, b_ada.reshape(depth, 1, n))


def _mod_spec(j, d, ctx_row, batch_axis):
    if ctx_row is None:
        return pl.BlockSpec((None, None, 1, d), lambda *g: (g[batch_axis], j, 0, 0))
    return pl.BlockSpec((None, None, 1, d), lambda *g: (ctx_row, j, 0, 0))


def _row_spec(d):
    return pl.BlockSpec((1, d), lambda *g: (0, 0))


def _qkv_kernel(x_ref, sh_ref, sc_ref, w_ref, cos_ref, sin_ref, q_ref, k_ref, v_ref, *, qw, kvw):
    u = _modulate(x_ref[...], sh_ref[...], sc_ref[...]).astype(BF16)
    r = _dot(u, w_ref[...])
    qk = r[:, :qw + kvw] * cos_ref[...] + r[:, qw + 2 * kvw:] * sin_ref[...]
    q_ref[...] = qk[:, :qw].astype(BF16)
    k_ref[...] = qk[:, qw:].astype(BF16)
    v_ref[...] = r[:, qw + kvw:qw + 2 * kvw].astype(BF16)


def _qkv_proj(x, mod_i, ctx_row, w_all, cos_t, sin_t, qw, kvw, window=None):
    first, b = window if window is not None else (0, x.shape[0])
    _, l, d = x.shape
    tm = min(QKV_ROW_TILE, l)
    assert l % tm == 0
    n_all = w_all.shape[1]
    return pl.pallas_call(
        functools.partial(_qkv_kernel, qw=qw, kvw=kvw),
        grid=(l // tm, b),
        in_specs=[
            pl.BlockSpec((None, tm, d), lambda t, bb: (bb + first, t, 0)),
            _mod_spec(0, d, ctx_row, 1),
            _mod_spec(1, d, ctx_row, 1),
            pl.BlockSpec((d, n_all), lambda t, bb: (0, 0)),
            pl.BlockSpec((tm, qw + kvw), lambda t, bb: (t, 0)),
            pl.BlockSpec((tm, qw + kvw), lambda t, bb: (t, 0)),
        ],
        out_specs=[
            pl.BlockSpec((None, tm, qw), lambda t, bb: (bb, t, 0)),
            pl.BlockSpec((None, tm, kvw), lambda t, bb: (bb, t, 0)),
            pl.BlockSpec((None, tm, kvw), lambda t, bb: (bb, t, 0)),
        ],
        out_shape=[
            jax.ShapeDtypeStruct((b, l, qw), BF16),
            jax.ShapeDtypeStruct((b, l, kvw), BF16),
            jax.ShapeDtypeStruct((b, l, kvw), BF16),
        ],
        compiler_params=_cparams(2),
        name="qkv_rope",
    )(x, mod_i, mod_i, w_all, cos_t, sin_t)


def _attn_kernel(sink_ref, q_ref, *refs, tq, seq, n_kv, group, hd, has_window):
    if has_window:
        k_ref, v_ref, kc_ref, vc_ref, o_ref, s_scr, p_scr, m_scr, bias_scr = refs
    else:
        kc_ref, vc_ref, o_ref, s_scr, p_scr, m_scr = refs
    q = q_ref[...]
    kc = kc_ref[...]
    vc = vc_ref[...]
    span = tq + 2 * WINDOW if has_window else 0
    rows = group * tq
    if has_window:
        q0 = pl.program_id(1) * tq
        start = pl.multiple_of(jnp.clip(q0 - WINDOW, 0, seq - span), LANES)
        kw = k_ref[pl.ds(start, span), :]
        vw = v_ref[pl.ds(start, span), :]
        qpos = q0 + lax.broadcasted_iota(jnp.int32, (tq, span), 0)
        kpos = start + lax.broadcasted_iota(jnp.int32, (tq, span), 1)
        bias_scr[...] = jnp.where(jnp.abs(qpos - kpos) <= WINDOW, 0.0, NEG_BIG)

    for h in range(n_kv):
        heads = [h * group + g for g in range(group)]
        qh = jnp.concatenate([q[:, j * hd:(j + 1) * hd] for j in heads], axis=0)
        if has_window:
            s_scr[h, :, :span] = _dot_nt(qh, kw[:, h * hd:(h + 1) * hd])
        s_scr[h, :, span:] = _dot_nt(qh, kc[:, h * hd:(h + 1) * hd])

    chunks_per_head = tq // ATTN_ROW_CHUNK
    n_chunks = rows // ATTN_ROW_CHUNK
    n_tiles = (span + kc.shape[0]) // LANES
    win_tiles = span // LANES

    def logit_tiles(h, r):
        row = pl.multiple_of(r * ATTN_ROW_CHUNK, ATTN_ROW_CHUNK)
        s = s_scr[h, pl.ds(row, ATTN_ROW_CHUNK), :]
        tiles = [s[:, i * LANES:(i + 1) * LANES] for i in range(n_tiles)]
        if has_window:
            brow = pl.multiple_of((r % chunks_per_head) * ATTN_ROW_CHUNK, ATTN_ROW_CHUNK)
            bias = bias_scr[pl.ds(brow, ATTN_ROW_CHUNK), :]
            tiles = [t + bias[:, i * LANES:(i + 1) * LANES] if i < win_tiles else t for i, t in enumerate(tiles)]
        return row, tiles

    for h in range(n_kv):
        def row_max(r, carry, h=h):
            row, tiles = logit_tiles(h, r)
            sink = sink_ref[h * group + r // chunks_per_head]
            m = jnp.max(functools.reduce(jnp.maximum, tiles), axis=-1, keepdims=True)
            m_scr[h, pl.ds(row, ATTN_ROW_CHUNK), :] = jnp.broadcast_to(jnp.maximum(m, sink),
                                                                     (ATTN_ROW_CHUNK, LANES))
            return carry

        lax.fori_loop(0, n_chunks, row_max, 0, unroll=ATTN_UNROLL)

    for h in range(n_kv):
        def probs(r, carry, h=h):
            row, tiles = logit_tiles(h, r)
            sink = sink_ref[h * group + r // chunks_per_head]
            m = m_scr[h, pl.ds(row, ATTN_ROW_CHUNK), :]
            es = [jnp.exp(t - m) for t in tiles]
            for i, e in enumerate(es):
                p_scr[h, pl.ds(row, ATTN_ROW_CHUNK), i * LANES:(i + 1) * LANES] = e.astype(BF16)
            den = jnp.sum(functools.reduce(jnp.add, es), axis=-1, keepdims=True) + jnp.exp(sink - m)
            m_scr[h, pl.ds(row, ATTN_ROW_CHUNK), :] = 1.0 / den
            return carry

        lax.fori_loop(0, n_chunks, probs, 0, unroll=ATTN_UNROLL)

    for h in range(n_kv):
        o = _dot(p_scr[h, :, span:], vc[:, h * hd:(h + 1) * hd])
        if has_window:
            o = o + _dot(p_scr[h, :, :span], vw[:, h * hd:(h + 1) * hd])
        o = o * m_scr[h, :, :hd]
        for g in range(group):
            j = h * group + g
            o_ref[:, j * hd:(j + 1) * hd] = o[g * tq:(g + 1) * tq, :].astype(o_ref.dtype)


def _attention(q, k, v, kc, vc, sinks, has_window):
    b, l, qw = q.shape
    c, kvw = kc.shape[1], kc.shape[2]
    hd = kvw // N_KV_HEADS
    group = qw // kvw
    tq = Q_TILE if has_window else l
    assert l % tq == 0
    if has_window:
        assert l >= tq + 2 * WINDOW
    kern = functools.partial(_attn_kernel, tq=tq, seq=l, n_kv=N_KV_HEADS, group=group, hd=hd,
                             has_window=has_window)
    in_specs = [pl.BlockSpec(memory_space=pltpu.SMEM),
                pl.BlockSpec((None, tq, qw), lambda bb, t: (bb, t, 0))]
    args = [sinks, q]
    if has_window:
        in_specs += [pl.BlockSpec((None, l, kvw), lambda bb, t: (bb, 0, 0)),
                     pl.BlockSpec((None, l, kvw), lambda bb, t: (bb, 0, 0))]
        args += [k, v]
    in_specs += [pl.BlockSpec((None, c, kvw), lambda bb, t: (bb, 0, 0)),
                 pl.BlockSpec((None, c, kvw), lambda bb, t: (bb, 0, 0))]
    args += [kc, vc]
    n_keys = c + (tq + 2 * WINDOW if has_window else 0)
    scratch = [pltpu.VMEM((N_KV_HEADS, group * tq, n_keys), F32),
               pltpu.VMEM((N_KV_HEADS, group * tq, n_keys), BF16),
               pltpu.VMEM((N_KV_HEADS, group * tq, LANES), F32)]
    if has_window:
        scratch.append(pltpu.VMEM((tq, tq + 2 * WINDOW), F32))
    return pl.pallas_call(
        kern,
        grid=(b, l // tq),
        in_specs=in_specs,
        out_specs=pl.BlockSpec((None, tq, qw), lambda bb, t: (bb, t, 0)),
        out_shape=jax.ShapeDtypeStruct((b, l, qw), BF16),
        scratch_shapes=scratch,
        compiler_params=_cparams(2),
        name="win_attn" if has_window else "ctx_attn",
    )(*args)


def _proj_post_kernel(a_ref, w_ref, x_ref, gate_ref, g_ref, b_ref, *refs, alpha):
    route_in, o_ref, route_out = refs[:N_ROUTE_IN], refs[N_ROUTE_IN], refs[N_ROUTE_IN + 1:]
    y = _dot(a_ref[...], w_ref[...])
    z = alpha * x_ref[...] + gate_ref[...] * y
    x_new = _normalize(z, LN_EPS) * g_ref[...] + b_ref[...]
    o_ref[...] = x_new
    _route_tokens(x_new, *route_in, *route_out)


def _proj_post(a, w, x, mod_i, gate_j, ctx_row, ln_g, ln_b, alpha, rwt, rbias, window=None):
    first, b = window if window is not None else (0, x.shape[0])
    _, l, d = x.shape
    ka = a.shape[2]
    tm = min(ROW_TILE, l)
    assert l % tm == 0
    r_in, r_args, r_out, r_shape = _route_io(b, l, d, tm, mod_i, ctx_row, rwt, rbias)
    res = pl.pallas_call(
        functools.partial(_proj_post_kernel, alpha=alpha),
        grid=(b, l // tm),
        in_specs=[
            pl.BlockSpec((None, tm, ka), lambda bb, t: (bb, t, 0)),
            pl.BlockSpec((ka, d), lambda bb, t: (0, 0)),
            pl.BlockSpec((None, tm, d), lambda bb, t: (bb + first, t, 0)),
            _mod_spec(gate_j, d, ctx_row, 0),
            _row_spec(d),
            _row_spec(d),
        ] + r_in,
        out_specs=[pl.BlockSpec((None, tm, d), lambda bb, t: (bb, t, 0))] + r_out,
        out_shape=[jax.ShapeDtypeStruct((b, l, d), F32)] + r_shape,
        compiler_params=_cparams(2),
        name="proj_post_route",
    )(a, w, x, mod_i, ln_g, ln_b, *r_args)
    return res[0], res[1:]


def _fnet_a_kernel(x_ref, sh_ref, sc_ref, cs_ref, o_ref, *, d, gc):
    u = _modulate(x_ref[...], sh_ref[...], sc_ref[...]).astype(BF16)
    cs = cs_ref[...]
    for g in range(d // gc):
        r = _dot(u[:, g * gc:(g + 1) * gc], cs)
        o_ref[:, g * gc:(g + 1) * gc] = r[:, :gc].astype(BF16)
        o_ref[:, d + g * gc:d + (g + 1) * gc] = r[:, gc:].astype(BF16)


def _fnet_a(x, mod_i, ctx_row, cs):
    b, l, d = x.shape
    gc = d // F_GROUPS
    tm = min(ROW_TILE, l)
    assert l % tm == 0
    return pl.pallas_call(
        functools.partial(_fnet_a_kernel, d=d, gc=gc),
        grid=(b, l // tm),
        in_specs=[
            pl.BlockSpec((None, tm, d), lambda bb, t: (bb, t, 0)),
            _mod_spec(0, d, ctx_row, 0),
            _mod_spec(1, d, ctx_row, 0),
            pl.BlockSpec((gc, 2 * gc), lambda bb, t: (0, 0)),
        ],
        out_specs=pl.BlockSpec((None, tm, 2 * d), lambda bb, t: (bb, t, 0)),
        out_shape=jax.ShapeDtypeStruct((b, l, 2 * d), BF16),
        compiler_params=_cparams(2),
        name="fnet_chan_dft",
    )(x, mod_i, mod_i, cs)


def _fnet_b_kernel(cl_ref, sl_ref, ab_ref, wf_ref, x_ref, gate_ref, g_ref, b_ref, *refs, d, alpha):
    route_in, o_ref, route_out = refs[:N_ROUTE_IN], refs[N_ROUTE_IN], refs[N_ROUTE_IN + 1:]
    f = _dot(cl_ref[...], ab_ref[:, :d]) + _dot(sl_ref[...], ab_ref[:, d:])
    y = _dot(f.astype(BF16), wf_ref[...])
    z = alpha * x_ref[...] + gate_ref[...] * y
    x_new = _normalize(z, LN_EPS) * g_ref[...] + b_ref[...]
    o_ref[...] = x_new
    _route_tokens(x_new, *route_in, *route_out)


def _fnet_b(cl, sl, ab, wf, x, mod_i, gate_j, ctx_row, ln_g, ln_b, alpha, rwt, rbias):
    b, l, d = x.shape
    tm = min(ROW_TILE, l)
    assert l % tm == 0
    r_in, r_args, r_out, r_shape = _route_io(b, l, d, tm, mod_i, ctx_row, rwt, rbias)
    res = pl.pallas_call(
        functools.partial(_fnet_b_kernel, d=d, alpha=alpha),
        grid=(b, l // tm),
        in_specs=[
            pl.BlockSpec((tm, l), lambda bb, t: (t, 0)),
            pl.BlockSpec((tm, l), lambda bb, t: (t, 0)),
            pl.BlockSpec((None, l, 2 * d), lambda bb, t: (bb, 0, 0)),
            pl.BlockSpec((d, d), lambda bb, t: (0, 0)),
            pl.BlockSpec((None, tm, d), lambda bb, t: (bb, t, 0)),
            _mod_spec(gate_j, d, ctx_row, 0),
            _row_spec(d),
            _row_spec(d),
        ] + r_in,
        out_specs=[pl.BlockSpec((None, tm, d), lambda bb, t: (bb, t, 0))] + r_out,
        out_shape=[jax.ShapeDtypeStruct((b, l, d), F32)] + r_shape,
        compiler_params=_cparams(2),
        name="fnet_seq_dft_route",
    )(cl, sl, ab, wf, x, mod_i, ln_g, ln_b, *r_args)
    return res[0], res[1:]


def _dft_tables(n, scale):
    j = jnp.arange(n, dtype=jnp.int32)
    ang = ((j[:, None] * j[None, :]) % n).astype(F32) * (2.0 * math.pi / n)
    return jnp.cos(ang) * scale, jnp.sin(ang) * scale


def _route_tokens(x, sh_ref, sc_ref, rwt_ref, bias_ref, tri_ref, u_ref, te_ref, w_ref, rk_ref, cnt_ref):
    n_groups, topk_groups, top_k = N_EXPERT_GROUPS, TOPK_GROUPS, TOP_K
    u = _modulate(x, sh_ref[...], sc_ref[...])
    uh, ul = _split_bf16(u)
    u_ref[...] = _pack_halves(u)
    wh, wl = _split_bf16(rwt_ref[...])
    logits = _dot_nt(wh, uh) + _dot_nt(wl, uh) + _dot_nt(wh, ul)
    e, tm = logits.shape
    pg = e // n_groups
    neg = -jnp.inf
    scores = jax.nn.sigmoid(logits)
    sel = scores + bias_ref[...]
    sub = lax.broadcasted_iota(jnp.int32, (pg, tm), 0)
    groups = [sel[g * pg:(g + 1) * pg, :] for g in range(n_groups)]
    sgroups = [scores[g * pg:(g + 1) * pg, :] for g in range(n_groups)]

    gs_rows = []
    for s_g in groups:
        m1 = jnp.max(s_g, axis=0, keepdims=True)
        first = jnp.min(jnp.where(s_g == m1, sub, pg), axis=0, keepdims=True)
        m2 = jnp.max(jnp.where(sub == first, neg, s_g), axis=0, keepdims=True)
        gs_rows.append(m1 + m2)
    gs = jnp.concatenate(gs_rows, axis=0)
    gidx = lax.broadcasted_iota(jnp.int32, (n_groups, tm), 0)
    ok = jnp.zeros((n_groups, tm), F32)
    for _ in range(topk_groups):
        m = jnp.max(gs, axis=0, keepdims=True)
        first = jnp.min(jnp.where(gs == m, gidx, n_groups), axis=0, keepdims=True)
        hit = gidx == first
        ok = jnp.where(hit, 1.0, ok)
        gs = jnp.where(hit, neg, gs)

    cur = [jnp.where(ok[g:g + 1, :] > 0.0, groups[g], neg) for g in range(n_groups)]
    eidx = [sub + g * pg for g in range(n_groups)]
    chosen = [jnp.zeros((pg, tm), F32) for _ in range(n_groups)]
    e_rows, s_rows = [], []
    for _ in range(top_k):
        m = jnp.max(functools.reduce(jnp.maximum, cur), axis=0, keepdims=True)
        cand = functools.reduce(jnp.minimum, [jnp.where(cur[g] == m, eidx[g], e) for g in range(n_groups)])
        first = jnp.min(cand, axis=0, keepdims=True)
        picked = jnp.zeros((pg, tm), F32)
        for g in range(n_groups):
            hit = eidx[g] == first
            picked = picked + jnp.where(hit, sgroups[g], 0.0)
            chosen[g] = jnp.where(hit, 1.0, chosen[g])
            cur[g] = jnp.where(hit, neg, cur[g])
        e_rows.append(first)
        s_rows.append(jnp.sum(picked, axis=0, keepdims=True))
    w = jnp.concatenate(s_rows, axis=0)
    te_ref[...] = jnp.concatenate(e_rows, axis=0)
    w_ref[...] = w / jnp.sum(w, axis=0, keepdims=True) * ROUTED_SCALE

    sel_mask = jnp.concatenate(chosen, axis=0).astype(BF16)
    before = _dot(sel_mask, tri_ref[...])
    rk_rows = []
    for k in range(top_k):
        acc = jnp.zeros((pg, tm), F32)
        for g in range(n_groups):
            acc = acc + jnp.where(eidx[g] == e_rows[k], before[g * pg:(g + 1) * pg, :], 0.0)
        rk_rows.append(jnp.sum(acc, axis=0, keepdims=True))
    rk_ref[...] = jnp.concatenate(rk_rows, axis=0).astype(jnp.int32)
    cnt_ref[...] = _dot_nt(jnp.ones((8, tm), BF16), sel_mask)


N_ROUTE_IN = 5
N_ROUTE_OUT = 5


def _route_io(b, l, d, tm, mod_i, ctx_row, rwt, bias):
    e = rwt.shape[0]
    assert l % tm == 0 and e % N_EXPERT_GROUPS == 0
    nt = l // tm
    t = b * l
    row = lax.broadcasted_iota(jnp.int32, (tm, tm), 0)
    col = lax.broadcasted_iota(jnp.int32, (tm, tm), 1)
    tri = (row < col).astype(BF16)
    in_specs = [
        _mod_spec(3, d, ctx_row, 0),
        _mod_spec(4, d, ctx_row, 0),
        pl.BlockSpec((e, d), lambda bb, tt: (0, 0)),
        pl.BlockSpec((e, 1), lambda bb, tt: (0, 0)),
        pl.BlockSpec((tm, tm), lambda bb, tt: (0, 0)),
    ]
    args = [mod_i, mod_i, rwt, bias.reshape(e, 1).astype(F32), tri]
    tok_spec = pl.BlockSpec((TOP_K, tm), lambda bb, tt: (0, bb * nt + tt))
    out_specs = [
        pl.BlockSpec((None, tm, d // 2), lambda bb, tt: (bb, tt, 0)),
        tok_spec, tok_spec, tok_spec,
        pl.BlockSpec((None, 8, e), lambda bb, tt: (bb * nt + tt, 0, 0)),
    ]
    out_shape = [
        jax.ShapeDtypeStruct((b, l, d // 2), I32),
        jax.ShapeDtypeStruct((TOP_K, t), jnp.int32),
        jax.ShapeDtypeStruct((TOP_K, t), F32),
        jax.ShapeDtypeStruct((TOP_K, t), jnp.int32),
        jax.ShapeDtypeStruct((b * nt, 8, e), F32),
    ]
    return in_specs, args, out_specs, out_shape


def _expert_layout(cnt, tmb, n_blocks):
    cnt_i = cnt[:, 0, :].astype(jnp.int32)
    e = cnt_i.shape[1]
    counts = cnt_i.sum(axis=0)
    padded = (counts + tmb - 1) // tmb * tmb
    pend = jnp.cumsum(padded)
    base = (pend - padded)[None, :] + jnp.cumsum(cnt_i, axis=0) - cnt_i
    n_used = (pend[-1] // tmb).astype(jnp.int32)
    blk = jnp.arange(n_blocks, dtype=jnp.int32)
    block_e = jnp.sum((blk[:, None] * tmb >= pend[None, :]).astype(jnp.int32), axis=1)
    last_e = jnp.sum((jnp.maximum(n_used - 1, 0) * tmb >= pend).astype(jnp.int32))
    block_e = jnp.clip(jnp.where(blk < n_used, block_e, last_e), 0, e - 1).astype(jnp.int32)
    seg_end = (pend - padded + counts)[block_e]
    block_valid = jnp.clip(seg_end - blk * tmb, 0, tmb).astype(jnp.int32)
    return base.reshape(-1).astype(jnp.int32), block_e, block_valid, n_used.reshape(1)


def _pos_kernel(base_ref, te_ref, rk_ref, pos_ref, *, n_experts):
    i = pl.program_id(0)
    te = te_ref[...]
    pos = rk_ref[...]
    for e in range(n_experts):
        pos = pos + jnp.where(te == e, base_ref[i * n_experts + e], 0)
    pos_ref[...] = pos


def _positions(base, top_e, rank, n_experts, tm):
    k, t = top_e.shape
    spec = pl.BlockSpec((k, tm), lambda i, base_ref: (0, i))
    return pl.pallas_call(
        functools.partial(_pos_kernel, n_experts=n_experts),
        grid_spec=pltpu.PrefetchScalarGridSpec(
            num_scalar_prefetch=1, grid=(t // tm,), in_specs=[spec, spec], out_specs=spec),
        out_shape=jax.ShapeDtypeStruct((k, t), jnp.int32),
        compiler_params=_cparams(1),
        name="moe_positions",
    )(base, top_e, rank)


def _sc_dispatch(rows, pos3, cap):
    t, w = rows.shape
    n_chunks, top_k, n = pos3.shape
    n_workers = SC_CORES * SC_SUBCORES
    assert n_chunks * n == t and n <= LANES
    per_worker = -(-n_chunks // n_workers)
    mesh = plsc.VectorSubcoreMesh(core_axis_name="core", subcore_axis_name="subcore",
                                  num_cores=SC_CORES, num_subcores=SC_SUBCORES)

    def body(rows_hbm, pos_hbm, out_hbm, idx_v, rows_v, sem):
        wid = lax.axis_index("subcore") * SC_CORES + lax.axis_index("core")

        @pl.loop(0, per_worker)
        def _(j):
            c = wid * per_worker + j

            @pl.when(c < n_chunks)
            def _():
                pltpu.sync_copy(pos_hbm.at[c], idx_v)
                pltpu.sync_copy(rows_hbm.at[pl.ds(c * n, n)], rows_v)
                copies = [pltpu.async_copy(rows_v, out_hbm.at[idx_v.at[k]], sem) for k in range(top_k)]
                for cp in copies:
                    cp.wait()

    return pl.kernel(
        body,
        out_type=jax.ShapeDtypeStruct((cap, w), I32),
        mesh=mesh,
        scratch_types=[pltpu.VMEM((top_k, n), I32), pltpu.VMEM((n, w), I32), pltpu.SemaphoreType.DMA],
        name="moe_dispatch_sc",
    )(rows, pos3)


def _sc_combine(rows, pos3, wts3):
    cap, w = rows.shape
    n_chunks, top_k, n = pos3.shape
    t = n_chunks * n
    n_workers = SC_CORES * SC_SUBCORES
    assert n <= SC_LANES and w % SC_LANES == 0
    per_worker = -(-n_chunks // n_workers)
    mesh = plsc.VectorSubcoreMesh(core_axis_name="core", subcore_axis_name="subcore",
                                  num_cores=SC_CORES, num_subcores=SC_SUBCORES)

    def body(rows_hbm, pos_hbm, wts_hbm, out_hbm, idx_v, wts_v, acc_v, *rest):
        bufs, sem = rest[:top_k], rest[top_k]
        wid = lax.axis_index("subcore") * SC_CORES + lax.axis_index("core")

        @pl.loop(0, per_worker)
        def _(it):
            c = wid * per_worker + it

            @pl.when(c < n_chunks)
            def _():
                pltpu.sync_copy(pos_hbm.at[c], idx_v)
                pltpu.sync_copy(wts_hbm.at[c], wts_v)
                gathers = [pltpu.async_copy(rows_hbm.at[idx_v.at[k]], bufs[k], sem) for k in range(top_k)]
                for g in gathers:
                    g.wait()

                @pl.loop(0, n)
                def _(j):
                    tok = jnp.full((SC_LANES,), j, I32)
                    wk = [plsc.load_gather(wts_v, [jnp.full((SC_LANES,), k, I32), tok]) for k in range(top_k)]

                    @plsc.parallel_loop(0, w // SC_LANES, unroll=4)
                    def _(g):
                        col = g * SC_LANES
                        hi = jnp.zeros((SC_LANES,), F32)
                        lo = jnp.zeros((SC_LANES,), F32)
                        for k in range(top_k):
                            words = bufs[k][j, pl.ds(col, SC_LANES)]
                            hi = hi + wk[k] * plsc.bitcast(words & jnp.int32(-65536), F32)
                            lo = lo + wk[k] * plsc.bitcast(lax.shift_left(words, 16), F32)
                        acc_v[j, pl.ds(col, SC_LANES)] = hi
                        acc_v[j, pl.ds(w + col, SC_LANES)] = lo

                pltpu.sync_copy(acc_v, out_hbm.at[pl.ds(c * n, n)])

    return pl.kernel(
        body,
        out_type=jax.ShapeDtypeStruct((t, 2 * w), F32),
        mesh=mesh,
        scratch_types=[pltpu.VMEM((top_k, n), I32), pltpu.VMEM((top_k, n), F32), pltpu.VMEM((n, 2 * w), F32)]
        + [pltpu.VMEM((n, w), I32) for _ in range(top_k)] + [pltpu.SemaphoreType.DMA],
        compiler_params=pltpu.CompilerParams(needs_layout_passes=False),
        name="moe_combine_sc",
    )(rows, pos3, wts3)


def _gmm_kernel(be_ref, bv_ref, nu_ref, x_ref, *refs, tmb, per_step):
    w_refs, o_ref = refs[:-1], refs[-1]
    i = pl.program_id(0)

    @pl.when(i * per_step < nu_ref[0])
    def _():
        for s in range(per_step):
            wg_ref, wu_ref, wd_ref = w_refs[3 * s:3 * s + 3]
            words = x_ref[s * tmb:(s + 1) * tmb, :]
            half = words.shape[1]
            live = lax.broadcasted_iota(I32, (tmb, half), 0) < bv_ref[i * per_step + s]
            xa, xb = _unpack_halves(jnp.where(live, words, 0))
            g = _dot(xa, wg_ref[:half, :]) + _dot(xb, wg_ref[half:, :])
            u = _dot(xa, wu_ref[:half, :]) + _dot(xb, wu_ref[half:, :])
            o_ref[s * tmb:(s + 1) * tmb, :] = _pack_halves(_dot((_silu(g) * u).astype(BF16), wd_ref[...]))


def _grouped_experts(x_sorted, block_e, block_valid, n_used, wg, wu, wd, tmb):
    cap, half = x_sorted.shape
    d = 2 * half
    ff = wg.shape[2]
    n_blocks = cap // tmb
    per_step = MOE_BLOCKS_PER_STEP if n_blocks % MOE_BLOCKS_PER_STEP == 0 else 1
    n_steps = n_blocks // per_step

    def row_map(i, be, bv, nu):
        return (jnp.minimum(i, (nu[0] + per_step - 1) // per_step - 1), 0)

    w_specs = []
    for s in range(per_step):
        w_map = functools.partial(lambda i, be, bv, nu, s: (be[i * per_step + s], 0, 0), s=s)
        w_specs += [pl.BlockSpec((None, d, ff), w_map), pl.BlockSpec((None, d, ff), w_map),
                    pl.BlockSpec((None, ff, d), w_map)]

    return pl.pallas_call(
        functools.partial(_gmm_kernel, tmb=tmb, per_step=per_step),
        grid_spec=pltpu.PrefetchScalarGridSpec(
            num_scalar_prefetch=3,
            grid=(n_steps,),
            in_specs=[pl.BlockSpec((per_step * tmb, half), row_map)] + w_specs,
            out_specs=pl.BlockSpec((per_step * tmb, half), row_map),
        ),
        out_shape=jax.ShapeDtypeStruct((cap, half), I32),
        compiler_params=_cparams(1),
        name="moe_experts",
    )(block_e, block_valid, n_used, x_sorted, *([wg, wu, wd] * per_step))


def _moe_out_kernel(u_ref, r_ref, sg_ref, su_ref, sd_ref, x_ref, gate_ref, g_ref, b_ref, o_ref, *, alpha):
    ua, ub = _unpack_halves(u_ref[...])
    half = ua.shape[1]
    g = _dot(ua, sg_ref[:half, :]) + _dot(ub, sg_ref[half:, :])
    s = _dot(ua, su_ref[:half, :]) + _dot(ub, su_ref[half:, :])
    y = r_ref[...] + _dot((_silu(g) * s).astype(BF16), sd_ref[...])
    z = alpha * x_ref[...] + gate_ref[...] * y
    o_ref[...] = _normalize(z, LN_EPS) * g_ref[...] + b_ref[...]


def _moe_out_into_kernel(u_ref, r_ref, sg_ref, su_ref, sd_ref, x_ref, gate_ref, g_ref, b_ref, buffer_ref, o_ref,
                         *, alpha):
    del buffer_ref
    _moe_out_kernel(u_ref, r_ref, sg_ref, su_ref, sd_ref, x_ref, gate_ref, g_ref, b_ref, o_ref, alpha=alpha)


def _moe_out(u2, routed, sg, su, sd, x, mod_i, ctx_row, ln_g, ln_b, alpha, place=None):
    b, l, d = x.shape
    ff = sg.shape[1]
    tm = min(ROW_TILE, l)
    assert l % tm == 0
    first, total, buffer = place if place is not None else (0, b, None)
    tok = pl.BlockSpec((None, tm, d), lambda bb, t: (bb, t, 0))
    in_specs = [
        pl.BlockSpec((None, tm, d // 2), lambda bb, t: (bb, t, 0)),
        tok,
        pl.BlockSpec((d, ff), lambda bb, t: (0, 0)),
        pl.BlockSpec((d, ff), lambda bb, t: (0, 0)),
        pl.BlockSpec((ff, d), lambda bb, t: (0, 0)),
        tok,
        _mod_spec(5, d, ctx_row, 0),
        _row_spec(d),
        _row_spec(d),
    ]
    args = [u2, routed, sg, su, sd, x, mod_i, ln_g, ln_b]
    kern = functools.partial(_moe_out_kernel, alpha=alpha)
    aliases = {}
    if buffer is not None:
        in_specs.append(pl.BlockSpec(memory_space=pl.ANY))
        aliases = {len(args): 0}
        args.append(buffer)
        kern = functools.partial(_moe_out_into_kernel, alpha=alpha)
    return pl.pallas_call(
        kern,
        grid=(b, l // tm),
        in_specs=in_specs,
        out_specs=pl.BlockSpec((None, tm, d), lambda bb, t: (bb + first, t, 0)),
        out_shape=jax.ShapeDtypeStruct((total, l, d), F32),
        input_output_aliases=aliases,
        compiler_params=_cparams(2),
        name="moe_shared_post",
    )(*args)


def _moe_dispatch(shape, routing, n_experts):
    b, l, d = shape
    t = b * l
    tmb = MOE_ROW_TILE
    n_blocks = -(-(t * TOP_K) // tmb) + n_experts
    u2, top_e, wts, rank, cnt = routing
    base, block_e, block_valid, n_used = _expert_layout(cnt, tmb, n_blocks)
    pos = _positions(base, top_e, rank, n_experts, min(ROW_TILE, l))
    pos_d = pos.reshape(TOP_K, t // DISPATCH_CHUNK, DISPATCH_CHUNK).transpose(1, 0, 2)
    pos_c = pos.reshape(TOP_K, t // COMBINE_CHUNK, COMBINE_CHUNK).transpose(1, 0, 2)
    wts_c = wts.reshape(TOP_K, t // COMBINE_CHUNK, COMBINE_CHUNK).transpose(1, 0, 2)
    x_sorted = _sc_dispatch(u2.reshape(t, d // 2), pos_d, n_blocks * tmb)
    return x_sorted, block_e, block_valid, n_used, pos_c, wts_c


def _moe_experts(shape, dispatched, wg, wu, wd):
    x_sorted, block_e, block_valid, n_used, pos_c, wts_c = dispatched
    y_sorted = _grouped_experts(x_sorted, block_e, block_valid, n_used, wg, wu, wd, MOE_ROW_TILE)
    return y_sorted, _sc_combine(y_sorted, pos_c, wts_c).reshape(shape)


def _after(token, value):
    if token is None:
        return value
    return lax.optimization_barrier((token, value))[1]


def _rope_tables(l, hd, n_q_heads, n_k_heads, rope):
    axis_rot = hd // 2
    qscale = hd ** -0.5
    if rope:
        t = jnp.arange(l, dtype=jnp.int32)
        r = (t // GRID_W).astype(F32)
        col = (t % GRID_W).astype(F32)
        inv = ROPE_BASE ** (-jnp.arange(0, axis_rot, 2, dtype=F32) / axis_rot)
        ar = r[:, None] * inv
        ac = col[:, None] * inv
        ang = jnp.concatenate([ar, ar, ac, ac], axis=-1)
        cos, sin = jnp.cos(ang), jnp.sin(ang)
    else:
        cos, sin = jnp.ones((l, hd), F32), jnp.zeros((l, hd), F32)
    cos_t = jnp.concatenate([jnp.tile(cos, (1, n_q_heads)) * qscale, jnp.tile(cos, (1, n_k_heads))], axis=1)
    sin_t = jnp.concatenate([jnp.tile(sin, (1, n_q_heads)) * qscale, jnp.tile(sin, (1, n_k_heads))], axis=1)
    return cos_t, sin_t


def _rot_columns(w, hd):
    d, n = w.shape
    q = hd // 4
    w4 = w.reshape(d, n // (2 * q), 2, q)
    return jnp.stack([-w4[:, :, 1, :], w4[:, :, 0, :]], axis=2).reshape(d, n)


def kernel(x, c, ctx, c_ctx, w_ada, b_ada, ln_g, ln_b, attn_w_qkv, attn_w_o, attn_sinks, fnet_w,
           router_w, router_bias, exp_w_gate, exp_w_up, exp_w_down, sh_w_gate, sh_w_up, sh_w_down):
    b, l, d = x.shape
    cl = ctx.shape[1]
    depth = w_ada.shape[0]
    n_heads = attn_sinks.shape[1]
    hd = d // n_heads
    qw = n_heads * hd
    kvw = N_KV_HEADS * hd
    alpha = (2.0 * depth) ** 0.25
    gc = d // F_GROUPS

    mp = -(-(b + 1) // 16) * 16
    cvec = jnp.concatenate([c, c_ctx[None, :], jnp.zeros((mp - b - 1, d), F32)], axis=0)
    mod = _ada_all(cvec, w_ada, b_ada).reshape(depth, mp, 6, 1, d)

    def run_stream(x, h, mod, first, count):
        b = count
        token, _ = yield None
        x = _after(token, x)
        window = (first, count)
        for i in range(depth):
            kind = i % N_MIXERS
            j = i // N_MIXERS
            update_ctx = any((m % N_MIXERS) == 0 for m in range(i + 1, depth))
            mod_i = mod[i]
            g1 = ln_g[i, 0].reshape(1, d)
            b1 = ln_b[i, 0].reshape(1, d)
            g2 = ln_g[i, 1].reshape(1, d)
            b2 = ln_b[i, 1].reshape(1, d)
            rwt, rbias = moe_w[i][:2]

            if kind == 0:
                w_all, w_o = attn_w[j]
                q, k, v = _qkv_proj(x, mod_i, None, w_all, cos_l, sin_l, qw, kvw, window)
                q_c, k_c, v_c = _qkv_proj(h, mod_i, b, w_all, cos_c, sin_c, qw, kvw, window)
                o = _attention(q, k, v, k_c, v_c, attn_sinks[j], True)
                x, x_route = _proj_post(o, w_o, x, mod_i, 2, None, g1, b1, alpha, rwt, rbias, window)
                if update_ctx:
                    o_c = _attention(q_c, None, None, k_c, v_c, attn_sinks[j], False)
                    h, h_route = _proj_post(o_c, w_o, h, mod_i, 2, b, g1, b1, alpha, rwt, rbias, window)
            else:
                if window is not None:
                    x, h = x[first:first + count], h[first:first + count]
                streams = [(x, None)] + ([(h, b)] if update_ctx else [])
                outs = []
                for s, ctx_row in streams:
                    c_l, s_l = dft_seq[s.shape[1]]
                    ab = _fnet_a(s, mod_i, ctx_row, cs)
                    outs.append(_fnet_b(c_l, s_l, ab, fnet_wb[j], s, mod_i, 2, ctx_row, g1, b1, alpha, rwt, rbias))
                x, x_route = outs[0]
                if update_ctx:
                    h, h_route = outs[1]

            wg, wu, wd, sg, su, sd = moe_w[i][2:]
            n_experts = rwt.shape[0]
            x_disp = _moe_dispatch(x.shape, x_route, n_experts)
            h_disp = _moe_dispatch(h.shape, h_route, n_experts) if update_ctx else None
            window = None
            token, _ = yield x
            x_disp = (_after(token, x_disp[0]),) + x_disp[1:]
            y_sorted, x_routed = _moe_experts(x.shape, x_disp, wg, wu, wd)
            h_routed = _moe_experts(h.shape, h_disp, wg, wu, wd)[1] if update_ctx else None
            token, buffer = yield y_sorted
            x_routed = _after(token, x_routed)
            place = (first, total, buffer) if i == depth - 1 else None
            x = _moe_out(x_route[0], x_routed, sg, su, sd, x, mod_i, None, g2, b2, alpha, place)
            if update_ctx:
                h = _moe_out(h_route[0], h_routed, sg, su, sd, h, mod_i, b, g2, b2, alpha)
        return x

    attn_w = []
    for j in range(attn_w_qkv.shape[0]):
        w = attn_w_qkv[j]
        w_all = jnp.concatenate([w, _rot_columns(w[:, :qw + kvw], hd)], axis=1).astype(BF16)
        attn_w.append((w_all, attn_w_o[j].astype(BF16)))
    cos_l, sin_l = _rope_tables(l, hd, n_heads, N_KV_HEADS, True)
    cos_c, sin_c = _rope_tables(cl, hd, n_heads, N_KV_HEADS, False)
    fnet_wb = [fnet_w[j].astype(BF16) for j in range(fnet_w.shape[0])]
    cc, sc = _dft_tables(gc, gc ** -0.5)
    cs = jnp.concatenate([cc, sc], axis=1).astype(BF16)
    dft_seq = {}
    for ls in (l, cl):
        c_l, s_l = _dft_tables(ls, ls ** -0.5)
        dft_seq[ls] = (c_l.astype(BF16), (-s_l).astype(BF16))
    moe_w = [(router_w[i].T, router_bias[i], exp_w_gate[i].astype(BF16), exp_w_up[i].astype(BF16),
              exp_w_down[i].astype(BF16), sh_w_gate[i].astype(BF16), sh_w_up[i].astype(BF16),
              sh_w_down[i].astype(BF16)) for i in range(depth)]

    n_streams = N_STREAMS if b % N_STREAMS == 0 else 1
    bs = b // n_streams
    total = b
    gens = []
    for s in range(n_streams):
        mod_s = jnp.concatenate([mod[:, s * bs:(s + 1) * bs], mod[:, b:b + 1]], axis=1)
        gens.append(run_stream(x, ctx, mod_s, s * bs, bs))
    for g in gens:
        next(g)
    running = list(range(n_streams))
    token = result = None
    while running:
        for s in list(running):
            try:
                token = gens[s].send((token, result))
            except StopIteration as done:
                token = result = done.value
                running.remove(s)
    return result
```

```python
import functools
import math

import jax
import jax.numpy as jnp
from jax import lax
from jax.experimental import pallas as pl
from jax.experimental.pallas import tpu as pltpu
from jax.experimental.pallas import tpu_sc as plsc

F32 = jnp.float32
BF16 = jnp.bfloat16
I32 = jnp.int32

N_KV_HEADS = 4
WINDOW = 128
GRID_W = 64
ROPE_BASE = 10000.0
F_GROUPS = 4
TOP_K = 8
N_EXPERT_GROUPS = 8
TOPK_GROUPS = 4
ROUTED_SCALE = 2.5
N_MIXERS = 2
LN_EPS = 1e-5
MOD_EPS = 1e-6

LANES = 128
VMEM_LIMIT_BYTES = 52 * 1024 * 1024
ROW_TILE = 512
QKV_ROW_TILE = 256
Q_TILE = 128
ATTN_ROW_CHUNK = 32
ATTN_UNROLL = 16
MOE_ROW_TILE = 512
MOE_BLOCKS_PER_STEP = 2
DISPATCH_CHUNK = 128
COMBINE_CHUNK = 16
N_STREAMS = 2
SC_CORES = 2
SC_SUBCORES = 16
SC_LANES = 16
NEG_BIG = -1e30


def _cparams(n_axes):
    return pltpu.CompilerParams(dimension_semantics=("arbitrary",) * n_axes,
                                vmem_limit_bytes=VMEM_LIMIT_BYTES)


def _dot(a, b):
    return jnp.dot(a, b, preferred_element_type=F32)


def _dot_nt(a, b):
    return lax.dot_general(a, b, (((1,), (1,)), ((), ())), preferred_element_type=F32)


def _split_bf16(a):
    hi = a.astype(BF16)
    lo = (a - hi.astype(F32)).astype(BF16)
    return hi, lo


def _normalize(x, eps):
    mu = jnp.mean(x, axis=-1, keepdims=True)
    xc = x - mu
    var = jnp.mean(xc * xc, axis=-1, keepdims=True)
    return xc * lax.rsqrt(var + eps)


def _modulate(x, shift, scale):
    return _normalize(x, MOD_EPS) * (1.0 + scale) + shift


def _silu(x):
    return x * jax.nn.sigmoid(x)


def _pack_halves(x):
    n = x.shape[1] // 2
    r = x.astype(BF16).astype(F32)
    hi = pltpu.bitcast(r[:, :n], I32)
    lo = pltpu.bitcast(r[:, n:], I32)
    return hi | lax.shift_right_logical(lo, 16)


def _unpack_halves(w):
    a = pltpu.bitcast(w & jnp.int32(-65536), F32).astype(BF16)
    b = pltpu.bitcast(lax.shift_left(w, 16), F32).astype(BF16)
    return a, b


def _ada_kernel(c_ref, w_ref, b_ref, o_ref):
    s = _silu(c_ref[...])
    sh, sl = _split_bf16(s)
    wh, wl = _split_bf16(w_ref[...])
    o_ref[...] = _dot(sh, wh) + _dot(sl, wh) + _dot(sh, wl) + b_ref[...]


def _ada_all(cvec, w_ada, b_ada):
    depth, d, n = w_ada.shape
    mp = cvec.shape[0]
    tn = 1536
    assert n % tn == 0
    return pl.pallas_call(
        _ada_kernel,
        grid=(depth, n // tn),
        in_specs=[
            pl.BlockSpec((mp, d), lambda i, j: (0, 0)),
            pl.BlockSpec((None, d, tn), lambda i, j: (i, 0, j)),
            pl.BlockSpec((None, 1, tn), lambda i, j: (i, 0, j)),
        ],
        out_specs=pl.BlockSpec((None, mp, tn), lambda i, j: (i, 0, j)),
        out_shape=jax.ShapeDtypeStruct((depth, mp, n), F32),
        compiler_params=_cparams(2),
        name="ada_mod",
    )(cvec, w_ada, b_ada.reshape(depth, 1, n))


def _mod_spec(j, d, ctx_row, batch_axis):
    if ctx_row is None:
        return pl.BlockSpec((None, None, 1, d), lambda *g: (g[batch_axis], j, 0, 0))
    return pl.BlockSpec((None, None, 1, d), lambda *g: (ctx_row, j, 0, 0))


def _row_spec(d):
    return pl.BlockSpec((1, d), lambda *g: (0, 0))


def _qkv_kernel(x_ref, sh_ref, sc_ref, w_ref, cos_ref, sin_ref, q_ref, k_ref, v_ref, *, qw, kvw):
    u = _modulate(x_ref[...], sh_ref[...], sc_ref[...]).astype(BF16)
    r = _dot(u, w_ref[...])
    qk = r[:, :qw + kvw] * cos_ref[...] + r[:, qw + 2 * kvw:] * sin_ref[...]
    q_ref[...] = qk[:, :qw].astype(BF16)
    k_ref[...] = qk[:, qw:].astype(BF16)
    v_ref[...] = r[:, qw + kvw:qw + 2 * kvw].astype(BF16)


def _qkv_proj(x, mod_i, ctx_row, w_all, cos_t, sin_t, qw, kvw, window=None):
    first, b = window if window is not None else (0, x.shape[0])
    _, l, d = x.shape
    tm = min(QKV_ROW_TILE, l)
    assert l % tm == 0
    n_all = w_all.shape[1]
    return pl.pallas_call(
        functools.partial(_qkv_kernel, qw=qw, kvw=kvw),
        grid=(l // tm, b),
        in_specs=[
            pl.BlockSpec((None, tm, d), lambda t, bb: (bb + first, t, 0)),
            _mod_spec(0, d, ctx_row, 1),
            _mod_spec(1, d, ctx_row, 1),
            pl.BlockSpec((d, n_all), lambda t, bb: (0, 0)),
            pl.BlockSpec((tm, qw + kvw), lambda t, bb: (t, 0)),
            pl.BlockSpec((tm, qw + kvw), lambda t, bb: (t, 0)),
        ],
        out_specs=[
            pl.BlockSpec((None, tm, qw), lambda t, bb: (bb, t, 0)),
            pl.BlockSpec((None, tm, kvw), lambda t, bb: (bb, t, 0)),
            pl.BlockSpec((None, tm, kvw), lambda t, bb: (bb, t, 0)),
        ],
        out_shape=[
            jax.ShapeDtypeStruct((b, l, qw), BF16),
            jax.ShapeDtypeStruct((b, l, kvw), BF16),
            jax.ShapeDtypeStruct((b, l, kvw), BF16),
        ],
        compiler_params=_cparams(2),
        name="qkv_rope",
    )(x, mod_i, mod_i, w_all, cos_t, sin_t)


def _attn_kernel(sink_ref, q_ref, *refs, tq, seq, n_kv, group, hd, has_window):
    if has_window:
        k_ref, v_ref, kc_ref, vc_ref, o_ref, s_scr, p_scr, m_scr, bias_scr = refs
    else:
        kc_ref, vc_ref, o_ref, s_scr, p_scr, m_scr = refs
    q = q_ref[...]
    kc = kc_ref[...]
    vc = vc_ref[...]
    span = tq + 2 * WINDOW if has_window else 0
    rows = group * tq
    if has_window:
        q0 = pl.program_id(1) * tq
        start = pl.multiple_of(jnp.clip(q0 - WINDOW, 0, seq - span), LANES)
        kw = k_ref[pl.ds(start, span), :]
        vw = v_ref[pl.ds(start, span), :]
        qpos = q0 + lax.broadcasted_iota(jnp.int32, (tq, span), 0)
        kpos = start + lax.broadcasted_iota(jnp.int32, (tq, span), 1)
        bias_scr[...] = jnp.where(jnp.abs(qpos - kpos) <= WINDOW, 0.0, NEG_BIG)

    for h in range(n_kv):
        heads = [h * group + g for g in range(group)]
        qh = jnp.concatenate([q[:, j * hd:(j + 1) * hd] for j in heads], axis=0)
        if has_window:
            s_scr[h, :, :span] = _dot_nt(qh, kw[:, h * hd:(h + 1) * hd])
        s_scr[h, :, span:] = _dot_nt(qh, kc[:, h * hd:(h + 1) * hd])

    chunks_per_head = tq // ATTN_ROW_CHUNK
    n_chunks = rows // ATTN_ROW_CHUNK
    n_tiles = (span + kc.shape[0]) // LANES
    win_tiles = span // LANES

    def logit_tiles(h, r):
        row = pl.multiple_of(r * ATTN_ROW_CHUNK, ATTN_ROW_CHUNK)
        s = s_scr[h, pl.ds(row, ATTN_ROW_CHUNK), :]
        tiles = [s[:, i * LANES:(i + 1) * LANES] for i in range(n_tiles)]
        if has_window:
            brow = pl.multiple_of((r % chunks_per_head) * ATTN_ROW_CHUNK, ATTN_ROW_CHUNK)
            bias = bias_scr[pl.ds(brow, ATTN_ROW_CHUNK), :]
            tiles = [t + bias[:, i * LANES:(i + 1) * LANES] if i < win_tiles else t for i, t in enumerate(tiles)]
        return row, tiles

    for h in range(n_kv):
        def row_max(r, carry, h=h):
            row, tiles = logit_tiles(h, r)
            sink = sink_ref[h * group + r // chunks_per_head]
            m = jnp.max(functools.reduce(jnp.maximum, tiles), axis=-1, keepdims=True)
            m_scr[h, pl.ds(row, ATTN_ROW_CHUNK), :] = jnp.broadcast_to(jnp.maximum(m, sink),
                                                                     (ATTN_ROW_CHUNK, LANES))
            return carry

        lax.fori_loop(0, n_chunks, row_max, 0, unroll=ATTN_UNROLL)

    for h in range(n_kv):
        def probs(r, carry, h=h):
            row, tiles = logit_tiles(h, r)
            sink = sink_ref[h * group + r // chunks_per_head]
            m = m_scr[h, pl.ds(row, ATTN_ROW_CHUNK), :]
            es = [jnp.exp(t - m) for t in tiles]
            for i, e in enumerate(es):
                p_scr[h, pl.ds(row, ATTN_ROW_CHUNK), i * LANES:(i + 1) * LANES] = e.astype(BF16)
            den = jnp.sum(functools.reduce(jnp.add, es), axis=-1, keepdims=True) + jnp.exp(sink - m)
            m_scr[h, pl.ds(row, ATTN_ROW_CHUNK), :] = 1.0 / den
            return carry

        lax.fori_loop(0, n_chunks, probs, 0, unroll=ATTN_UNROLL)

    for h in range(n_kv):
        o = _dot(p_scr[h, :, span:], vc[:, h * hd:(h + 1) * hd])
        if has_window:
            o = o + _dot(p_scr[h, :, :span], vw[:, h * hd:(h + 1) * hd])
        o = o * m_scr[h, :, :hd]
        for g in range(group):
            j = h * group + g
            o_ref[:, j * hd:(j + 1) * hd] = o[g * tq:(g + 1) * tq, :].astype(o_ref.dtype)


def _attention(q, k, v, kc, vc, sinks, has_window):
    b, l, qw = q.shape
    c, kvw = kc.shape[1], kc.shape[2]
    hd = kvw // N_KV_HEADS
    group = qw // kvw
    tq = Q_TILE if has_window else l
    assert l % tq == 0
    if has_window:
        assert l >= tq + 2 * WINDOW
    kern = functools.partial(_attn_kernel, tq=tq, seq=l, n_kv=N_KV_HEADS, group=group, hd=hd,
                             has_window=has_window)
    in_specs = [pl.BlockSpec(memory_space=pltpu.SMEM),
                pl.BlockSpec((None, tq, qw), lambda bb, t: (bb, t, 0))]
    args = [sinks, q]
    if has_window:
        in_specs += [pl.BlockSpec((None, l, kvw), lambda bb, t: (bb, 0, 0)),
                     pl.BlockSpec((None, l, kvw), lambda bb, t: (bb, 0, 0))]
        args += [k, v]
    in_specs += [pl.BlockSpec((None, c, kvw), lambda bb, t: (bb, 0, 0)),
                 pl.BlockSpec((None, c, kvw), lambda bb, t: (bb, 0, 0))]
    args += [kc, vc]
    n_keys = c + (tq + 2 * WINDOW if has_window else 0)
    scratch = [pltpu.VMEM((N_KV_HEADS, group * tq, n_keys), F32),
               pltpu.VMEM((N_KV_HEADS, group * tq, n_keys), BF16),
               pltpu.VMEM((N_KV_HEADS, group * tq, LANES), F32)]
    if has_window:
        scratch.append(pltpu.VMEM((tq, tq + 2 * WINDOW), F32))
    return pl.pallas_call(
        kern,
        grid=(b, l // tq),
        in_specs=in_specs,
        out_specs=pl.BlockSpec((None, tq, qw), lambda bb, t: (bb, t, 0)),
        out_shape=jax.ShapeDtypeStruct((b, l, qw), BF16),
        scratch_shapes=scratch,
        compiler_params=_cparams(2),
        name="win_attn" if has_window else "ctx_attn",
    )(*args)


def _proj_post_kernel(a_ref, w_ref, x_ref, gate_ref, g_ref, b_ref, *refs, alpha):
    route_in, o_ref, route_out = refs[:N_ROUTE_IN], refs[N_ROUTE_IN], refs[N_ROUTE_IN + 1:]
    y = _dot(a_ref[...], w_ref[...])
    z = alpha * x_ref[...] + gate_ref[...] * y
    x_new = _normalize(z, LN_EPS) * g_ref[...] + b_ref[...]
    o_ref[...] = x_new
    _route_tokens(x_new, *route_in, *route_out)


def _proj_post(a, w, x, mod_i, gate_j, ctx_row, ln_g, ln_b, alpha, rwt, rbias, window=None):
    first, b = window if window is not None else (0, x.shape[0])
    _, l, d = x.shape
    ka = a.shape[2]
    tm = min(ROW_TILE, l)
    assert l % tm == 0
    r_in, r_args, r_out, r_shape = _route_io(b, l, d, tm, mod_i, ctx_row, rwt, rbias)
    res = pl.pallas_call(
        functools.partial(_proj_post_kernel, alpha=alpha),
        grid=(b, l // tm),
        in_specs=[
            pl.BlockSpec((None, tm, ka), lambda bb, t: (bb, t, 0)),
            pl.BlockSpec((ka, d), lambda bb, t: (0, 0)),
            pl.BlockSpec((None, tm, d), lambda bb, t: (bb + first, t, 0)),
            _mod_spec(gate_j, d, ctx_row, 0),
            _row_spec(d),
            _row_spec(d),
        ] + r_in,
        out_specs=[pl.BlockSpec((None, tm, d), lambda bb, t: (bb, t, 0))] + r_out,
        out_shape=[jax.ShapeDtypeStruct((b, l, d), F32)] + r_shape,
        compiler_params=_cparams(2),
        name="proj_post_route",
    )(a, w, x, mod_i, ln_g, ln_b, *r_args)
    return res[0], res[1:]


def _fnet_a_kernel(x_ref, sh_ref, sc_ref, cs_ref, o_ref, *, d, gc):
    u = _modulate(x_ref[...], sh_ref[...], sc_ref[...]).astype(BF16)
    cs = cs_ref[...]
    for g in range(d // gc):
        r = _dot(u[:, g * gc:(g + 1) * gc], cs)
        o_ref[:, g * gc:(g + 1) * gc] = r[:, :gc].astype(BF16)
        o_ref[:, d + g * gc:d + (g + 1) * gc] = r[:, gc:].astype(BF16)


def _fnet_a(x, mod_i, ctx_row, cs):
    b, l, d = x.shape
    gc = d // F_GROUPS
    tm = min(ROW_TILE, l)
    assert l % tm == 0
    return pl.pallas_call(
        functools.partial(_fnet_a_kernel, d=d, gc=gc),
        grid=(b, l // tm),
        in_specs=[
            pl.BlockSpec((None, tm, d), lambda bb, t: (bb, t, 0)),
            _mod_spec(0, d, ctx_row, 0),
            _mod_spec(1, d, ctx_row, 0),
            pl.BlockSpec((gc, 2 * gc), lambda bb, t: (0, 0)),
        ],
        out_specs=pl.BlockSpec((None, tm, 2 * d), lambda bb, t: (bb, t, 0)),
        out_shape=jax.ShapeDtypeStruct((b, l, 2 * d), BF16),
        compiler_params=_cparams(2),
        name="fnet_chan_dft",
    )(x, mod_i, mod_i, cs)


def _fnet_b_kernel(cl_ref, sl_ref, ab_ref, wf_ref, x_ref, gate_ref, g_ref, b_ref, *refs, d, alpha):
    route_in, o_ref, route_out = refs[:N_ROUTE_IN], refs[N_ROUTE_IN], refs[N_ROUTE_IN + 1:]
    f = _dot(cl_ref[...], ab_ref[:, :d]) + _dot(sl_ref[...], ab_ref[:, d:])
    y = _dot(f.astype(BF16), wf_ref[...])
    z = alpha * x_ref[...] + gate_ref[...] * y
    x_new = _normalize(z, LN_EPS) * g_ref[...] + b_ref[...]
    o_ref[...] = x_new
    _route_tokens(x_new, *route_in, *route_out)


def _fnet_b(cl, sl, ab, wf, x, mod_i, gate_j, ctx_row, ln_g, ln_b, alpha, rwt, rbias):
    b, l, d = x.shape
    tm = min(ROW_TILE, l)
    assert l % tm == 0
    r_in, r_args, r_out, r_shape = _route_io(b, l, d, tm, mod_i, ctx_row, rwt, rbias)
    res = pl.pallas_call(
        functools.partial(_fnet_b_kernel, d=d, alpha=alpha),
        grid=(b, l // tm),
        in_specs=[
            pl.BlockSpec((tm, l), lambda bb, t: (t, 0)),
            pl.BlockSpec((tm, l), lambda bb, t: (t, 0)),
            pl.BlockSpec((None, l, 2 * d), lambda bb, t: (bb, 0, 0)),
            pl.BlockSpec((d, d), lambda bb, t: (0, 0)),
            pl.BlockSpec((None, tm, d), lambda bb, t: (bb, t, 0)),
            _mod_spec(gate_j, d, ctx_row, 0),
            _row_spec(d),
            _row_spec(d),
        ] + r_in,
        out_specs=[pl.BlockSpec((None, tm, d), lambda bb, t: (bb, t, 0))] + r_out,
        out_shape=[jax.ShapeDtypeStruct((b, l, d), F32)] + r_shape,
        compiler_params=_cparams(2),
        name="fnet_seq_dft_route",
    )(cl, sl, ab, wf, x, mod_i, ln_g, ln_b, *r_args)
    return res[0], res[1:]


def _dft_tables(n, scale):
    j = jnp.arange(n, dtype=jnp.int32)
    ang = ((j[:, None] * j[None, :]) % n).astype(F32) * (2.0 * math.pi / n)
    return jnp.cos(ang) * scale, jnp.sin(ang) * scale


def _route_tokens(x, sh_ref, sc_ref, rwt_ref, bias_ref, tri_ref, u_ref, te_ref, w_ref, rk_ref, cnt_ref):
    n_groups, topk_groups, top_k = N_EXPERT_GROUPS, TOPK_GROUPS, TOP_K
    u = _modulate(x, sh_ref[...], sc_ref[...])
    uh, ul = _split_bf16(u)
    u_ref[...] = _pack_halves(u)
    wh, wl = _split_bf16(rwt_ref[...])
    logits = _dot_nt(wh, uh) + _dot_nt(wl, uh) + _dot_nt(wh, ul)
    e, tm = logits.shape
    pg = e // n_groups
    neg = -jnp.inf
    scores = jax.nn.sigmoid(logits)
    sel = scores + bias_ref[...]
    sub = lax.broadcasted_iota(jnp.int32, (pg, tm), 0)
    groups = [sel[g * pg:(g + 1) * pg, :] for g in range(n_groups)]
    sgroups = [scores[g * pg:(g + 1) * pg, :] for g in range(n_groups)]

    gs_rows = []
    for s_g in groups:
        m1 = jnp.max(s_g, axis=0, keepdims=True)
        first = jnp.min(jnp.where(s_g == m1, sub, pg), axis=0, keepdims=True)
        m2 = jnp.max(jnp.where(sub == first, neg, s_g), axis=0, keepdims=True)
        gs_rows.append(m1 + m2)
    gs = jnp.concatenate(gs_rows, axis=0)
    gidx = lax.broadcasted_iota(jnp.int32, (n_groups, tm), 0)
    ok = jnp.zeros((n_groups, tm), F32)
    for _ in range(topk_groups):
        m = jnp.max(gs, axis=0, keepdims=True)
        first = jnp.min(jnp.where(gs == m, gidx, n_groups), axis=0, keepdims=True)
        hit = gidx == first
        ok = jnp.where(hit, 1.0, ok)
        gs = jnp.where(hit, neg, gs)

    cur = [jnp.where(ok[g:g + 1, :] > 0.0, groups[g], neg) for g in range(n_groups)]
    eidx = [sub + g * pg for g in range(n_groups)]
    chosen = [jnp.zeros((pg, tm), F32) for _ in range(n_groups)]
    e_rows, s_rows = [], []
    for _ in range(top_k):
        m = jnp.max(functools.reduce(jnp.maximum, cur), axis=0, keepdims=True)
        cand = functools.reduce(jnp.minimum, [jnp.where(cur[g] == m, eidx[g], e) for g in range(n_groups)])
        first = jnp.min(cand, axis=0, keepdims=True)
        picked = jnp.zeros((pg, tm), F32)
        for g in range(n_groups):
            hit = eidx[g] == first
            picked = picked + jnp.where(hit, sgroups[g], 0.0)
            chosen[g] = jnp.where(hit, 1.0, chosen[g])
            cur[g] = jnp.where(hit, neg, cur[g])
        e_rows.append(first)
        s_rows.append(jnp.sum(picked, axis=0, keepdims=True))
    w = jnp.concatenate(s_rows, axis=0)
    te_ref[...] = jnp.concatenate(e_rows, axis=0)
    w_ref[...] = w / jnp.sum(w, axis=0, keepdims=True) * ROUTED_SCALE

    sel_mask = jnp.concatenate(chosen, axis=0).astype(BF16)
    before = _dot(sel_mask, tri_ref[...])
    rk_rows = []
    for k in range(top_k):
        acc = jnp.zeros((pg, tm), F32)
        for g in range(n_groups):
            acc = acc + jnp.where(eidx[g] == e_rows[k], before[g * pg:(g + 1) * pg, :], 0.0)
        rk_rows.append(jnp.sum(acc, axis=0, keepdims=True))
    rk_ref[...] = jnp.concatenate(rk_rows, axis=0).astype(jnp.int32)
    cnt_ref[...] = _dot_nt(jnp.ones((8, tm), BF16), sel_mask)


N_ROUTE_IN = 5
N_ROUTE_OUT = 5


def _route_io(b, l, d, tm, mod_i, ctx_row, rwt, bias):
    e = rwt.shape[0]
    assert l % tm == 0 and e % N_EXPERT_GROUPS == 0
    nt = l // tm
    t = b * l
    row = lax.broadcasted_iota(jnp.int32, (tm, tm), 0)
    col = lax.broadcasted_iota(jnp.int32, (tm, tm), 1)
    tri = (row < col).astype(BF16)
    in_specs = [
        _mod_spec(3, d, ctx_row, 0),
        _mod_spec(4, d, ctx_row, 0),
        pl.BlockSpec((e, d), lambda bb, tt: (0, 0)),
        pl.BlockSpec((e, 1), lambda bb, tt: (0, 0)),
        pl.BlockSpec((tm, tm), lambda bb, tt: (0, 0)),
    ]
    args = [mod_i, mod_i, rwt, bias.reshape(e, 1).astype(F32), tri]
    tok_spec = pl.BlockSpec((TOP_K, tm), lambda bb, tt: (0, bb * nt + tt))
    out_specs = [
        pl.BlockSpec((None, tm, d // 2), lambda bb, tt: (bb, tt, 0)),
        tok_spec, tok_spec, tok_spec,
        pl.BlockSpec((None, 8, e), lambda bb, tt: (bb * nt + tt, 0, 0)),
    ]
    out_shape = [
        jax.ShapeDtypeStruct((b, l, d // 2), I32),
        jax.ShapeDtypeStruct((TOP_K, t), jnp.int32),
        jax.ShapeDtypeStruct((TOP_K, t), F32),
        jax.ShapeDtypeStruct((TOP_K, t), jnp.int32),
        jax.ShapeDtypeStruct((b * nt, 8, e), F32),
    ]
    return in_specs, args, out_specs, out_shape


def _expert_layout(cnt, tmb, n_blocks):
    cnt_i = cnt[:, 0, :].astype(jnp.int32)
    e = cnt_i.shape[1]
    counts = cnt_i.sum(axis=0)
    padded = (counts + tmb - 1) // tmb * tmb
    pend = jnp.cumsum(padded)
    base = (pend - padded)[None, :] + jnp.cumsum(cnt_i, axis=0) - cnt_i
    n_used = (pend[-1] // tmb).astype(jnp.int32)
    blk = jnp.arange(n_blocks, dtype=jnp.int32)
    block_e = jnp.sum((blk[:, None] * tmb >= pend[None, :]).astype(jnp.int32), axis=1)
    last_e = jnp.sum((jnp.maximum(n_used - 1, 0) * tmb >= pend).astype(jnp.int32))
    block_e = jnp.clip(jnp.where(blk < n_used, block_e, last_e), 0, e - 1).astype(jnp.int32)
    seg_end = (pend - padded + counts)[block_e]
    block_valid = jnp.clip(seg_end - blk * tmb, 0, tmb).astype(jnp.int32)
    return base.reshape(-1).astype(jnp.int32), block_e, block_valid, n_used.reshape(1)


def _pos_kernel(base_ref, te_ref, rk_ref, w_ref, pos_d_ref, pos_c_ref, w_c_ref, *, n_experts):
    i = pl.program_id(0)
    te = te_ref[...]
    pos = rk_ref[...]
    for e in range(n_experts):
        pos = pos + jnp.where(te == e, base_ref[i * n_experts + e], 0)
    w = w_ref[...]
    for c in range(pos_d_ref.shape[0]):
        pos_d_ref[c] = pos[:, c * DISPATCH_CHUNK:(c + 1) * DISPATCH_CHUNK]
    for c in range(pos_c_ref.shape[0]):
        pos_c_ref[c] = pos[:, c * COMBINE_CHUNK:(c + 1) * COMBINE_CHUNK]
        w_c_ref[c] = w[:, c * COMBINE_CHUNK:(c + 1) * COMBINE_CHUNK]


def _positions(base, top_e, rank, wts, n_experts, tm):
    k, t = top_e.shape
    assert tm % DISPATCH_CHUNK == 0 and tm % COMBINE_CHUNK == 0
    nd, nc = tm // DISPATCH_CHUNK, tm // COMBINE_CHUNK
    spec = pl.BlockSpec((k, tm), lambda i, base_ref: (0, i))
    return pl.pallas_call(
        functools.partial(_pos_kernel, n_experts=n_experts),
        grid_spec=pltpu.PrefetchScalarGridSpec(
            num_scalar_prefetch=1, grid=(t // tm,), in_specs=[spec, spec, spec],
            out_specs=[pl.BlockSpec((nd, k, DISPATCH_CHUNK), lambda i, base_ref: (i, 0, 0)),
                       pl.BlockSpec((nc, k, COMBINE_CHUNK), lambda i, base_ref: (i, 0, 0)),
                       pl.BlockSpec((nc, k, COMBINE_CHUNK), lambda i, base_ref: (i, 0, 0))]),
        out_shape=[jax.ShapeDtypeStruct((t // DISPATCH_CHUNK, k, DISPATCH_CHUNK), I32),
                   jax.ShapeDtypeStruct((t // COMBINE_CHUNK, k, COMBINE_CHUNK), I32),
                   jax.ShapeDtypeStruct((t // COMBINE_CHUNK, k, COMBINE_CHUNK), F32)],
        compiler_params=_cparams(1),
        name="moe_positions",
    )(base, top_e, rank, wts)


def _sc_dispatch(rows, pos3, cap):
    t, w = rows.shape
    n_chunks, top_k, n = pos3.shape
    n_workers = SC_CORES * SC_SUBCORES
    assert n_chunks * n == t and n <= LANES
    per_worker = -(-n_chunks // n_workers)
    mesh = plsc.VectorSubcoreMesh(core_axis_name="core", subcore_axis_name="subcore",
                                  num_cores=SC_CORES, num_subcores=SC_SUBCORES)

    def body(rows_hbm, pos_hbm, out_hbm, idx_v, rows_v, sem):
        wid = lax.axis_index("subcore") * SC_CORES + lax.axis_index("core")

        @pl.loop(0, per_worker)
        def _(j):
            c = wid * per_worker + j

            @pl.when(c < n_chunks)
            def _():
                pltpu.sync_copy(pos_hbm.at[c], idx_v)
                pltpu.sync_copy(rows_hbm.at[pl.ds(c * n, n)], rows_v)
                copies = [pltpu.async_copy(rows_v, out_hbm.at[idx_v.at[k]], sem) for k in range(top_k)]
                for cp in copies:
                    cp.wait()

    return pl.kernel(
        body,
        out_type=jax.ShapeDtypeStruct((cap, w), I32),
        mesh=mesh,
        scratch_types=[pltpu.VMEM((top_k, n), I32), pltpu.VMEM((n, w), I32), pltpu.SemaphoreType.DMA],
        name="moe_dispatch_sc",
    )(rows, pos3)


def _sc_combine(rows, pos3, wts3):
    cap, w = rows.shape
    n_chunks, top_k, n = pos3.shape
    t = n_chunks * n
    n_workers = SC_CORES * SC_SUBCORES
    assert n <= SC_LANES and w % SC_LANES == 0
    per_worker = -(-n_chunks // n_workers)
    mesh = plsc.VectorSubcoreMesh(core_axis_name="core", subcore_axis_name="subcore",
                                  num_cores=SC_CORES, num_subcores=SC_SUBCORES)

    def body(rows_hbm, pos_hbm, wts_hbm, out_hbm, idx_v, wts_v, acc_v, *rest):
        bufs, sem = rest[:top_k], rest[top_k]
        wid = lax.axis_index("subcore") * SC_CORES + lax.axis_index("core")

        @pl.loop(0, per_worker)
        def _(it):
            c = wid * per_worker + it

            @pl.when(c < n_chunks)
            def _():
                pltpu.sync_copy(pos_hbm.at[c], idx_v)
                pltpu.sync_copy(wts_hbm.at[c], wts_v)
                gathers = [pltpu.async_copy(rows_hbm.at[idx_v.at[k]], bufs[k], sem) for k in range(top_k)]
                for g in gathers:
                    g.wait()

                @pl.loop(0, n)
                def _(j):
                    tok = jnp.full((SC_LANES,), j, I32)
                    wk = [plsc.load_gather(wts_v, [jnp.full((SC_LANES,), k, I32), tok]) for k in range(top_k)]

                    @plsc.parallel_loop(0, w // SC_LANES, unroll=4)
                    def _(g):
                        col = g * SC_LANES
                        hi = jnp.zeros((SC_LANES,), F32)
                        lo = jnp.zeros((SC_LANES,), F32)
                        for k in range(top_k):
                            words = bufs[k][j, pl.ds(col, SC_LANES)]
                            hi = hi + wk[k] * plsc.bitcast(words & jnp.int32(-65536), F32)
                            lo = lo + wk[k] * plsc.bitcast(lax.shift_left(words, 16), F32)
                        acc_v[j, pl.ds(col, SC_LANES)] = hi
                        acc_v[j, pl.ds(w + col, SC_LANES)] = lo

                pltpu.sync_copy(acc_v, out_hbm.at[pl.ds(c * n, n)])

    return pl.kernel(
        body,
        out_type=jax.ShapeDtypeStruct((t, 2 * w), F32),
        mesh=mesh,
        scratch_types=[pltpu.VMEM((top_k, n), I32), pltpu.VMEM((top_k, n), F32), pltpu.VMEM((n, 2 * w), F32)]
        + [pltpu.VMEM((n, w), I32) for _ in range(top_k)] + [pltpu.SemaphoreType.DMA],
        compiler_params=pltpu.CompilerParams(needs_layout_passes=False),
        name="moe_combine_sc",
    )(rows, pos3, wts3)


def _gmm_kernel(be_ref, bv_ref, nu_ref, x_ref, *refs, tmb, per_step):
    w_refs, o_ref = refs[:-1], refs[-1]
    i = pl.program_id(0)

    @pl.when(i * per_step < nu_ref[0])
    def _():
        for s in range(per_step):
            wg_ref, wu_ref, wd_ref = w_refs[3 * s:3 * s + 3]
            words = x_ref[s * tmb:(s + 1) * tmb, :]
            half = words.shape[1]
            live = lax.broadcasted_iota(I32, (tmb, half), 0) < bv_ref[i * per_step + s]
            xa, xb = _unpack_halves(jnp.where(live, words, 0))
            g = _dot(xa, wg_ref[:half, :]) + _dot(xb, wg_ref[half:, :])
            u = _dot(xa, wu_ref[:half, :]) + _dot(xb, wu_ref[half:, :])
            o_ref[s * tmb:(s + 1) * tmb, :] = _pack_halves(_dot((_silu(g) * u).astype(BF16), wd_ref[...]))


def _grouped_experts(x_sorted, block_e, block_valid, n_used, wg, wu, wd, tmb):
    cap, half = x_sorted.shape
    d = 2 * half
    ff = wg.shape[2]
    n_blocks = cap // tmb
    per_step = MOE_BLOCKS_PER_STEP if n_blocks % MOE_BLOCKS_PER_STEP == 0 else 1
    n_steps = n_blocks // per_step

    def row_map(i, be, bv, nu):
        return (jnp.minimum(i, (nu[0] + per_step - 1) // per_step - 1), 0)

    w_specs = []
    for s in range(per_step):
        w_map = functools.partial(lambda i, be, bv, nu, s: (be[i * per_step + s], 0, 0), s=s)
        w_specs += [pl.BlockSpec((None, d, ff), w_map), pl.BlockSpec((None, d, ff), w_map),
                    pl.BlockSpec((None, ff, d), w_map)]

    return pl.pallas_call(
        functools.partial(_gmm_kernel, tmb=tmb, per_step=per_step),
        grid_spec=pltpu.PrefetchScalarGridSpec(
            num_scalar_prefetch=3,
            grid=(n_steps,),
            in_specs=[pl.BlockSpec((per_step * tmb, half), row_map)] + w_specs,
            out_specs=pl.BlockSpec((per_step * tmb, half), row_map),
        ),
        out_shape=jax.ShapeDtypeStruct((cap, half), I32),
        compiler_params=_cparams(1),
        name="moe_experts",
    )(block_e, block_valid, n_used, x_sorted, *([wg, wu, wd] * per_step))


def _moe_out_kernel(r_ref, sg_ref, su_ref, sd_ref, x_ref, sh_ref, sc_ref, gate_ref, g_ref, b_ref, o_ref, *, alpha):
    x = x_ref[...]
    u = _modulate(x, sh_ref[...], sc_ref[...]).astype(BF16)
    y = r_ref[...] + _dot((_silu(_dot(u, sg_ref[...])) * _dot(u, su_ref[...])).astype(BF16), sd_ref[...])
    z = alpha * x + gate_ref[...] * y
    o_ref[...] = _normalize(z, LN_EPS) * g_ref[...] + b_ref[...]


def _moe_out_into_kernel(*refs, alpha):
    _moe_out_kernel(*refs[:-2], refs[-1], alpha=alpha)


def _moe_out(routed, sg, su, sd, x, mod_i, ctx_row, ln_g, ln_b, alpha, place=None):
    b, l, d = x.shape
    ff = sg.shape[1]
    tm = min(ROW_TILE, l)
    assert l % tm == 0
    first, total, buffer = place if place is not None else (0, b, None)
    tok = pl.BlockSpec((None, tm, d), lambda bb, t: (bb, t, 0))
    in_specs = [
        tok,
        pl.BlockSpec((d, ff), lambda bb, t: (0, 0)),
        pl.BlockSpec((d, ff), lambda bb, t: (0, 0)),
        pl.BlockSpec((ff, d), lambda bb, t: (0, 0)),
        tok,
        _mod_spec(3, d, ctx_row, 0),
        _mod_spec(4, d, ctx_row, 0),
        _mod_spec(5, d, ctx_row, 0),
        _row_spec(d),
        _row_spec(d),
    ]
    args = [routed, sg, su, sd, x, mod_i, mod_i, mod_i, ln_g, ln_b]
    kern = functools.partial(_moe_out_kernel, alpha=alpha)
    aliases = {}
    if buffer is not None:
        in_specs.append(pl.BlockSpec(memory_space=pl.ANY))
        aliases = {len(args): 0}
        args.append(buffer)
        kern = functools.partial(_moe_out_into_kernel, alpha=alpha)
    return pl.pallas_call(
        kern,
        grid=(b, l // tm),
        in_specs=in_specs,
        out_specs=pl.BlockSpec((None, tm, d), lambda bb, t: (bb + first, t, 0)),
        out_shape=jax.ShapeDtypeStruct((total, l, d), F32),
        input_output_aliases=aliases,
        compiler_params=_cparams(2),
        name="moe_shared_post",
    )(*args)


def _moe_dispatch(shape, routing, n_experts):
    b, l, d = shape
    t = b * l
    tmb = MOE_ROW_TILE
    n_blocks = -(-(t * TOP_K) // tmb) + n_experts
    u2, top_e, wts, rank, cnt = routing
    base, block_e, block_valid, n_used = _expert_layout(cnt, tmb, n_blocks)
    pos_d, pos_c, wts_c = _positions(base, top_e, rank, wts, n_experts, min(ROW_TILE, l))
    x_sorted = _sc_dispatch(u2.reshape(t, d // 2), pos_d, n_blocks * tmb)
    return x_sorted, block_e, block_valid, n_used, pos_c, wts_c


def _moe_experts(shape, dispatched, wg, wu, wd):
    x_sorted, block_e, block_valid, n_used, pos_c, wts_c = dispatched
    y_sorted = _grouped_experts(x_sorted, block_e, block_valid, n_used, wg, wu, wd, MOE_ROW_TILE)
    return y_sorted, _sc_combine(y_sorted, pos_c, wts_c).reshape(shape)


def _after(token, value):
    if token is None:
        return value
    return lax.optimization_barrier((token, value))[1]


def _rope_tables(l, hd, n_q_heads, n_k_heads, rope):
    axis_rot = hd // 2
    qscale = hd ** -0.5
    if rope:
        t = jnp.arange(l, dtype=jnp.int32)
        r = (t // GRID_W).astype(F32)
        col = (t % GRID_W).astype(F32)
        inv = ROPE_BASE ** (-jnp.arange(0, axis_rot, 2, dtype=F32) / axis_rot)
        ar = r[:, None] * inv
        ac = col[:, None] * inv
        ang = jnp.concatenate([ar, ar, ac, ac], axis=-1)
        cos, sin = jnp.cos(ang), jnp.sin(ang)
    else:
        cos, sin = jnp.ones((l, hd), F32), jnp.zeros((l, hd), F32)
    cos_t = jnp.concatenate([jnp.tile(cos, (1, n_q_heads)) * qscale, jnp.tile(cos, (1, n_k_heads))], axis=1)
    sin_t = jnp.concatenate([jnp.tile(sin, (1, n_q_heads)) * qscale, jnp.tile(sin, (1, n_k_heads))], axis=1)
    return cos_t, sin_t


def _rot_columns(w, hd):
    d, n = w.shape
    q = hd // 4
    w4 = w.reshape(d, n // (2 * q), 2, q)
    return jnp.stack([-w4[:, :, 1, :], w4[:, :, 0, :]], axis=2).reshape(d, n)


def kernel(x, c, ctx, c_ctx, w_ada, b_ada, ln_g, ln_b, attn_w_qkv, attn_w_o, attn_sinks, fnet_w,
           router_w, router_bias, exp_w_gate, exp_w_up, exp_w_down, sh_w_gate, sh_w_up, sh_w_down):
    b, l, d = x.shape
    cl = ctx.shape[1]
    depth = w_ada.shape[0]
    n_heads = attn_sinks.shape[1]
    hd = d // n_heads
    qw = n_heads * hd
    kvw = N_KV_HEADS * hd
    alpha = (2.0 * depth) ** 0.25
    gc = d // F_GROUPS

    mp = -(-(b + 1) // 16) * 16
    cvec = jnp.concatenate([c, c_ctx[None, :], jnp.zeros((mp - b - 1, d), F32)], axis=0)
    mod = _ada_all(cvec, w_ada, b_ada).reshape(depth, mp, 6, 1, d)

    def run_stream(x, h, mod, first, count):
        b = count
        token, _ = yield None
        x = _after(token, x)
        window = (first, count)
        for i in range(depth):
            kind = i % N_MIXERS
            j = i // N_MIXERS
            update_ctx = any((m % N_MIXERS) == 0 for m in range(i + 1, depth))
            mod_i = mod[i]
            g1 = ln_g[i, 0].reshape(1, d)
            b1 = ln_b[i, 0].reshape(1, d)
            g2 = ln_g[i, 1].reshape(1, d)
            b2 = ln_b[i, 1].reshape(1, d)
            rwt, rbias = moe_w[i][:2]

            if kind == 0:
                w_all, w_o = attn_w[j]
                q, k, v = _qkv_proj(x, mod_i, None, w_all, cos_l, sin_l, qw, kvw, window)
                q_c, k_c, v_c = _qkv_proj(h, mod_i, b, w_all, cos_c, sin_c, qw, kvw, window)
                o = _attention(q, k, v, k_c, v_c, attn_sinks[j], True)
                x, x_route = _proj_post(o, w_o, x, mod_i, 2, None, g1, b1, alpha, rwt, rbias, window)
                if update_ctx:
                    o_c = _attention(q_c, None, None, k_c, v_c, attn_sinks[j], False)
                    h, h_route = _proj_post(o_c, w_o, h, mod_i, 2, b, g1, b1, alpha, rwt, rbias, window)
            else:
                if window is not None:
                    x, h = x[first:first + count], h[first:first + count]
                streams = [(x, None)] + ([(h, b)] if update_ctx else [])
                outs = []
                for s, ctx_row in streams:
                    c_l, s_l = dft_seq[s.shape[1]]
                    ab = _fnet_a(s, mod_i, ctx_row, cs)
                    outs.append(_fnet_b(c_l, s_l, ab, fnet_wb[j], s, mod_i, 2, ctx_row, g1, b1, alpha, rwt, rbias))
                x, x_route = outs[0]
                if update_ctx:
                    h, h_route = outs[1]

            wg, wu, wd, sg, su, sd = moe_w[i][2:]
            n_experts = rwt.shape[0]
            x_disp = _moe_dispatch(x.shape, x_route, n_experts)
            h_disp = _moe_dispatch(h.shape, h_route, n_experts) if update_ctx else None
            window = None
            token, _ = yield x
            x_disp = (_after(token, x_disp[0]),) + x_disp[1:]
            y_sorted, x_routed = _moe_experts(x.shape, x_disp, wg, wu, wd)
            h_routed = _moe_experts(h.shape, h_disp, wg, wu, wd)[1] if update_ctx else None
            token, buffer = yield y_sorted
            x_routed = _after(token, x_routed)
            place = (first, total, buffer) if i == depth - 1 else None
            x = _moe_out(x_routed, sg, su, sd, x, mod_i, None, g2, b2, alpha, place)
            if update_ctx:
                h = _moe_out(h_routed, sg, su, sd, h, mod_i, b, g2, b2, alpha)
        return x

    attn_w = []
    for j in range(attn_w_qkv.shape[0]):
        w = attn_w_qkv[j]
        w_all = jnp.concatenate([w, _rot_columns(w[:, :qw + kvw], hd)], axis=1).astype(BF16)
        attn_w.append((w_all, attn_w_o[j].astype(BF16)))
    cos_l, sin_l = _rope_tables(l, hd, n_heads, N_KV_HEADS, True)
    cos_c, sin_c = _rope_tables(cl, hd, n_heads, N_KV_HEADS, False)
    fnet_wb = [fnet_w[j].astype(BF16) for j in range(fnet_w.shape[0])]
    cc, sc = _dft_tables(gc, gc ** -0.5)
    cs = jnp.concatenate([cc, sc], axis=1).astype(BF16)
    dft_seq = {}
    for ls in (l, cl):
        c_l, s_l = _dft_tables(ls, ls ** -0.5)
        dft_seq[ls] = (c_l.astype(BF16), (-s_l).astype(BF16))
    moe_w = [(router_w[i].T, router_bias[i], exp_w_gate[i].astype(BF16), exp_w_up[i].astype(BF16),
              exp_w_down[i].astype(BF16), sh_w_gate[i].astype(BF16), sh_w_up[i].astype(BF16),
              sh_w_down[i].astype(BF16)) for i in range(depth)]

    n_streams = N_STREAMS if b % N_STREAMS == 0 else 1
    bs = b // n_streams
    total = b
    gens = []
    for s in range(n_streams):
        mod_s = jnp.concatenate([mod[:, s * bs:(s + 1) * bs], mod[:, b:b + 1]], axis=1)
        gens.append(run_stream(x, ctx, mod_s, s * bs, bs))
    for g in gens:
        next(g)
    running = list(range(n_streams))
    token = result = None
    while running:
        for s in list(running):
            try:
                token = gens[s].send((token, result))
            except StopIteration as done:
                token = result = done.value
                running.remove(s)
    return result
```

```python
import functools
import math

import jax
import jax.numpy as jnp
from jax import lax
from jax.experimental import pallas as pl
from jax.experimental.pallas import tpu as pltpu
from jax.experimental.pallas import tpu_sc as plsc

F32 = jnp.float32
BF16 = jnp.bfloat16
I32 = jnp.int32

N_KV_HEADS = 4
WINDOW = 128
GRID_W = 64
ROPE_BASE = 10000.0
F_GROUPS = 4
TOP_K = 8
N_EXPERT_GROUPS = 8
TOPK_GROUPS = 4
ROUTED_SCALE = 2.5
N_MIXERS = 2
LN_EPS = 1e-5
MOD_EPS = 1e-6

LANES = 128
VMEM_LIMIT_BYTES = 52 * 1024 * 1024
ROW_TILE = 512
QKV_ROW_TILE = 256
Q_TILE = 128
ATTN_ROW_CHUNK = 32
ATTN_UNROLL = 16
MOE_ROW_TILE = 512
MOE_BLOCKS_PER_STEP = 2
DISPATCH_CHUNK = 128
COMBINE_CHUNK = 16
N_STREAMS = 2
SC_CORES = 2
SC_SUBCORES = 16
SC_LANES = 16
NEG_BIG = -1e30


def _cparams(n_axes):
    return pltpu.CompilerParams(dimension_semantics=("arbitrary",) * n_axes,
                                vmem_limit_bytes=VMEM_LIMIT_BYTES)


def _dot(a, b):
    return jnp.dot(a, b, preferred_element_type=F32)


def _dot_nt(a, b):
    return lax.dot_general(a, b, (((1,), (1,)), ((), ())), preferred_element_type=F32)


def _split_bf16(a):
    hi = a.astype(BF16)
    lo = (a - hi.astype(F32)).astype(BF16)
    return hi, lo


def _normalize(x, eps):
    mu = jnp.mean(x, axis=-1, keepdims=True)
    xc = x - mu
    var = jnp.mean(xc * xc, axis=-1, keepdims=True)
    return xc * lax.rsqrt(var + eps)


def _modulate(x, shift, scale):
    return _normalize(x, MOD_EPS) * (1.0 + scale) + shift


def _silu(x):
    return x * jax.nn.sigmoid(x)


def _pack_halves(x):
    n = x.shape[1] // 2
    r = x.astype(BF16).astype(F32)
    hi = pltpu.bitcast(r[:, :n], I32)
    lo = pltpu.bitcast(r[:, n:], I32)
    return hi | lax.shift_right_logical(lo, 16)


def _unpack_halves(w):
    a = pltpu.bitcast(w & jnp.int32(-65536), F32).astype(BF16)
    b = pltpu.bitcast(lax.shift_left(w, 16), F32).astype(BF16)
    return a, b


def _ada_kernel(c_ref, w_ref, b_ref, o_ref):
    s = _silu(c_ref[...])
    sh, sl = _split_bf16(s)
    wh, wl = _split_bf16(w_ref[...])
    o_ref[...] = _dot(sh, wh) + _dot(sl, wh) + _dot(sh, wl) + b_ref[...]


def _ada_all(cvec, w_ada, b_ada):
    depth, d, n = w_ada.shape
    mp = cvec.shape[0]
    tn = 1536
    assert n % tn == 0
    return pl.pallas_call(
        _ada_kernel,
        grid=(depth, n // tn),
        in_specs=[
            pl.BlockSpec((mp, d), lambda i, j: (0, 0)),
            pl.BlockSpec((None, d, tn), lambda i, j: (i, 0, j)),
            pl.BlockSpec((None, 1, tn), lambda i, j: (i, 0, j)),
        ],
        out_specs=pl.BlockSpec((None, mp, tn), lambda i, j: (i, 0, j)),
        out_shape=jax.ShapeDtypeStruct((depth, mp, n), F32),
        compiler_params=_cparams(2),
        name="ada_mod",
    )(cvec, w_ada, b_ada.reshape(depth, 1, n))


def _mod_spec(j, d, ctx_row, batch_axis):
    if ctx_row is None:
        return pl.BlockSpec((None, None, 1, d), lambda *g: (g[batch_axis], j, 0, 0))
    return pl.BlockSpec((None, None, 1, d), lambda *g: (ctx_row, j, 0, 0))


def _row_spec(d):
    return pl.BlockSpec((1, d), lambda *g: (0, 0))


def _qkv_kernel(x_ref, sh_ref, sc_ref, w_ref, cos_ref, sin_ref, q_ref, k_ref, v_ref, *, qw, kvw):
    u = _modulate(x_ref[...], sh_ref[...], sc_ref[...]).astype(BF16)
    r = _dot(u, w_ref[...])
    qk = r[:, :qw + kvw] * cos_ref[...] + r[:, qw + 2 * kvw:] * sin_ref[...]
    q_ref[...] = qk[:, :qw].astype(BF16)
    k_ref[...] = qk[:, qw:].astype(BF16)
    v_ref[...] = r[:, qw + kvw:qw + 2 * kvw].astype(BF16)


def _qkv_proj(x, mod_i, ctx_row, w_all, cos_t, sin_t, qw, kvw, window=None):
    first, b = window if window is not None else (0, x.shape[0])
    _, l, d = x.shape
    tm = min(QKV_ROW_TILE, l)
    assert l % tm == 0
    n_all = w_all.shape[1]
    return pl.pallas_call(
        functools.partial(_qkv_kernel, qw=qw, kvw=kvw),
        grid=(l // tm, b),
        in_specs=[
            pl.BlockSpec((None, tm, d), lambda t, bb: (bb + first, t, 0)),
            _mod_spec(0, d, ctx_row, 1),
            _mod_spec(1, d, ctx_row, 1),
            pl.BlockSpec((d, n_all), lambda t, bb: (0, 0)),
            pl.BlockSpec((tm, qw + kvw), lambda t, bb: (t, 0)),
            pl.BlockSpec((tm, qw + kvw), lambda t, bb: (t, 0)),
        ],
        out_specs=[
            pl.BlockSpec((None, tm, qw), lambda t, bb: (bb, t, 0)),
            pl.BlockSpec((None, tm, kvw), lambda t, bb: (bb, t, 0)),
            pl.BlockSpec((None, tm, kvw), lambda t, bb: (bb, t, 0)),
        ],
        out_shape=[
            jax.ShapeDtypeStruct((b, l, qw), BF16),
            jax.ShapeDtypeStruct((b, l, kvw), BF16),
            jax.ShapeDtypeStruct((b, l, kvw), BF16),
        ],
        compiler_params=_cparams(2),
        name="qkv_rope",
    )(x, mod_i, mod_i, w_all, cos_t, sin_t)


def _attn_kernel(sink_ref, q_ref, *refs, tq, seq, n_kv, group, hd, has_window):
    if has_window:
        k_ref, v_ref, kc_ref, vc_ref, o_ref, s_scr, p_scr, m_scr, bias_scr = refs
    else:
        kc_ref, vc_ref, o_ref, s_scr, p_scr, m_scr = refs
    q = q_ref[...]
    kc = kc_ref[...]
    vc = vc_ref[...]
    span = tq + 2 * WINDOW if has_window else 0
    rows = group * tq
    if has_window:
        q0 = pl.program_id(1) * tq
        start = pl.multiple_of(jnp.clip(q0 - WINDOW, 0, seq - span), LANES)
        kw = k_ref[pl.ds(start, span), :]
        vw = v_ref[pl.ds(start, span), :]
        qpos = q0 + lax.broadcasted_iota(jnp.int32, (tq, span), 0)
        kpos = start + lax.broadcasted_iota(jnp.int32, (tq, span), 1)
        bias_scr[...] = jnp.where(jnp.abs(qpos - kpos) <= WINDOW, 0.0, NEG_BIG)

    for h in range(n_kv):
        heads = [h * group + g for g in range(group)]
        qh = jnp.concatenate([q[:, j * hd:(j + 1) * hd] for j in heads], axis=0)
        if has_window:
            s_scr[h, :, :span] = _dot_nt(qh, kw[:, h * hd:(h + 1) * hd])
        s_scr[h, :, span:] = _dot_nt(qh, kc[:, h * hd:(h + 1) * hd])

    chunks_per_head = tq // ATTN_ROW_CHUNK
    n_chunks = rows // ATTN_ROW_CHUNK
    n_tiles = (span + kc.shape[0]) // LANES
    win_tiles = span // LANES

    def logit_tiles(h, r):
        row = pl.multiple_of(r * ATTN_ROW_CHUNK, ATTN_ROW_CHUNK)
        s = s_scr[h, pl.ds(row, ATTN_ROW_CHUNK), :]
        tiles = [s[:, i * LANES:(i + 1) * LANES] for i in range(n_tiles)]
        if has_window:
            brow = pl.multiple_of((r % chunks_per_head) * ATTN_ROW_CHUNK, ATTN_ROW_CHUNK)
            bias = bias_scr[pl.ds(brow, ATTN_ROW_CHUNK), :]
            tiles = [t + bias[:, i * LANES:(i + 1) * LANES] if i < win_tiles else t for i, t in enumerate(tiles)]
        return row, tiles

    for h in range(n_kv):
        def row_max(r, carry, h=h):
            row, tiles = logit_tiles(h, r)
            sink = sink_ref[h * group + r // chunks_per_head]
            m = jnp.max(functools.reduce(jnp.maximum, tiles), axis=-1, keepdims=True)
            m_scr[h, pl.ds(row, ATTN_ROW_CHUNK), :] = jnp.broadcast_to(jnp.maximum(m, sink),
                                                                     (ATTN_ROW_CHUNK, LANES))
            return carry

        lax.fori_loop(0, n_chunks, row_max, 0, unroll=ATTN_UNROLL)

    for h in range(n_kv):
        def probs(r, carry, h=h):
            row, tiles = logit_tiles(h, r)
            sink = sink_ref[h * group + r // chunks_per_head]
            m = m_scr[h, pl.ds(row, ATTN_ROW_CHUNK), :]
            es = [jnp.exp(t - m) for t in tiles]
            for i, e in enumerate(es):
                p_scr[h, pl.ds(row, ATTN_ROW_CHUNK), i * LANES:(i + 1) * LANES] = e.astype(BF16)
            den = jnp.sum(functools.reduce(jnp.add, es), axis=-1, keepdims=True) + jnp.exp(sink - m)
            m_scr[h, pl.ds(row, ATTN_ROW_CHUNK), :] = 1.0 / den
            return carry

        lax.fori_loop(0, n_chunks, probs, 0, unroll=ATTN_UNROLL)

    for h in range(n_kv):
        o = _dot(p_scr[h, :, span:], vc[:, h * hd:(h + 1) * hd])
        if has_window:
            o = o + _dot(p_scr[h, :, :span], vw[:, h * hd:(h + 1) * hd])
        o = o * m_scr[h, :, :hd]
        for g in range(group):
            j = h * group + g
            o_ref[:, j * hd:(j + 1) * hd] = o[g * tq:(g + 1) * tq, :].astype(o_ref.dtype)


def _attention(q, k, v, kc, vc, sinks, has_window):
    b, l, qw = q.shape
    c, kvw = kc.shape[1], kc.shape[2]
    hd = kvw // N_KV_HEADS
    group = qw // kvw
    tq = Q_TILE if has_window else l
    assert l % tq == 0
    if has_window:
        assert l >= tq + 2 * WINDOW
    kern = functools.partial(_attn_kernel, tq=tq, seq=l, n_kv=N_KV_HEADS, group=group, hd=hd,
                             has_window=has_window)
    in_specs = [pl.BlockSpec(memory_space=pltpu.SMEM),
                pl.BlockSpec((None, tq, qw), lambda bb, t: (bb, t, 0))]
    args = [sinks, q]
    if has_window:
        in_specs += [pl.BlockSpec((None, l, kvw), lambda bb, t: (bb, 0, 0)),
                     pl.BlockSpec((None, l, kvw), lambda bb, t: (bb, 0, 0))]
        args += [k, v]
    in_specs += [pl.BlockSpec((None, c, kvw), lambda bb, t: (bb, 0, 0)),
                 pl.BlockSpec((None, c, kvw), lambda bb, t: (bb, 0, 0))]
    args += [kc, vc]
    n_keys = c + (tq + 2 * WINDOW if has_window else 0)
    scratch = [pltpu.VMEM((N_KV_HEADS, group * tq, n_keys), F32),
               pltpu.VMEM((N_KV_HEADS, group * tq, n_keys), BF16),
               pltpu.VMEM((N_KV_HEADS, group * tq, LANES), F32)]
    if has_window:
        scratch.append(pltpu.VMEM((tq, tq + 2 * WINDOW), F32))
    return pl.pallas_call(
        kern,
        grid=(b, l // tq),
        in_specs=in_specs,
        out_specs=pl.BlockSpec((None, tq, qw), lambda bb, t: (bb, t, 0)),
        out_shape=jax.ShapeDtypeStruct((b, l, qw), BF16),
        scratch_shapes=scratch,
        compiler_params=_cparams(2),
        name="win_attn" if has_window else "ctx_attn",
    )(*args)


def _proj_post_kernel(a_ref, w_ref, x_ref, gate_ref, g_ref, b_ref, *refs, alpha):
    route_in, o_ref, route_out = refs[:N_ROUTE_IN], refs[N_ROUTE_IN], refs[N_ROUTE_IN + 1:]
    y = _dot(a_ref[...], w_ref[...])
    z = alpha * x_ref[...] + gate_ref[...] * y
    x_new = _normalize(z, LN_EPS) * g_ref[...] + b_ref[...]
    o_ref[...] = x_new
    _route_tokens(x_new, *route_in, *route_out)


def _proj_post(a, w, x, mod_i, gate_j, ctx_row, ln_g, ln_b, alpha, rwt, rbias, window=None):
    first, b = window if window is not None else (0, x.shape[0])
    _, l, d = x.shape
    ka = a.shape[2]
    tm = min(ROW_TILE, l)
    assert l % tm == 0
    r_in, r_args, r_out, r_shape = _route_io(b, l, d, tm, mod_i, ctx_row, rwt, rbias)
    res = pl.pallas_call(
        functools.partial(_proj_post_kernel, alpha=alpha),
        grid=(b, l // tm),
        in_specs=[
            pl.BlockSpec((None, tm, ka), lambda bb, t: (bb, t, 0)),
            pl.BlockSpec((ka, d), lambda bb, t: (0, 0)),
            pl.BlockSpec((None, tm, d), lambda bb, t: (bb + first, t, 0)),
            _mod_spec(gate_j, d, ctx_row, 0),
            _row_spec(d),
            _row_spec(d),
        ] + r_in,
        out_specs=[pl.BlockSpec((None, tm, d), lambda bb, t: (bb, t, 0))] + r_out,
        out_shape=[jax.ShapeDtypeStruct((b, l, d), F32)] + r_shape,
        compiler_params=_cparams(2),
        name="proj_post_route",
    )(a, w, x, mod_i, ln_g, ln_b, *r_args)
    return res[0], res[1:]


def _fnet_a_kernel(x_ref, sh_ref, sc_ref, cs_ref, o_ref, *, d, gc):
    u = _modulate(x_ref[...], sh_ref[...], sc_ref[...]).astype(BF16)
    cs = cs_ref[...]
    for g in range(d // gc):
        r = _dot(u[:, g * gc:(g + 1) * gc], cs)
        o_ref[:, g * gc:(g + 1) * gc] = r[:, :gc].astype(BF16)
        o_ref[:, d + g * gc:d + (g + 1) * gc] = r[:, gc:].astype(BF16)


def _fnet_a(x, mod_i, ctx_row, cs):
    b, l, d = x.shape
    gc = d // F_GROUPS
    tm = min(ROW_TILE, l)
    assert l % tm == 0
    return pl.pallas_call(
        functools.partial(_fnet_a_kernel, d=d, gc=gc),
        grid=(b, l // tm),
        in_specs=[
            pl.BlockSpec((None, tm, d), lambda bb, t: (bb, t, 0)),
            _mod_spec(0, d, ctx_row, 0),
            _mod_spec(1, d, ctx_row, 0),
            pl.BlockSpec((gc, 2 * gc), lambda bb, t: (0, 0)),
        ],
        out_specs=pl.BlockSpec((None, tm, 2 * d), lambda bb, t: (bb, t, 0)),
        out_shape=jax.ShapeDtypeStruct((b, l, 2 * d), BF16),
        compiler_params=_cparams(2),
        name="fnet_chan_dft",
    )(x, mod_i, mod_i, cs)


def _fnet_b_kernel(cl_ref, sl_ref, ab_ref, wf_ref, x_ref, gate_ref, g_ref, b_ref, *refs, d, alpha):
    route_in, o_ref, route_out = refs[:N_ROUTE_IN], refs[N_ROUTE_IN], refs[N_ROUTE_IN + 1:]
    f = _dot(cl_ref[...], ab_ref[:, :d]) + _dot(sl_ref[...], ab_ref[:, d:])
    y = _dot(f.astype(BF16), wf_ref[...])
    z = alpha * x_ref[...] + gate_ref[...] * y
    x_new = _normalize(z, LN_EPS) * g_ref[...] + b_ref[...]
    o_ref[...] = x_new
    _route_tokens(x_new, *route_in, *route_out)


def _fnet_b(cl, sl, ab, wf, x, mod_i, gate_j, ctx_row, ln_g, ln_b, alpha, rwt, rbias):
    b, l, d = x.shape
    tm = min(ROW_TILE, l)
    assert l % tm == 0
    r_in, r_args, r_out, r_shape = _route_io(b, l, d, tm, mod_i, ctx_row, rwt, rbias)
    res = pl.pallas_call(
        functools.partial(_fnet_b_kernel, d=d, alpha=alpha),
        grid=(b, l // tm),
        in_specs=[
            pl.BlockSpec((tm, l), lambda bb, t: (t, 0)),
            pl.BlockSpec((tm, l), lambda bb, t: (t, 0)),
            pl.BlockSpec((None, l, 2 * d), lambda bb, t: (bb, 0, 0)),
            pl.BlockSpec((d, d), lambda bb, t: (0, 0)),
            pl.BlockSpec((None, tm, d), lambda bb, t: (bb, t, 0)),
            _mod_spec(gate_j, d, ctx_row, 0),
            _row_spec(d),
            _row_spec(d),
        ] + r_in,
        out_specs=[pl.BlockSpec((None, tm, d), lambda bb, t: (bb, t, 0))] + r_out,
        out_shape=[jax.ShapeDtypeStruct((b, l, d), F32)] + r_shape,
        compiler_params=_cparams(2),
        name="fnet_seq_dft_route",
    )(cl, sl, ab, wf, x, mod_i, ln_g, ln_b, *r_args)
    return res[0], res[1:]


def _dft_tables(n, scale):
    j = jnp.arange(n, dtype=jnp.int32)
    ang = ((j[:, None] * j[None, :]) % n).astype(F32) * (2.0 * math.pi / n)
    return jnp.cos(ang) * scale, jnp.sin(ang) * scale


def _route_tokens(x, sh_ref, sc_ref, rwt_ref, bias_ref, tri_ref, u_ref, te_ref, w_ref, rk_ref, cnt_ref):
    n_groups, topk_groups, top_k = N_EXPERT_GROUPS, TOPK_GROUPS, TOP_K
    u = _modulate(x, sh_ref[...], sc_ref[...])
    uh, ul = _split_bf16(u)
    u_ref[...] = _pack_halves(u)
    wh, wl = _split_bf16(rwt_ref[...])
    logits = _dot_nt(wh, uh) + _dot_nt(wl, uh) + _dot_nt(wh, ul)
    e, tm = logits.shape
    pg = e // n_groups
    neg = -jnp.inf
    scores = jax.nn.sigmoid(logits)
    sel = scores + bias_ref[...]
    sub = lax.broadcasted_iota(jnp.int32, (pg, tm), 0)
    groups = [sel[g * pg:(g + 1) * pg, :] for g in range(n_groups)]
    sgroups = [scores[g * pg:(g + 1) * pg, :] for g in range(n_groups)]

    gs_rows = []
    for s_g in groups:
        m1 = jnp.max(s_g, axis=0, keepdims=True)
        first = jnp.min(jnp.where(s_g == m1, sub, pg), axis=0, keepdims=True)
        m2 = jnp.max(jnp.where(sub == first, neg, s_g), axis=0, keepdims=True)
        gs_rows.append(m1 + m2)
    gs = jnp.concatenate(gs_rows, axis=0)
    gidx = lax.broadcasted_iota(jnp.int32, (n_groups, tm), 0)
    ok = jnp.zeros((n_groups, tm), F32)
    for _ in range(topk_groups):
        m = jnp.max(gs, axis=0, keepdims=True)
        first = jnp.min(jnp.where(gs == m, gidx, n_groups), axis=0, keepdims=True)
        hit = gidx == first
        ok = jnp.where(hit, 1.0, ok)
        gs = jnp.where(hit, neg, gs)

    cur = [jnp.where(ok[g:g + 1, :] > 0.0, groups[g], neg) for g in range(n_groups)]
    eidx = [sub + g * pg for g in range(n_groups)]
    chosen = [jnp.zeros((pg, tm), F32) for _ in range(n_groups)]
    e_rows, s_rows = [], []
    for _ in range(top_k):
        m = jnp.max(functools.reduce(jnp.maximum, cur), axis=0, keepdims=True)
        cand = functools.reduce(jnp.minimum, [jnp.where(cur[g] == m, eidx[g], e) for g in range(n_groups)])
        first = jnp.min(cand, axis=0, keepdims=True)
        picked = jnp.zeros((pg, tm), F32)
        for g in range(n_groups):
            hit = eidx[g] == first
            picked = picked + jnp.where(hit, sgroups[g], 0.0)
            chosen[g] = jnp.where(hit, 1.0, chosen[g])
            cur[g] = jnp.where(hit, neg, cur[g])
        e_rows.append(first)
        s_rows.append(jnp.sum(picked, axis=0, keepdims=True))
    w = jnp.concatenate(s_rows, axis=0)
    te_ref[...] = jnp.concatenate(e_rows, axis=0)
    w_ref[...] = w / jnp.sum(w, axis=0, keepdims=True) * ROUTED_SCALE

    sel_mask = jnp.concatenate(chosen, axis=0).astype(BF16)
    before = _dot(sel_mask, tri_ref[...])
    rk_rows = []
    for k in range(top_k):
        acc = jnp.zeros((pg, tm), F32)
        for g in range(n_groups):
            acc = acc + jnp.where(eidx[g] == e_rows[k], before[g * pg:(g + 1) * pg, :], 0.0)
        rk_rows.append(jnp.sum(acc, axis=0, keepdims=True))
    rk_ref[...] = jnp.concatenate(rk_rows, axis=0).astype(jnp.int32)
    cnt_ref[...] = _dot_nt(jnp.ones((8, tm), BF16), sel_mask)


N_ROUTE_IN = 5
N_ROUTE_OUT = 5


def _route_io(b, l, d, tm, mod_i, ctx_row, rwt, bias):
    e = rwt.shape[0]
    assert l % tm == 0 and e % N_EXPERT_GROUPS == 0
    nt = l // tm
    t = b * l
    row = lax.broadcasted_iota(jnp.int32, (tm, tm), 0)
    col = lax.broadcasted_iota(jnp.int32, (tm, tm), 1)
    tri = (row < col).astype(BF16)
    in_specs = [
        _mod_spec(3, d, ctx_row, 0),
        _mod_spec(4, d, ctx_row, 0),
        pl.BlockSpec((e, d), lambda bb, tt: (0, 0)),
        pl.BlockSpec((e, 1), lambda bb, tt: (0, 0)),
        pl.BlockSpec((tm, tm), lambda bb, tt: (0, 0)),
    ]
    args = [mod_i, mod_i, rwt, bias.reshape(e, 1).astype(F32), tri]
    tok_spec = pl.BlockSpec((TOP_K, tm), lambda bb, tt: (0, bb * nt + tt))
    out_specs = [
        pl.BlockSpec((None, tm, d // 2), lambda bb, tt: (bb, tt, 0)),
        tok_spec, tok_spec, tok_spec,
        pl.BlockSpec((None, 8, e), lambda bb, tt: (bb * nt + tt, 0, 0)),
    ]
    out_shape = [
        jax.ShapeDtypeStruct((b, l, d // 2), I32),
        jax.ShapeDtypeStruct((TOP_K, t), jnp.int32),
        jax.ShapeDtypeStruct((TOP_K, t), F32),
        jax.ShapeDtypeStruct((TOP_K, t), jnp.int32),
        jax.ShapeDtypeStruct((b * nt, 8, e), F32),
    ]
    return in_specs, args, out_specs, out_shape


def _expert_layout(cnt, tmb, n_blocks):
    cnt_i = cnt[:, 0, :].astype(jnp.int32)
    e = cnt_i.shape[1]
    counts = cnt_i.sum(axis=0)
    padded = (counts + tmb - 1) // tmb * tmb
    pend = jnp.cumsum(padded)
    base = (pend - padded)[None, :] + jnp.cumsum(cnt_i, axis=0) - cnt_i
    n_used = (pend[-1] // tmb).astype(jnp.int32)
    blk = jnp.arange(n_blocks, dtype=jnp.int32)
    block_e = jnp.sum((blk[:, None] * tmb >= pend[None, :]).astype(jnp.int32), axis=1)
    last_e = jnp.sum((jnp.maximum(n_used - 1, 0) * tmb >= pend).astype(jnp.int32))
    block_e = jnp.clip(jnp.where(blk < n_used, block_e, last_e), 0, e - 1).astype(jnp.int32)
    seg_end = (pend - padded + counts)[block_e]
    block_valid = jnp.clip(seg_end - blk * tmb, 0, tmb).astype(jnp.int32)
    return base.reshape(-1).astype(jnp.int32), block_e, block_valid, n_used.reshape(1)


def _pos_kernel(base_ref, te_ref, rk_ref, w_ref, pos_d_ref, pos_c_ref, w_c_ref, *, n_experts):
    i = pl.program_id(0)
    te = te_ref[...]
    pos = rk_ref[...]
    for e in range(n_experts):
        pos = pos + jnp.where(te == e, base_ref[i * n_experts + e], 0)
    w = w_ref[...]
    for c in range(pos_d_ref.shape[0]):
        pos_d_ref[c] = pos[:, c * DISPATCH_CHUNK:(c + 1) * DISPATCH_CHUNK]
    for c in range(pos_c_ref.shape[0]):
        pos_c_ref[c] = pos[:, c * COMBINE_CHUNK:(c + 1) * COMBINE_CHUNK]
        w_c_ref[c] = w[:, c * COMBINE_CHUNK:(c + 1) * COMBINE_CHUNK]


def _positions(base, top_e, rank, wts, n_experts, tm):
    k, t = top_e.shape
    assert tm % DISPATCH_CHUNK == 0 and tm % COMBINE_CHUNK == 0
    nd, nc = tm // DISPATCH_CHUNK, tm // COMBINE_CHUNK
    spec = pl.BlockSpec((k, tm), lambda i, base_ref: (0, i))
    return pl.pallas_call(
        functools.partial(_pos_kernel, n_experts=n_experts),
        grid_spec=pltpu.PrefetchScalarGridSpec(
            num_scalar_prefetch=1, grid=(t // tm,), in_specs=[spec, spec, spec],
            out_specs=[pl.BlockSpec((nd, k, DISPATCH_CHUNK), lambda i, base_ref: (i, 0, 0)),
                       pl.BlockSpec((nc, k, COMBINE_CHUNK), lambda i, base_ref: (i, 0, 0)),
                       pl.BlockSpec((nc, k, COMBINE_CHUNK), lambda i, base_ref: (i, 0, 0))]),
        out_shape=[jax.ShapeDtypeStruct((t // DISPATCH_CHUNK, k, DISPATCH_CHUNK), I32),
                   jax.ShapeDtypeStruct((t // COMBINE_CHUNK, k, COMBINE_CHUNK), I32),
                   jax.ShapeDtypeStruct((t // COMBINE_CHUNK, k, COMBINE_CHUNK), F32)],
        compiler_params=_cparams(1),
        name="moe_positions",
    )(base, top_e, rank, wts)


def _sc_dispatch(rows, pos3, cap):
    t, w = rows.shape
    n_chunks, top_k, n = pos3.shape
    n_workers = SC_CORES * SC_SUBCORES
    assert n_chunks * n == t and n <= LANES
    per_worker = -(-n_chunks // n_workers)
    mesh = plsc.VectorSubcoreMesh(core_axis_name="core", subcore_axis_name="subcore",
                                  num_cores=SC_CORES, num_subcores=SC_SUBCORES)

    def body(rows_hbm, pos_hbm, out_hbm, idx_v, rows_v, sem):
        wid = lax.axis_index("subcore") * SC_CORES + lax.axis_index("core")

        @pl.loop(0, per_worker)
        def _(j):
            c = wid * per_worker + j

            @pl.when(c < n_chunks)
            def _():
                pltpu.sync_copy(pos_hbm.at[c], idx_v)
                pltpu.sync_copy(rows_hbm.at[pl.ds(c * n, n)], rows_v)
                copies = [pltpu.async_copy(rows_v, out_hbm.at[idx_v.at[k]], sem) for k in range(top_k)]
                for cp in copies:
                    cp.wait()

    return pl.kernel(
        body,
        out_type=jax.ShapeDtypeStruct((cap, w), I32),
        mesh=mesh,
        scratch_types=[pltpu.VMEM((top_k, n), I32), pltpu.VMEM((n, w), I32), pltpu.SemaphoreType.DMA],
        name="moe_dispatch_sc",
    )(rows, pos3)


def _sc_combine(rows, pos3, wts3):
    cap, w = rows.shape
    n_chunks, top_k, n = pos3.shape
    t = n_chunks * n
    n_workers = SC_CORES * SC_SUBCORES
    assert n <= SC_LANES and w % SC_LANES == 0
    per_worker = -(-n_chunks // n_workers)
    mesh = plsc.VectorSubcoreMesh(core_axis_name="core", subcore_axis_name="subcore",
                                  num_cores=SC_CORES, num_subcores=SC_SUBCORES)

    def body(rows_hbm, pos_hbm, wts_hbm, out_hbm, idx_v, wts_v, acc_v, *rest):
        bufs, sem = rest[:top_k], rest[top_k]
        wid = lax.axis_index("subcore") * SC_CORES + lax.axis_index("core")

        @pl.loop(0, per_worker)
        def _(it):
            c = wid * per_worker + it

            @pl.when(c < n_chunks)
            def _():
                pltpu.sync_copy(pos_hbm.at[c], idx_v)
                pltpu.sync_copy(wts_hbm.at[c], wts_v)
                gathers = [pltpu.async_copy(rows_hbm.at[idx_v.at[k]], bufs[k], sem) for k in range(top_k)]
                for g in gathers:
                    g.wait()

                @pl.loop(0, n)
                def _(j):
                    tok = jnp.full((SC_LANES,), j, I32)
                    wk = [plsc.load_gather(wts_v, [jnp.full((SC_LANES,), k, I32), tok]) for k in range(top_k)]

                    @plsc.parallel_loop(0, w // SC_LANES, unroll=4)
                    def _(g):
                        col = g * SC_LANES
                        hi = jnp.zeros((SC_LANES,), F32)
                        lo = jnp.zeros((SC_LANES,), F32)
                        for k in range(top_k):
                            words = bufs[k][j, pl.ds(col, SC_LANES)]
                            hi = hi + wk[k] * plsc.bitcast(words & jnp.int32(-65536), F32)
                            lo = lo + wk[k] * plsc.bitcast(lax.shift_left(words, 16), F32)
                        acc_v[j, pl.ds(col, SC_LANES)] = hi
                        acc_v[j, pl.ds(w + col, SC_LANES)] = lo

                pltpu.sync_copy(acc_v, out_hbm.at[pl.ds(c * n, n)])

    return pl.kernel(
        body,
        out_type=jax.ShapeDtypeStruct((t, 2 * w), F32),
        mesh=mesh,
        scratch_types=[pltpu.VMEM((top_k, n), I32), pltpu.VMEM((top_k, n), F32), pltpu.VMEM((n, 2 * w), F32)]
        + [pltpu.VMEM((n, w), I32) for _ in range(top_k)] + [pltpu.SemaphoreType.DMA],
        compiler_params=pltpu.CompilerParams(needs_layout_passes=False),
        name="moe_combine_sc",
    )(rows, pos3, wts3)


def _gmm_kernel(be_ref, bv_ref, nu_ref, x_ref, *refs, tmb, per_step):
    w_refs, o_ref = refs[:-1], refs[-1]
    i = pl.program_id(0)

    @pl.when(i * per_step < nu_ref[0])
    def _():
        for s in range(per_step):
            wg_ref, wu_ref, wd_ref = w_refs[3 * s:3 * s + 3]
            words = x_ref[s * tmb:(s + 1) * tmb, :]
            half = words.shape[1]
            live = lax.broadcasted_iota(I32, (tmb, half), 0) < bv_ref[i * per_step + s]
            xa, xb = _unpack_halves(jnp.where(live, words, 0))
            g = _dot(xa, wg_ref[:half, :]) + _dot(xb, wg_ref[half:, :])
            u = _dot(xa, wu_ref[:half, :]) + _dot(xb, wu_ref[half:, :])
            o_ref[s * tmb:(s + 1) * tmb, :] = _pack_halves(_dot((_silu(g) * u).astype(BF16), wd_ref[...]))


def _grouped_experts(x_sorted, block_e, block_valid, n_used, wg, wu, wd, layer, tmb):
    cap, half = x_sorted.shape
    d = 2 * half
    ff = wg.shape[3]
    n_blocks = cap // tmb
    per_step = MOE_BLOCKS_PER_STEP if n_blocks % MOE_BLOCKS_PER_STEP == 0 else 1
    n_steps = n_blocks // per_step

    def row_map(i, be, bv, nu):
        return (jnp.minimum(i, (nu[0] + per_step - 1) // per_step - 1), 0)

    w_specs = []
    for s in range(per_step):
        w_map = functools.partial(lambda i, be, bv, nu, s: (layer, be[i * per_step + s], 0, 0), s=s)
        w_specs += [pl.BlockSpec((None, None, d, ff), w_map), pl.BlockSpec((None, None, d, ff), w_map),
                    pl.BlockSpec((None, None, ff, d), w_map)]

    return pl.pallas_call(
        functools.partial(_gmm_kernel, tmb=tmb, per_step=per_step),
        grid_spec=pltpu.PrefetchScalarGridSpec(
            num_scalar_prefetch=3,
            grid=(n_steps,),
            in_specs=[pl.BlockSpec((per_step * tmb, half), row_map)] + w_specs,
            out_specs=pl.BlockSpec((per_step * tmb, half), row_map),
        ),
        out_shape=jax.ShapeDtypeStruct((cap, half), I32),
        compiler_params=_cparams(1),
        name="moe_experts",
    )(block_e, block_valid, n_used, x_sorted, *([wg, wu, wd] * per_step))


def _moe_out_kernel(r_ref, sg_ref, su_ref, sd_ref, x_ref, sh_ref, sc_ref, gate_ref, g_ref, b_ref, o_ref, *, alpha):
    x = x_ref[...]
    u = _modulate(x, sh_ref[...], sc_ref[...]).astype(BF16)
    y = r_ref[...] + _dot((_silu(_dot(u, sg_ref[...])) * _dot(u, su_ref[...])).astype(BF16), sd_ref[...])
    z = alpha * x + gate_ref[...] * y
    o_ref[...] = _normalize(z, LN_EPS) * g_ref[...] + b_ref[...]


def _moe_out_into_kernel(*refs, alpha):
    _moe_out_kernel(*refs[:-2], refs[-1], alpha=alpha)


def _moe_out(routed, sg, su, sd, x, mod_i, ctx_row, ln_g, ln_b, alpha, place=None):
    b, l, d = x.shape
    ff = sg.shape[1]
    tm = min(ROW_TILE, l)
    assert l % tm == 0
    first, total, buffer = place if place is not None else (0, b, None)
    tok = pl.BlockSpec((None, tm, d), lambda bb, t: (bb, t, 0))
    in_specs = [
        tok,
        pl.BlockSpec((d, ff), lambda bb, t: (0, 0)),
        pl.BlockSpec((d, ff), lambda bb, t: (0, 0)),
        pl.BlockSpec((ff, d), lambda bb, t: (0, 0)),
        tok,
        _mod_spec(3, d, ctx_row, 0),
        _mod_spec(4, d, ctx_row, 0),
        _mod_spec(5, d, ctx_row, 0),
        _row_spec(d),
        _row_spec(d),
    ]
    args = [routed, sg, su, sd, x, mod_i, mod_i, mod_i, ln_g, ln_b]
    kern = functools.partial(_moe_out_kernel, alpha=alpha)
    aliases = {}
    if buffer is not None:
        in_specs.append(pl.BlockSpec(memory_space=pl.ANY))
        aliases = {len(args): 0}
        args.append(buffer)
        kern = functools.partial(_moe_out_into_kernel, alpha=alpha)
    return pl.pallas_call(
        kern,
        grid=(b, l // tm),
        in_specs=in_specs,
        out_specs=pl.BlockSpec((None, tm, d), lambda bb, t: (bb + first, t, 0)),
        out_shape=jax.ShapeDtypeStruct((total, l, d), F32),
        input_output_aliases=aliases,
        compiler_params=_cparams(2),
        name="moe_shared_post",
    )(*args)


def _moe_dispatch(shape, routing, n_experts):
    b, l, d = shape
    t = b * l
    tmb = MOE_ROW_TILE
    n_blocks = -(-(t * TOP_K) // tmb) + n_experts
    u2, top_e, wts, rank, cnt = routing
    base, block_e, block_valid, n_used = _expert_layout(cnt, tmb, n_blocks)
    pos_d, pos_c, wts_c = _positions(base, top_e, rank, wts, n_experts, min(ROW_TILE, l))
    x_sorted = _sc_dispatch(u2.reshape(t, d // 2), pos_d, n_blocks * tmb)
    return x_sorted, block_e, block_valid, n_used, pos_c, wts_c


def _moe_experts(shape, dispatched, wg, wu, wd, layer):
    x_sorted, block_e, block_valid, n_used, pos_c, wts_c = dispatched
    y_sorted = _grouped_experts(x_sorted, block_e, block_valid, n_used, wg, wu, wd, layer, MOE_ROW_TILE)
    return y_sorted, _sc_combine(y_sorted, pos_c, wts_c).reshape(shape)


def _after(token, value):
    if token is None:
        return value
    return lax.optimization_barrier((token, value))[1]


def _rope_tables(l, hd, n_q_heads, n_k_heads, rope):
    axis_rot = hd // 2
    qscale = hd ** -0.5
    if rope:
        t = jnp.arange(l, dtype=jnp.int32)
        r = (t // GRID_W).astype(F32)
        col = (t % GRID_W).astype(F32)
        inv = ROPE_BASE ** (-jnp.arange(0, axis_rot, 2, dtype=F32) / axis_rot)
        ar = r[:, None] * inv
        ac = col[:, None] * inv
        ang = jnp.concatenate([ar, ar, ac, ac], axis=-1)
        cos, sin = jnp.cos(ang), jnp.sin(ang)
    else:
        cos, sin = jnp.ones((l, hd), F32), jnp.zeros((l, hd), F32)
    cos_t = jnp.concatenate([jnp.tile(cos, (1, n_q_heads)) * qscale, jnp.tile(cos, (1, n_k_heads))], axis=1)
    sin_t = jnp.concatenate([jnp.tile(sin, (1, n_q_heads)) * qscale, jnp.tile(sin, (1, n_k_heads))], axis=1)
    return cos_t, sin_t


def _rot_columns(w, hd):
    d, n = w.shape
    q = hd // 4
    w4 = w.reshape(d, n // (2 * q), 2, q)
    return jnp.stack([-w4[:, :, 1, :], w4[:, :, 0, :]], axis=2).reshape(d, n)


def kernel(x, c, ctx, c_ctx, w_ada, b_ada, ln_g, ln_b, attn_w_qkv, attn_w_o, attn_sinks, fnet_w,
           router_w, router_bias, exp_w_gate, exp_w_up, exp_w_down, sh_w_gate, sh_w_up, sh_w_down):
    b, l, d = x.shape
    cl = ctx.shape[1]
    depth = w_ada.shape[0]
    n_heads = attn_sinks.shape[1]
    hd = d // n_heads
    qw = n_heads * hd
    kvw = N_KV_HEADS * hd
    alpha = (2.0 * depth) ** 0.25
    gc = d // F_GROUPS

    mp = -(-(b + 1) // 16) * 16
    cvec = jnp.concatenate([c, c_ctx[None, :], jnp.zeros((mp - b - 1, d), F32)], axis=0)
    mod = _ada_all(cvec, w_ada, b_ada).reshape(depth, mp, 6, 1, d)

    def run_stream(x, h, mod, first, count):
        b = count
        token, _ = yield None
        x = _after(token, x)
        window = (first, count)
        for i in range(depth):
            kind = i % N_MIXERS
            j = i // N_MIXERS
            update_ctx = any((m % N_MIXERS) == 0 for m in range(i + 1, depth))
            mod_i = mod[i]
            g1 = ln_g[i, 0].reshape(1, d)
            b1 = ln_b[i, 0].reshape(1, d)
            g2 = ln_g[i, 1].reshape(1, d)
            b2 = ln_b[i, 1].reshape(1, d)
            rwt, rbias = moe_w[i][:2]

            if kind == 0:
                w_all, w_o = attn_w[j]
                q, k, v = _qkv_proj(x, mod_i, None, w_all, cos_l, sin_l, qw, kvw, window)
                q_c, k_c, v_c = _qkv_proj(h, mod_i, b, w_all, cos_c, sin_c, qw, kvw, window)
                o = _attention(q, k, v, k_c, v_c, attn_sinks[j], True)
                x, x_route = _proj_post(o, w_o, x, mod_i, 2, None, g1, b1, alpha, rwt, rbias, window)
                if update_ctx:
                    o_c = _attention(q_c, None, None, k_c, v_c, attn_sinks[j], False)
                    h, h_route = _proj_post(o_c, w_o, h, mod_i, 2, b, g1, b1, alpha, rwt, rbias, window)
            else:
                if window is not None:
                    x, h = x[first:first + count], h[first:first + count]
                streams = [(x, None)] + ([(h, b)] if update_ctx else [])
                outs = []
                for s, ctx_row in streams:
                    c_l, s_l = dft_seq[s.shape[1]]
                    ab = _fnet_a(s, mod_i, ctx_row, cs)
                    outs.append(_fnet_b(c_l, s_l, ab, fnet_wb[j], s, mod_i, 2, ctx_row, g1, b1, alpha, rwt, rbias))
                x, x_route = outs[0]
                if update_ctx:
                    h, h_route = outs[1]

            wg, wu, wd, sg, su, sd = moe_w[i][2:]
            n_experts = rwt.shape[0]
            x_disp = _moe_dispatch(x.shape, x_route, n_experts)
            h_disp = _moe_dispatch(h.shape, h_route, n_experts) if update_ctx else None
            window = None
            token, _ = yield x
            x_disp = (_after(token, x_disp[0]),) + x_disp[1:]
            y_sorted, x_routed = _moe_experts(x.shape, x_disp, wg, wu, wd, i)
            h_routed = _moe_experts(h.shape, h_disp, wg, wu, wd, i)[1] if update_ctx else None
            token, buffer = yield y_sorted
            x_routed = _after(token, x_routed)
            place = (first, total, buffer) if i == depth - 1 else None
            x = _moe_out(x_routed, sg, su, sd, x, mod_i, None, g2, b2, alpha, place)
            if update_ctx:
                h = _moe_out(h_routed, sg, su, sd, h, mod_i, b, g2, b2, alpha)
        return x

    attn_w = []
    for j in range(attn_w_qkv.shape[0]):
        w = attn_w_qkv[j]
        w_all = jnp.concatenate([w, _rot_columns(w[:, :qw + kvw], hd)], axis=1).astype(BF16)
        attn_w.append((w_all, attn_w_o[j].astype(BF16)))
    cos_l, sin_l = _rope_tables(l, hd, n_heads, N_KV_HEADS, True)
    cos_c, sin_c = _rope_tables(cl, hd, n_heads, N_KV_HEADS, False)
    fnet_wb = [fnet_w[j].astype(BF16) for j in range(fnet_w.shape[0])]
    cc, sc = _dft_tables(gc, gc ** -0.5)
    cs = jnp.concatenate([cc, sc], axis=1).astype(BF16)
    dft_seq = {}
    for ls in (l, cl):
        c_l, s_l = _dft_tables(ls, ls ** -0.5)
        dft_seq[ls] = (c_l.astype(BF16), (-s_l).astype(BF16))
    wg_all, wu_all, wd_all = exp_w_gate.astype(BF16), exp_w_up.astype(BF16), exp_w_down.astype(BF16)
    moe_w = [(router_w[i].T, router_bias[i], wg_all, wu_all, wd_all, sh_w_gate[i].astype(BF16),
              sh_w_up[i].astype(BF16), sh_w_down[i].astype(BF16)) for i in range(depth)]

    n_streams = N_STREAMS if b % N_STREAMS == 0 else 1
    bs = b // n_streams
    total = b
    gens = []
    for s in range(n_streams):
        mod_s = jnp.concatenate([mod[:, s * bs:(s + 1) * bs], mod[:, b:b + 1]], axis=1)
        gens.append(run_stream(x, ctx, mod_s, s * bs, bs))
    for g in gens:
        next(g)
    running = list(range(n_streams))
    token = result = None
    while running:
        for s in list(running):
            try:
                token = gens[s].send((token, result))
            except StopIteration as done:
                token = result = done.value
                running.remove(s)
    return result
```

```python
import functools
import math

import jax
import jax.numpy as jnp
from jax import lax
from jax.experimental import pallas as pl
from jax.experimental.pallas import tpu as pltpu
from jax.experimental.pallas import tpu_sc as plsc

F32 = jnp.float32
BF16 = jnp.bfloat16
I32 = jnp.int32

N_KV_HEADS = 4
WINDOW = 128
GRID_W = 64
ROPE_BASE = 10000.0
F_GROUPS = 4
TOP_K = 8
N_EXPERT_GROUPS = 8
TOPK_GROUPS = 4
ROUTED_SCALE = 2.5
N_MIXERS = 2
LN_EPS = 1e-5
MOD_EPS = 1e-6

LANES = 128
VMEM_LIMIT_BYTES = 52 * 1024 * 1024
ROW_TILE = 512
QKV_ROW_TILE = 512
Q_TILE = 128
ATTN_ROW_CHUNK = 32
ATTN_UNROLL = 16
MOE_ROW_TILE = 512
MOE_BLOCKS_PER_STEP = 2
DISPATCH_CHUNK = 128
COMBINE_CHUNK = 16
N_STREAMS = 2
SC_CORES = 2
SC_SUBCORES = 16
SC_LANES = 16
NEG_BIG = -1e30


def _cparams(n_axes):
    return pltpu.CompilerParams(dimension_semantics=("arbitrary",) * n_axes,
                                vmem_limit_bytes=VMEM_LIMIT_BYTES)


def _dot(a, b):
    return jnp.dot(a, b, preferred_element_type=F32)


def _dot_nt(a, b):
    return lax.dot_general(a, b, (((1,), (1,)), ((), ())), preferred_element_type=F32)


def _split_bf16(a):
    hi = a.astype(BF16)
    lo = (a - hi.astype(F32)).astype(BF16)
    return hi, lo


def _normalize(x, eps):
    mu = jnp.mean(x, axis=-1, keepdims=True)
    xc = x - mu
    var = jnp.mean(xc * xc, axis=-1, keepdims=True)
    return xc * lax.rsqrt(var + eps)


def _modulate(x, shift, scale):
    return _normalize(x, MOD_EPS) * (1.0 + scale) + shift


def _silu(x):
    return x * jax.nn.sigmoid(x)


def _pack_halves(x):
    n = x.shape[1] // 2
    r = x.astype(BF16).astype(F32)
    hi = pltpu.bitcast(r[:, :n], I32)
    lo = pltpu.bitcast(r[:, n:], I32)
    return hi | lax.shift_right_logical(lo, 16)


def _unpack_halves(w):
    a = pltpu.bitcast(w & jnp.int32(-65536), F32).astype(BF16)
    b = pltpu.bitcast(lax.shift_left(w, 16), F32).astype(BF16)
    return a, b


def _ada_kernel(c_ref, w_ref, b_ref, o_ref):
    s = _silu(c_ref[...])
    sh, sl = _split_bf16(s)
    wh, wl = _split_bf16(w_ref[...])
    o_ref[...] = _dot(sh, wh) + _dot(sl, wh) + _dot(sh, wl) + b_ref[...]


def _ada_all(cvec, w_ada, b_ada):
    depth, d, n = w_ada.shape
    mp = cvec.shape[0]
    tn = 1536
    assert n % tn == 0
    return pl.pallas_call(
        _ada_kernel,
        grid=(depth, n // tn),
        in_specs=[
            pl.BlockSpec((mp, d), lambda i, j: (0, 0)),
            pl.BlockSpec((None, d, tn), lambda i, j: (i, 0, j)),
            pl.BlockSpec((None, 1, tn), lambda i, j: (i, 0, j)),
        ],
        out_specs=pl.BlockSpec((None, mp, tn), lambda i, j: (i, 0, j)),
        out_shape=jax.ShapeDtypeStruct((depth, mp, n), F32),
        compiler_params=_cparams(2),
        name="ada_mod",
    )(cvec, w_ada, b_ada.reshape(depth, 1, n))


def _mod_spec(j, d, ctx_row, batch_axis):
    if ctx_row is None:
        return pl.BlockSpec((None, None, 1, d), lambda *g: (g[batch_axis], j, 0, 0))
    return pl.BlockSpec((None, None, 1, d), lambda *g: (ctx_row, j, 0, 0))


def _row_spec(d):
    return pl.BlockSpec((1, d), lambda *g: (0, 0))


def _qkv_kernel(x_ref, sh_ref, sc_ref, w_ref, cos_ref, sin_ref, q_ref, k_ref, v_ref, *, qw, kvw):
    u = _modulate(x_ref[...], sh_ref[...], sc_ref[...]).astype(BF16)
    r = _dot(u, w_ref[...])
    qk = r[:, :qw + kvw] * cos_ref[...] + r[:, qw + 2 * kvw:] * sin_ref[...]
    q_ref[...] = qk[:, :qw].astype(BF16)
    k_ref[...] = qk[:, qw:].astype(BF16)
    v_ref[...] = r[:, qw + kvw:qw + 2 * kvw].astype(BF16)


def _qkv_proj(x, mod_i, ctx_row, w_all, cos_t, sin_t, qw, kvw, window=None):
    first, b = window if window is not None else (0, x.shape[0])
    _, l, d = x.shape
    tm = min(QKV_ROW_TILE, l)
    assert l % tm == 0
    n_all = w_all.shape[1]
    return pl.pallas_call(
        functools.partial(_qkv_kernel, qw=qw, kvw=kvw),
        grid=(l // tm, b),
        in_specs=[
            pl.BlockSpec((None, tm, d), lambda t, bb: (bb + first, t, 0)),
            _mod_spec(0, d, ctx_row, 1),
            _mod_spec(1, d, ctx_row, 1),
            pl.BlockSpec((d, n_all), lambda t, bb: (0, 0)),
            pl.BlockSpec((tm, qw + kvw), lambda t, bb: (t, 0)),
            pl.BlockSpec((tm, qw + kvw), lambda t, bb: (t, 0)),
        ],
        out_specs=[
            pl.BlockSpec((None, tm, qw), lambda t, bb: (bb, t, 0)),
            pl.BlockSpec((None, tm, kvw), lambda t, bb: (bb, t, 0)),
            pl.BlockSpec((None, tm, kvw), lambda t, bb: (bb, t, 0)),
        ],
        out_shape=[
            jax.ShapeDtypeStruct((b, l, qw), BF16),
            jax.ShapeDtypeStruct((b, l, kvw), BF16),
            jax.ShapeDtypeStruct((b, l, kvw), BF16),
        ],
        compiler_params=_cparams(2),
        name="qkv_rope",
    )(x, mod_i, mod_i, w_all, cos_t, sin_t)


def _attn_kernel(sink_ref, q_ref, *refs, tq, seq, n_kv, group, hd, has_window):
    if has_window:
        k_ref, v_ref, kc_ref, vc_ref, o_ref, s_scr, p_scr, m_scr, bias_scr = refs
    else:
        kc_ref, vc_ref, o_ref, s_scr, p_scr, m_scr = refs
    q = q_ref[...]
    kc = kc_ref[...]
    vc = vc_ref[...]
    span = tq + 2 * WINDOW if has_window else 0
    rows = group * tq
    if has_window:
        q0 = pl.program_id(1) * tq
        start = pl.multiple_of(jnp.clip(q0 - WINDOW, 0, seq - span), LANES)
        kw = k_ref[pl.ds(start, span), :]
        vw = v_ref[pl.ds(start, span), :]
        qpos = q0 + lax.broadcasted_iota(jnp.int32, (tq, span), 0)
        kpos = start + lax.broadcasted_iota(jnp.int32, (tq, span), 1)
        bias_scr[...] = jnp.where(jnp.abs(qpos - kpos) <= WINDOW, 0.0, NEG_BIG)

    for h in range(n_kv):
        heads = [h * group + g for g in range(group)]
        qh = jnp.concatenate([q[:, j * hd:(j + 1) * hd] for j in heads], axis=0)
        if has_window:
            s_scr[h, :, :span] = _dot_nt(qh, kw[:, h * hd:(h + 1) * hd])
        s_scr[h, :, span:] = _dot_nt(qh, kc[:, h * hd:(h + 1) * hd])

    chunks_per_head = tq // ATTN_ROW_CHUNK
    n_chunks = rows // ATTN_ROW_CHUNK
    n_tiles = (span + kc.shape[0]) // LANES
    win_tiles = span // LANES

    def logit_tiles(h, r):
        row = pl.multiple_of(r * ATTN_ROW_CHUNK, ATTN_ROW_CHUNK)
        s = s_scr[h, pl.ds(row, ATTN_ROW_CHUNK), :]
        tiles = [s[:, i * LANES:(i + 1) * LANES] for i in range(n_tiles)]
        if has_window:
            brow = pl.multiple_of((r % chunks_per_head) * ATTN_ROW_CHUNK, ATTN_ROW_CHUNK)
            bias = bias_scr[pl.ds(brow, ATTN_ROW_CHUNK), :]
            tiles = [t + bias[:, i * LANES:(i + 1) * LANES] if i < win_tiles else t for i, t in enumerate(tiles)]
        return row, tiles

    for h in range(n_kv):
        def row_max(r, carry, h=h):
            row, tiles = logit_tiles(h, r)
            sink = sink_ref[h * group + r // chunks_per_head]
            m = jnp.max(functools.reduce(jnp.maximum, tiles), axis=-1, keepdims=True)
            m_scr[h, pl.ds(row, ATTN_ROW_CHUNK), :] = jnp.broadcast_to(jnp.maximum(m, sink),
                                                                     (ATTN_ROW_CHUNK, LANES))
            return carry

        lax.fori_loop(0, n_chunks, row_max, 0, unroll=ATTN_UNROLL)

    for h in range(n_kv):
        def probs(r, carry, h=h):
            row, tiles = logit_tiles(h, r)
            sink = sink_ref[h * group + r // chunks_per_head]
            m = m_scr[h, pl.ds(row, ATTN_ROW_CHUNK), :]
            es = [jnp.exp(t - m) for t in tiles]
            for i, e in enumerate(es):
                p_scr[h, pl.ds(row, ATTN_ROW_CHUNK), i * LANES:(i + 1) * LANES] = e.astype(BF16)
            den = jnp.sum(functools.reduce(jnp.add, es), axis=-1, keepdims=True) + jnp.exp(sink - m)
            m_scr[h, pl.ds(row, ATTN_ROW_CHUNK), :] = 1.0 / den
            return carry

        lax.fori_loop(0, n_chunks, probs, 0, unroll=ATTN_UNROLL)

    for h in range(n_kv):
        o = _dot(p_scr[h, :, span:], vc[:, h * hd:(h + 1) * hd])
        if has_window:
            o = o + _dot(p_scr[h, :, :span], vw[:, h * hd:(h + 1) * hd])
        o = o * m_scr[h, :, :hd]
        for g in range(group):
            j = h * group + g
            o_ref[:, j * hd:(j + 1) * hd] = o[g * tq:(g + 1) * tq, :].astype(o_ref.dtype)


def _attention(q, k, v, kc, vc, sinks, has_window):
    b, l, qw = q.shape
    c, kvw = kc.shape[1], kc.shape[2]
    hd = kvw // N_KV_HEADS
    group = qw // kvw
    tq = Q_TILE if has_window else l
    assert l % tq == 0
    if has_window:
        assert l >= tq + 2 * WINDOW
    kern = functools.partial(_attn_kernel, tq=tq, seq=l, n_kv=N_KV_HEADS, group=group, hd=hd,
                             has_window=has_window)
    in_specs = [pl.BlockSpec(memory_space=pltpu.SMEM),
                pl.BlockSpec((None, tq, qw), lambda bb, t: (bb, t, 0))]
    args = [sinks, q]
    if has_window:
        in_specs += [pl.BlockSpec((None, l, kvw), lambda bb, t: (bb, 0, 0)),
                     pl.BlockSpec((None, l, kvw), lambda bb, t: (bb, 0, 0))]
        args += [k, v]
    in_specs += [pl.BlockSpec((None, c, kvw), lambda bb, t: (bb, 0, 0)),
                 pl.BlockSpec((None, c, kvw), lambda bb, t: (bb, 0, 0))]
    args += [kc, vc]
    n_keys = c + (tq + 2 * WINDOW if has_window else 0)
    scratch = [pltpu.VMEM((N_KV_HEADS, group * tq, n_keys), F32),
               pltpu.VMEM((N_KV_HEADS, group * tq, n_keys), BF16),
               pltpu.VMEM((N_KV_HEADS, group * tq, LANES), F32)]
    if has_window:
        scratch.append(pltpu.VMEM((tq, tq + 2 * WINDOW), F32))
    return pl.pallas_call(
        kern,
        grid=(b, l // tq),
        in_specs=in_specs,
        out_specs=pl.BlockSpec((None, tq, qw), lambda bb, t: (bb, t, 0)),
        out_shape=jax.ShapeDtypeStruct((b, l, qw), BF16),
        scratch_shapes=scratch,
        compiler_params=_cparams(2),
        name="win_attn" if has_window else "ctx_attn",
    )(*args)


def _proj_post_kernel(a_ref, w_ref, x_ref, gate_ref, g_ref, b_ref, *refs, alpha):
    route_in, o_ref, route_out = refs[:N_ROUTE_IN], refs[N_ROUTE_IN], refs[N_ROUTE_IN + 1:]
    y = _dot(a_ref[...], w_ref[...])
    z = alpha * x_ref[...] + gate_ref[...] * y
    x_new = _normalize(z, LN_EPS) * g_ref[...] + b_ref[...]
    o_ref[...] = x_new
    _route_tokens(x_new, *route_in, *route_out)


def _proj_post(a, w, x, mod_i, gate_j, ctx_row, ln_g, ln_b, alpha, rwt, rbias, window=None):
    first, b = window if window is not None else (0, x.shape[0])
    _, l, d = x.shape
    ka = a.shape[2]
    tm = min(ROW_TILE, l)
    assert l % tm == 0
    r_in, r_args, r_out, r_shape = _route_io(b, l, d, tm, mod_i, ctx_row, rwt, rbias)
    res = pl.pallas_call(
        functools.partial(_proj_post_kernel, alpha=alpha),
        grid=(b, l // tm),
        in_specs=[
            pl.BlockSpec((None, tm, ka), lambda bb, t: (bb, t, 0)),
            pl.BlockSpec((ka, d), lambda bb, t: (0, 0)),
            pl.BlockSpec((None, tm, d), lambda bb, t: (bb + first, t, 0)),
            _mod_spec(gate_j, d, ctx_row, 0),
            _row_spec(d),
            _row_spec(d),
        ] + r_in,
        out_specs=[pl.BlockSpec((None, tm, d), lambda bb, t: (bb, t, 0))] + r_out,
        out_shape=[jax.ShapeDtypeStruct((b, l, d), F32)] + r_shape,
        compiler_params=_cparams(2),
        name="proj_post_route",
    )(a, w, x, mod_i, ln_g, ln_b, *r_args)
    return res[0], res[1:]


def _fnet_a_kernel(x_ref, sh_ref, sc_ref, cs_ref, o_ref, *, d, gc):
    u = _modulate(x_ref[...], sh_ref[...], sc_ref[...]).astype(BF16)
    cs = cs_ref[...]
    for g in range(d // gc):
        r = _dot(u[:, g * gc:(g + 1) * gc], cs)
        o_ref[:, g * gc:(g + 1) * gc] = r[:, :gc].astype(BF16)
        o_ref[:, d + g * gc:d + (g + 1) * gc] = r[:, gc:].astype(BF16)


def _fnet_a(x, mod_i, ctx_row, cs):
    b, l, d = x.shape
    gc = d // F_GROUPS
    tm = min(ROW_TILE, l)
    assert l % tm == 0
    return pl.pallas_call(
        functools.partial(_fnet_a_kernel, d=d, gc=gc),
        grid=(b, l // tm),
        in_specs=[
            pl.BlockSpec((None, tm, d), lambda bb, t: (bb, t, 0)),
            _mod_spec(0, d, ctx_row, 0),
            _mod_spec(1, d, ctx_row, 0),
            pl.BlockSpec((gc, 2 * gc), lambda bb, t: (0, 0)),
        ],
        out_specs=pl.BlockSpec((None, tm, 2 * d), lambda bb, t: (bb, t, 0)),
        out_shape=jax.ShapeDtypeStruct((b, l, 2 * d), BF16),
        compiler_params=_cparams(2),
        name="fnet_chan_dft",
    )(x, mod_i, mod_i, cs)


def _fnet_b_kernel(cl_ref, sl_ref, ab_ref, wf_ref, x_ref, gate_ref, g_ref, b_ref, *refs, d, alpha):
    route_in, o_ref, route_out = refs[:N_ROUTE_IN], refs[N_ROUTE_IN], refs[N_ROUTE_IN + 1:]
    f = _dot(cl_ref[...], ab_ref[:, :d]) + _dot(sl_ref[...], ab_ref[:, d:])
    y = _dot(f.astype(BF16), wf_ref[...])
    z = alpha * x_ref[...] + gate_ref[...] * y
    x_new = _normalize(z, LN_EPS) * g_ref[...] + b_ref[...]
    o_ref[...] = x_new
    _route_tokens(x_new, *route_in, *route_out)


def _fnet_b(cl, sl, ab, wf, x, mod_i, gate_j, ctx_row, ln_g, ln_b, alpha, rwt, rbias):
    b, l, d = x.shape
    tm = min(ROW_TILE, l)
    assert l % tm == 0
    r_in, r_args, r_out, r_shape = _route_io(b, l, d, tm, mod_i, ctx_row, rwt, rbias)
    res = pl.pallas_call(
        functools.partial(_fnet_b_kernel, d=d, alpha=alpha),
        grid=(b, l // tm),
        in_specs=[
            pl.BlockSpec((tm, l), lambda bb, t: (t, 0)),
            pl.BlockSpec((tm, l), lambda bb, t: (t, 0)),
            pl.BlockSpec((None, l, 2 * d), lambda bb, t: (bb, 0, 0)),
            pl.BlockSpec((d, d), lambda bb, t: (0, 0)),
            pl.BlockSpec((None, tm, d), lambda bb, t: (bb, t, 0)),
            _mod_spec(gate_j, d, ctx_row, 0),
            _row_spec(d),
            _row_spec(d),
        ] + r_in,
        out_specs=[pl.BlockSpec((None, tm, d), lambda bb, t: (bb, t, 0))] + r_out,
        out_shape=[jax.ShapeDtypeStruct((b, l, d), F32)] + r_shape,
        compiler_params=_cparams(2),
        name="fnet_seq_dft_route",
    )(cl, sl, ab, wf, x, mod_i, ln_g, ln_b, *r_args)
    return res[0], res[1:]


def _dft_tables(n, scale):
    j = jnp.arange(n, dtype=jnp.int32)
    ang = ((j[:, None] * j[None, :]) % n).astype(F32) * (2.0 * math.pi / n)
    return jnp.cos(ang) * scale, jnp.sin(ang) * scale


def _route_tokens(x, sh_ref, sc_ref, rwt_ref, bias_ref, tri_ref, u_ref, te_ref, w_ref, rk_ref, cnt_ref):
    n_groups, topk_groups, top_k = N_EXPERT_GROUPS, TOPK_GROUPS, TOP_K
    u = _modulate(x, sh_ref[...], sc_ref[...])
    uh, ul = _split_bf16(u)
    u_ref[...] = _pack_halves(u)
    wh, wl = _split_bf16(rwt_ref[...])
    logits = _dot_nt(wh, uh) + _dot_nt(wl, uh) + _dot_nt(wh, ul)
    e, tm = logits.shape
    pg = e // n_groups
    neg = -jnp.inf
    scores = jax.nn.sigmoid(logits)
    sel = scores + bias_ref[...]
    sub = lax.broadcasted_iota(jnp.int32, (pg, tm), 0)
    groups = [sel[g * pg:(g + 1) * pg, :] for g in range(n_groups)]
    sgroups = [scores[g * pg:(g + 1) * pg, :] for g in range(n_groups)]

    gs_rows = []
    for s_g in groups:
        m1 = jnp.max(s_g, axis=0, keepdims=True)
        first = jnp.min(jnp.where(s_g == m1, sub, pg), axis=0, keepdims=True)
        m2 = jnp.max(jnp.where(sub == first, neg, s_g), axis=0, keepdims=True)
        gs_rows.append(m1 + m2)
    gs = jnp.concatenate(gs_rows, axis=0)
    gidx = lax.broadcasted_iota(jnp.int32, (n_groups, tm), 0)
    ok = jnp.zeros((n_groups, tm), F32)
    for _ in range(topk_groups):
        m = jnp.max(gs, axis=0, keepdims=True)
        first = jnp.min(jnp.where(gs == m, gidx, n_groups), axis=0, keepdims=True)
        hit = gidx == first
        ok = jnp.where(hit, 1.0, ok)
        gs = jnp.where(hit, neg, gs)

    cur = [jnp.where(ok[g:g + 1, :] > 0.0, groups[g], neg) for g in range(n_groups)]
    eidx = [sub + g * pg for g in range(n_groups)]
    chosen = [jnp.zeros((pg, tm), F32) for _ in range(n_groups)]
    e_rows, s_rows = [], []
    for _ in range(top_k):
        m = jnp.max(functools.reduce(jnp.maximum, cur), axis=0, keepdims=True)
        cand = functools.reduce(jnp.minimum, [jnp.where(cur[g] == m, eidx[g], e) for g in range(n_groups)])
        first = jnp.min(cand, axis=0, keepdims=True)
        picked = jnp.zeros((pg, tm), F32)
        for g in range(n_groups):
            hit = eidx[g] == first
            picked = picked + jnp.where(hit, sgroups[g], 0.0)
            chosen[g] = jnp.where(hit, 1.0, chosen[g])
            cur[g] = jnp.where(hit, neg, cur[g])
        e_rows.append(first)
        s_rows.append(jnp.sum(picked, axis=0, keepdims=True))
    w = jnp.concatenate(s_rows, axis=0)
    te_ref[...] = jnp.concatenate(e_rows, axis=0)
    w_ref[...] = w / jnp.sum(w, axis=0, keepdims=True) * ROUTED_SCALE

    sel_mask = jnp.concatenate(chosen, axis=0).astype(BF16)
    before = _dot(sel_mask, tri_ref[...])
    rk_rows = []
    for k in range(top_k):
        acc = jnp.zeros((pg, tm), F32)
        for g in range(n_groups):
            acc = acc + jnp.where(eidx[g] == e_rows[k], before[g * pg:(g + 1) * pg, :], 0.0)
        rk_rows.append(jnp.sum(acc, axis=0, keepdims=True))
    rk_ref[...] = jnp.concatenate(rk_rows, axis=0).astype(jnp.int32)
    cnt_ref[...] = _dot_nt(jnp.ones((8, tm), BF16), sel_mask)


N_ROUTE_IN = 5
N_ROUTE_OUT = 5


def _route_io(b, l, d, tm, mod_i, ctx_row, rwt, bias):
    e = rwt.shape[0]
    assert l % tm == 0 and e % N_EXPERT_GROUPS == 0
    nt = l // tm
    t = b * l
    row = lax.broadcasted_iota(jnp.int32, (tm, tm), 0)
    col = lax.broadcasted_iota(jnp.int32, (tm, tm), 1)
    tri = (row < col).astype(BF16)
    in_specs = [
        _mod_spec(3, d, ctx_row, 0),
        _mod_spec(4, d, ctx_row, 0),
        pl.BlockSpec((e, d), lambda bb, tt: (0, 0)),
        pl.BlockSpec((e, 1), lambda bb, tt: (0, 0)),
        pl.BlockSpec((tm, tm), lambda bb, tt: (0, 0)),
    ]
    args = [mod_i, mod_i, rwt, bias.reshape(e, 1).astype(F32), tri]
    tok_spec = pl.BlockSpec((TOP_K, tm), lambda bb, tt: (0, bb * nt + tt))
    out_specs = [
        pl.BlockSpec((None, tm, d // 2), lambda bb, tt: (bb, tt, 0)),
        tok_spec, tok_spec, tok_spec,
        pl.BlockSpec((None, 8, e), lambda bb, tt: (bb * nt + tt, 0, 0)),
    ]
    out_shape = [
        jax.ShapeDtypeStruct((b, l, d // 2), I32),
        jax.ShapeDtypeStruct((TOP_K, t), jnp.int32),
        jax.ShapeDtypeStruct((TOP_K, t), F32),
        jax.ShapeDtypeStruct((TOP_K, t), jnp.int32),
        jax.ShapeDtypeStruct((b * nt, 8, e), F32),
    ]
    return in_specs, args, out_specs, out_shape


def _expert_layout(cnt, tmb, n_blocks):
    cnt_i = cnt[:, 0, :].astype(jnp.int32)
    e = cnt_i.shape[1]
    counts = cnt_i.sum(axis=0)
    padded = (counts + tmb - 1) // tmb * tmb
    pend = jnp.cumsum(padded)
    base = (pend - padded)[None, :] + jnp.cumsum(cnt_i, axis=0) - cnt_i
    n_used = (pend[-1] // tmb).astype(jnp.int32)
    blk = jnp.arange(n_blocks, dtype=jnp.int32)
    block_e = jnp.sum((blk[:, None] * tmb >= pend[None, :]).astype(jnp.int32), axis=1)
    last_e = jnp.sum((jnp.maximum(n_used - 1, 0) * tmb >= pend).astype(jnp.int32))
    block_e = jnp.clip(jnp.where(blk < n_used, block_e, last_e), 0, e - 1).astype(jnp.int32)
    seg_end = (pend - padded + counts)[block_e]
    block_valid = jnp.clip(seg_end - blk * tmb, 0, tmb).astype(jnp.int32)
    return base.reshape(-1).astype(jnp.int32), block_e, block_valid, n_used.reshape(1)


def _pos_kernel(base_ref, te_ref, rk_ref, w_ref, pos_d_ref, pos_c_ref, w_c_ref, *, n_experts):
    i = pl.program_id(0)
    te = te_ref[...]
    pos = rk_ref[...]
    for e in range(n_experts):
        pos = pos + jnp.where(te == e, base_ref[i * n_experts + e], 0)
    w = w_ref[...]
    for c in range(pos_d_ref.shape[0]):
        pos_d_ref[c] = pos[:, c * DISPATCH_CHUNK:(c + 1) * DISPATCH_CHUNK]
    for c in range(pos_c_ref.shape[0]):
        pos_c_ref[c] = pos[:, c * COMBINE_CHUNK:(c + 1) * COMBINE_CHUNK]
        w_c_ref[c] = w[:, c * COMBINE_CHUNK:(c + 1) * COMBINE_CHUNK]


def _positions(base, top_e, rank, wts, n_experts, tm):
    k, t = top_e.shape
    assert tm % DISPATCH_CHUNK == 0 and tm % COMBINE_CHUNK == 0
    nd, nc = tm // DISPATCH_CHUNK, tm // COMBINE_CHUNK
    spec = pl.BlockSpec((k, tm), lambda i, base_ref: (0, i))
    return pl.pallas_call(
        functools.partial(_pos_kernel, n_experts=n_experts),
        grid_spec=pltpu.PrefetchScalarGridSpec(
            num_scalar_prefetch=1, grid=(t // tm,), in_specs=[spec, spec, spec],
            out_specs=[pl.BlockSpec((nd, k, DISPATCH_CHUNK), lambda i, base_ref: (i, 0, 0)),
                       pl.BlockSpec((nc, k, COMBINE_CHUNK), lambda i, base_ref: (i, 0, 0)),
                       pl.BlockSpec((nc, k, COMBINE_CHUNK), lambda i, base_ref: (i, 0, 0))]),
        out_shape=[jax.ShapeDtypeStruct((t // DISPATCH_CHUNK, k, DISPATCH_CHUNK), I32),
                   jax.ShapeDtypeStruct((t // COMBINE_CHUNK, k, COMBINE_CHUNK), I32),
                   jax.ShapeDtypeStruct((t // COMBINE_CHUNK, k, COMBINE_CHUNK), F32)],
        compiler_params=_cparams(1),
        name="moe_positions",
    )(base, top_e, rank, wts)


def _sc_dispatch(rows, pos3, cap):
    t, w = rows.shape
    n_chunks, top_k, n = pos3.shape
    n_workers = SC_CORES * SC_SUBCORES
    assert n_chunks * n == t and n <= LANES
    per_worker = -(-n_chunks // n_workers)
    mesh = plsc.VectorSubcoreMesh(core_axis_name="core", subcore_axis_name="subcore",
                                  num_cores=SC_CORES, num_subcores=SC_SUBCORES)

    def body(rows_hbm, pos_hbm, out_hbm, idx_v, rows_v, sem):
        wid = lax.axis_index("subcore") * SC_CORES + lax.axis_index("core")

        @pl.loop(0, per_worker)
        def _(j):
            c = wid * per_worker + j

            @pl.when(c < n_chunks)
            def _():
                pltpu.sync_copy(pos_hbm.at[c], idx_v)
                pltpu.sync_copy(rows_hbm.at[pl.ds(c * n, n)], rows_v)
                copies = [pltpu.async_copy(rows_v, out_hbm.at[idx_v.at[k]], sem) for k in range(top_k)]
                for cp in copies:
                    cp.wait()

    return pl.kernel(
        body,
        out_type=jax.ShapeDtypeStruct((cap, w), I32),
        mesh=mesh,
        scratch_types=[pltpu.VMEM((top_k, n), I32), pltpu.VMEM((n, w), I32), pltpu.SemaphoreType.DMA],
        name="moe_dispatch_sc",
    )(rows, pos3)


def _sc_combine(rows, pos3, wts3):
    cap, w = rows.shape
    n_chunks, top_k, n = pos3.shape
    t = n_chunks * n
    n_workers = SC_CORES * SC_SUBCORES
    assert n <= SC_LANES and w % SC_LANES == 0
    per_worker = -(-n_chunks // n_workers)
    mesh = plsc.VectorSubcoreMesh(core_axis_name="core", subcore_axis_name="subcore",
                                  num_cores=SC_CORES, num_subcores=SC_SUBCORES)

    def body(rows_hbm, pos_hbm, wts_hbm, out_hbm, idx_v, wts_v, acc_v, *rest):
        bufs, sem = rest[:top_k], rest[top_k]
        wid = lax.axis_index("subcore") * SC_CORES + lax.axis_index("core")

        @pl.loop(0, per_worker)
        def _(it):
            c = wid * per_worker + it

            @pl.when(c < n_chunks)
            def _():
                pltpu.sync_copy(pos_hbm.at[c], idx_v)
                pltpu.sync_copy(wts_hbm.at[c], wts_v)
                gathers = [pltpu.async_copy(rows_hbm.at[idx_v.at[k]], bufs[k], sem) for k in range(top_k)]
                for g in gathers:
                    g.wait()

                @pl.loop(0, n)
                def _(j):
                    tok = jnp.full((SC_LANES,), j, I32)
                    wk = [plsc.load_gather(wts_v, [jnp.full((SC_LANES,), k, I32), tok]) for k in range(top_k)]

                    @plsc.parallel_loop(0, w // SC_LANES, unroll=4)
                    def _(g):
                        col = g * SC_LANES
                        hi = jnp.zeros((SC_LANES,), F32)
                        lo = jnp.zeros((SC_LANES,), F32)
                        for k in range(top_k):
                            words = bufs[k][j, pl.ds(col, SC_LANES)]
                            hi = hi + wk[k] * plsc.bitcast(words & jnp.int32(-65536), F32)
                            lo = lo + wk[k] * plsc.bitcast(lax.shift_left(words, 16), F32)
                        acc_v[j, pl.ds(col, SC_LANES)] = hi
                        acc_v[j, pl.ds(w + col, SC_LANES)] = lo

                pltpu.sync_copy(acc_v, out_hbm.at[pl.ds(c * n, n)])

    return pl.kernel(
        body,
        out_type=jax.ShapeDtypeStruct((t, 2 * w), F32),
        mesh=mesh,
        scratch_types=[pltpu.VMEM((top_k, n), I32), pltpu.VMEM((top_k, n), F32), pltpu.VMEM((n, 2 * w), F32)]
        + [pltpu.VMEM((n, w), I32) for _ in range(top_k)] + [pltpu.SemaphoreType.DMA],
        compiler_params=pltpu.CompilerParams(needs_layout_passes=False),
        name="moe_combine_sc",
    )(rows, pos3, wts3)


def _gmm_kernel(be_ref, bv_ref, nu_ref, x_ref, *refs, tmb, per_step):
    w_refs, o_ref = refs[:-1], refs[-1]
    i = pl.program_id(0)

    @pl.when(i * per_step < nu_ref[0])
    def _():
        for s in range(per_step):
            wg_ref, wu_ref, wd_ref = w_refs[3 * s:3 * s + 3]
            words = x_ref[s * tmb:(s + 1) * tmb, :]
            half = words.shape[1]
            live = lax.broadcasted_iota(I32, (tmb, half), 0) < bv_ref[i * per_step + s]
            xa, xb = _unpack_halves(jnp.where(live, words, 0))
            g = _dot(xa, wg_ref[:half, :]) + _dot(xb, wg_ref[half:, :])
            u = _dot(xa, wu_ref[:half, :]) + _dot(xb, wu_ref[half:, :])
            o_ref[s * tmb:(s + 1) * tmb, :] = _pack_halves(_dot((_silu(g) * u).astype(BF16), wd_ref[...]))


def _grouped_experts(x_sorted, block_e, block_valid, n_used, wg, wu, wd, layer, tmb):
    cap, half = x_sorted.shape
    d = 2 * half
    ff = wg.shape[3]
    n_blocks = cap // tmb
    per_step = MOE_BLOCKS_PER_STEP if n_blocks % MOE_BLOCKS_PER_STEP == 0 else 1
    n_steps = n_blocks // per_step

    def row_map(i, be, bv, nu):
        return (jnp.minimum(i, (nu[0] + per_step - 1) // per_step - 1), 0)

    w_specs = []
    for s in range(per_step):
        w_map = functools.partial(lambda i, be, bv, nu, s: (layer, be[i * per_step + s], 0, 0), s=s)
        w_specs += [pl.BlockSpec((None, None, d, ff), w_map), pl.BlockSpec((None, None, d, ff), w_map),
                    pl.BlockSpec((None, None, ff, d), w_map)]

    return pl.pallas_call(
        functools.partial(_gmm_kernel, tmb=tmb, per_step=per_step),
        grid_spec=pltpu.PrefetchScalarGridSpec(
            num_scalar_prefetch=3,
            grid=(n_steps,),
            in_specs=[pl.BlockSpec((per_step * tmb, half), row_map)] + w_specs,
            out_specs=pl.BlockSpec((per_step * tmb, half), row_map),
        ),
        out_shape=jax.ShapeDtypeStruct((cap, half), I32),
        compiler_params=_cparams(1),
        name="moe_experts",
    )(block_e, block_valid, n_used, x_sorted, *([wg, wu, wd] * per_step))


def _moe_out_kernel(r_ref, sg_ref, su_ref, sd_ref, x_ref, sh_ref, sc_ref, gate_ref, g_ref, b_ref, o_ref, *, alpha):
    x = x_ref[...]
    u = _modulate(x, sh_ref[...], sc_ref[...]).astype(BF16)
    y = r_ref[...] + _dot((_silu(_dot(u, sg_ref[...])) * _dot(u, su_ref[...])).astype(BF16), sd_ref[...])
    z = alpha * x + gate_ref[...] * y
    o_ref[...] = _normalize(z, LN_EPS) * g_ref[...] + b_ref[...]


def _moe_out_into_kernel(*refs, alpha):
    _moe_out_kernel(*refs[:-2], refs[-1], alpha=alpha)


def _moe_out(routed, sg, su, sd, x, mod_i, ctx_row, ln_g, ln_b, alpha, place=None):
    b, l, d = x.shape
    ff = sg.shape[1]
    tm = min(ROW_TILE, l)
    assert l % tm == 0
    first, total, buffer = place if place is not None else (0, b, None)
    tok = pl.BlockSpec((None, tm, d), lambda bb, t: (bb, t, 0))
    in_specs = [
        tok,
        pl.BlockSpec((d, ff), lambda bb, t: (0, 0)),
        pl.BlockSpec((d, ff), lambda bb, t: (0, 0)),
        pl.BlockSpec((ff, d), lambda bb, t: (0, 0)),
        tok,
        _mod_spec(3, d, ctx_row, 0),
        _mod_spec(4, d, ctx_row, 0),
        _mod_spec(5, d, ctx_row, 0),
        _row_spec(d),
        _row_spec(d),
    ]
    args = [routed, sg, su, sd, x, mod_i, mod_i, mod_i, ln_g, ln_b]
    kern = functools.partial(_moe_out_kernel, alpha=alpha)
    aliases = {}
    if buffer is not None:
        in_specs.append(pl.BlockSpec(memory_space=pl.ANY))
        aliases = {len(args): 0}
        args.append(buffer)
        kern = functools.partial(_moe_out_into_kernel, alpha=alpha)
    return pl.pallas_call(
        kern,
        grid=(b, l // tm),
        in_specs=in_specs,
        out_specs=pl.BlockSpec((None, tm, d), lambda bb, t: (bb + first, t, 0)),
        out_shape=jax.ShapeDtypeStruct((total, l, d), F32),
        input_output_aliases=aliases,
        compiler_params=_cparams(2),
        name="moe_shared_post",
    )(*args)


def _moe_dispatch(shape, routing, n_experts):
    b, l, d = shape
    t = b * l
    tmb = MOE_ROW_TILE
    n_blocks = -(-(t * TOP_K) // tmb) + n_experts
    u2, top_e, wts, rank, cnt = routing
    base, block_e, block_valid, n_used = _expert_layout(cnt, tmb, n_blocks)
    pos_d, pos_c, wts_c = _positions(base, top_e, rank, wts, n_experts, min(ROW_TILE, l))
    x_sorted = _sc_dispatch(u2.reshape(t, d // 2), pos_d, n_blocks * tmb)
    return x_sorted, block_e, block_valid, n_used, pos_c, wts_c


def _moe_experts(shape, dispatched, wg, wu, wd, layer):
    x_sorted, block_e, block_valid, n_used, pos_c, wts_c = dispatched
    y_sorted = _grouped_experts(x_sorted, block_e, block_valid, n_used, wg, wu, wd, layer, MOE_ROW_TILE)
    return y_sorted, _sc_combine(y_sorted, pos_c, wts_c).reshape(shape)


def _after(token, value):
    if token is None:
        return value
    return lax.optimization_barrier((token, value))[1]


def _rope_tables(l, hd, n_q_heads, n_k_heads, rope):
    axis_rot = hd // 2
    qscale = hd ** -0.5
    if rope:
        t = jnp.arange(l, dtype=jnp.int32)
        r = (t // GRID_W).astype(F32)
        col = (t % GRID_W).astype(F32)
        inv = ROPE_BASE ** (-jnp.arange(0, axis_rot, 2, dtype=F32) / axis_rot)
        ar = r[:, None] * inv
        ac = col[:, None] * inv
        ang = jnp.concatenate([ar, ar, ac, ac], axis=-1)
        cos, sin = jnp.cos(ang), jnp.sin(ang)
    else:
        cos, sin = jnp.ones((l, hd), F32), jnp.zeros((l, hd), F32)
    cos_t = jnp.concatenate([jnp.tile(cos, (1, n_q_heads)) * qscale, jnp.tile(cos, (1, n_k_heads))], axis=1)
    sin_t = jnp.concatenate([jnp.tile(sin, (1, n_q_heads)) * qscale, jnp.tile(sin, (1, n_k_heads))], axis=1)
    return cos_t, sin_t


def _rot_columns(w, hd):
    d, n = w.shape
    q = hd // 4
    w4 = w.reshape(d, n // (2 * q), 2, q)
    return jnp.stack([-w4[:, :, 1, :], w4[:, :, 0, :]], axis=2).reshape(d, n)


def kernel(x, c, ctx, c_ctx, w_ada, b_ada, ln_g, ln_b, attn_w_qkv, attn_w_o, attn_sinks, fnet_w,
           router_w, router_bias, exp_w_gate, exp_w_up, exp_w_down, sh_w_gate, sh_w_up, sh_w_down):
    b, l, d = x.shape
    cl = ctx.shape[1]
    depth = w_ada.shape[0]
    n_heads = attn_sinks.shape[1]
    hd = d // n_heads
    qw = n_heads * hd
    kvw = N_KV_HEADS * hd
    alpha = (2.0 * depth) ** 0.25
    gc = d // F_GROUPS

    mp = -(-(b + 1) // 16) * 16
    cvec = jnp.concatenate([c, c_ctx[None, :], jnp.zeros((mp - b - 1, d), F32)], axis=0)
    mod = _ada_all(cvec, w_ada, b_ada).reshape(depth, mp, 6, 1, d)

    def run_stream(x, h, mod, first, count):
        b = count
        token, _ = yield None
        x = _after(token, x)
        window = (first, count)
        for i in range(depth):
            kind = i % N_MIXERS
            j = i // N_MIXERS
            update_ctx = any((m % N_MIXERS) == 0 for m in range(i + 1, depth))
            mod_i = mod[i]
            g1 = ln_g[i, 0].reshape(1, d)
            b1 = ln_b[i, 0].reshape(1, d)
            g2 = ln_g[i, 1].reshape(1, d)
            b2 = ln_b[i, 1].reshape(1, d)
            rwt, rbias = moe_w[i][:2]

            if kind == 0:
                w_all, w_o = attn_w[j]
                q, k, v = _qkv_proj(x, mod_i, None, w_all, cos_l, sin_l, qw, kvw, window)
                q_c, k_c, v_c = _qkv_proj(h, mod_i, b, w_all, cos_c, sin_c, qw, kvw, window)
                o = _attention(q, k, v, k_c, v_c, attn_sinks[j], True)
                x, x_route = _proj_post(o, w_o, x, mod_i, 2, None, g1, b1, alpha, rwt, rbias, window)
                if update_ctx:
                    o_c = _attention(q_c, None, None, k_c, v_c, attn_sinks[j], False)
                    h, h_route = _proj_post(o_c, w_o, h, mod_i, 2, b, g1, b1, alpha, rwt, rbias, window)
            else:
                if window is not None:
                    x, h = x[first:first + count], h[first:first + count]
                streams = [(x, None)] + ([(h, b)] if update_ctx else [])
                outs = []
                for s, ctx_row in streams:
                    c_l, s_l = dft_seq[s.shape[1]]
                    ab = _fnet_a(s, mod_i, ctx_row, cs)
                    outs.append(_fnet_b(c_l, s_l, ab, fnet_wb[j], s, mod_i, 2, ctx_row, g1, b1, alpha, rwt, rbias))
                x, x_route = outs[0]
                if update_ctx:
                    h, h_route = outs[1]

            wg, wu, wd, sg, su, sd = moe_w[i][2:]
            n_experts = rwt.shape[0]
            x_disp = _moe_dispatch(x.shape, x_route, n_experts)
            h_disp = _moe_dispatch(h.shape, h_route, n_experts) if update_ctx else None
            window = None
            token, _ = yield x
            x_disp = (_after(token, x_disp[0]),) + x_disp[1:]
            y_sorted, x_routed = _moe_experts(x.shape, x_disp, wg, wu, wd, i)
            h_routed = _moe_experts(h.shape, h_disp, wg, wu, wd, i)[1] if update_ctx else None
            token, buffer = yield y_sorted
            x_routed = _after(token, x_routed)
            place = (first, total, buffer) if i == depth - 1 else None
            x = _moe_out(x_routed, sg, su, sd, x, mod_i, None, g2, b2, alpha, place)
            if update_ctx:
                h = _moe_out(h_routed, sg, su, sd, h, mod_i, b, g2, b2, alpha)
        return x

    attn_w = []
    for j in range(attn_w_qkv.shape[0]):
        w = attn_w_qkv[j]
        w_all = jnp.concatenate([w, _rot_columns(w[:, :qw + kvw], hd)], axis=1).astype(BF16)
        attn_w.append((w_all, attn_w_o[j].astype(BF16)))
    cos_l, sin_l = _rope_tables(l, hd, n_heads, N_KV_HEADS, True)
    cos_c, sin_c = _rope_tables(cl, hd, n_heads, N_KV_HEADS, False)
    fnet_wb = [fnet_w[j].astype(BF16) for j in range(fnet_w.shape[0])]
    cc, sc = _dft_tables(gc, gc ** -0.5)
    cs = jnp.concatenate([cc, sc], axis=1).astype(BF16)
    dft_seq = {}
    for ls in (l, cl):
        c_l, s_l = _dft_tables(ls, ls ** -0.5)
        dft_seq[ls] = (c_l.astype(BF16), (-s_l).astype(BF16))
    wg_all, wu_all, wd_all = exp_w_gate.astype(BF16), exp_w_up.astype(BF16), exp_w_down.astype(BF16)
    moe_w = [(router_w[i].T, router_bias[i], wg_all, wu_all, wd_all, sh_w_gate[i].astype(BF16),
              sh_w_up[i].astype(BF16), sh_w_down[i].astype(BF16)) for i in range(depth)]

    n_streams = N_STREAMS if b % N_STREAMS == 0 else 1
    bs = b // n_streams
    total = b
    gens = []
    for s in range(n_streams):
        mod_s = jnp.concatenate([mod[:, s * bs:(s + 1) * bs], mod[:, b:b + 1]], axis=1)
        gens.append(run_stream(x, ctx, mod_s, s * bs, bs))
    for g in gens:
        next(g)
    running = list(range(n_streams))
    token = result = None
    while running:
        for s in list(running):
            try:
                token = gens[s].send((token, result))
            except StopIteration as done:
                token = result = done.value
                running.remove(s)
    return result
```
